```python
import math
import jax, jax.numpy as jnp
from jax import lax
import numpy as np

D_MODEL = 2048
BATCH = 8
SEQ = 2048
DEPTH = 1

PLE_DIM = 256
EPS = 1e-6
SSM_EXPAND = 2
D_INNER = SSM_EXPAND * D_MODEL
SSM_HEAD_DIM = 64
SSM_HEADS = D_INNER // SSM_HEAD_DIM
SSM_GROUPS = 8
SSM_STATE = 128
SSM_CONV = 4
SSM_CHUNK = 128
SSM_BC = SSM_GROUPS * SSM_STATE
SSM_CONV_DIM = D_INNER + 2 * SSM_BC
ATTN_HEADS = 16
ATTN_HEAD_DIM = 128
ATTN_KV_GROUPS = 4
ATTN_WIDTH = ATTN_HEADS * ATTN_HEAD_DIM
KV_WIDTH = ATTN_KV_GROUPS * ATTN_HEAD_DIM
CMP_BLOCK = 32
CMP_STRIDE = 16
SEL_BLOCK = 64
N_SEL = 16
WINDOW = 512
Q_BLOCK = 128
SEL_Q_BLOCK = 16
ROPE_THETA = 500000.0
ROPE_DIM = ATTN_HEAD_DIM // 4
D_FF = 5632
FFN_CONV = 3
IN_SIZES = (D_INNER, SSM_CONV_DIM, SSM_HEADS,
            ATTN_WIDTH, KV_WIDTH, KV_WIDTH, KV_WIDTH, KV_WIDTH, KV_WIDTH, KV_WIDTH,
            3 * ATTN_HEADS, D_MODEL, D_MODEL)
N_IN = sum(IN_SIZES)
NEG_INF = -1e30
FORCE_SCORE = 1e4

kernel_name = "hybrid_ssd_nsa_gated_merge_layer"


def rmsnorm(x, w):
    xf = x.astype(jnp.float32)
    y = xf * lax.rsqrt(jnp.mean(xf * xf, axis=-1, keepdims=True) + EPS)
    return (y * w.astype(jnp.float32)).astype(x.dtype)


def causal_dwconv(x, w, b):
    width = w.shape[0]
    xp = jnp.pad(x, ((0, 0), (width - 1, 0), (0, 0)))
    y = lax.conv_general_dilated(xp, w[:, None, :], window_strides=(1,), padding='VALID',
                                 dimension_numbers=('NWC', 'WIO', 'NWC'),
                                 feature_group_count=x.shape[-1])
    return y + b


def rope_partial(x, pos):
    half = ROPE_DIM // 2
    inv = ROPE_THETA ** (-(jnp.arange(half, dtype=jnp.float32) * 2.0 / ROPE_DIM))
    ang = pos.astype(jnp.float32)[:, None] * inv[None, :]
    cos = jnp.cos(ang)[None, :, None, :]
    sin = jnp.sin(ang)[None, :, None, :]
    xf = x.astype(jnp.float32)
    x1 = xf[..., :half]
    x2 = xf[..., half:ROPE_DIM]
    out = jnp.concatenate([x1 * cos - x2 * sin, x2 * cos + x1 * sin, xf[..., ROPE_DIM:]], axis=-1)
    return out.astype(x.dtype)


def ssd_chunked(xh, dt, a, bm, cm):
    f32 = jnp.float32
    Bsz, S, H, P = xh.shape
    G, N = bm.shape[-2:]
    R = H // G
    L = SSM_CHUNK
    nc = S // L
    x = (xh.astype(f32) * dt[..., None]).reshape(Bsz, nc, L, G, R, P)
    adt = (dt * a).reshape(Bsz, nc, L, G, R)
    bm = bm.astype(f32).reshape(Bsz, nc, L, G, N)
    cm = cm.astype(f32).reshape(Bsz, nc, L, G, N)
    acum = jnp.moveaxis(jnp.cumsum(adt, axis=2), 2, -1)
    tril = jnp.tril(jnp.ones((L, L), dtype=bool))
    seg = acum[..., :, None] - acum[..., None, :]
    decay = jnp.exp(jnp.where(tril, seg, -jnp.inf))
    cb = jnp.einsum('bclgn,bcsgn->bcgls', cm, bm)
    y_diag = jnp.einsum('bcgls,bcgrls,bcsgrp->bclgrp', cb, decay, x)
    decay_states = jnp.exp(acum[..., -1:] - acum)
    states = jnp.einsum('bclgn,bcgrl,bclgrp->bcgrpn', bm, decay_states, x)
    chunk_decay = jnp.exp(acum[..., -1])

    def step(h, inp):
        st, dec = inp
        return h * dec[..., None, None] + st, h

    h0 = jnp.zeros((Bsz, G, R, P, N), f32)
    _, prev = lax.scan(step, h0, (jnp.moveaxis(states, 1, 0), jnp.moveaxis(chunk_decay, 1, 0)))
    prev = jnp.moveaxis(prev, 0, 1)
    y_off = jnp.einsum('bclgn,bcgrpn,bcgrl->bclgrp', cm, prev, jnp.exp(acum))
    return (y_diag + y_off).reshape(Bsz, S, H, P)


def mamba2_mixer(z, xbc, dt_raw, conv_w, conv_b, dt_bias, a_log, d_skip, norm_w):
    f32 = jnp.float32
    Bsz, S, _ = z.shape
    xbc = jax.nn.silu(causal_dwconv(xbc, conv_w, conv_b))
    xs = xbc[..., :D_INNER].reshape(Bsz, S, SSM_HEADS, SSM_HEAD_DIM)
    bm = xbc[..., D_INNER:D_INNER + SSM_BC].reshape(Bsz, S, SSM_GROUPS, SSM_STATE)
    cm = xbc[..., D_INNER + SSM_BC:].reshape(Bsz, S, SSM_GROUPS, SSM_STATE)
    dt = jax.nn.softplus(dt_raw.astype(f32) + dt_bias.astype(f32))
    a = -jnp.exp(a_log.astype(f32))
    y = ssd_chunked(xs, dt, a, bm, cm) + d_skip.astype(f32)[:, None] * xs.astype(f32)
    y = y.reshape(Bsz, S, D_INNER) * jax.nn.silu(z.astype(f32))
    yg = y.reshape(Bsz, S, SSM_GROUPS, D_INNER // SSM_GROUPS)
    yg = yg * lax.rsqrt(jnp.mean(yg * yg, axis=-1, keepdims=True) + EPS)
    y = yg.reshape(Bsz, S, D_INNER) * norm_w.astype(f32)
    return y.astype(z.dtype)


def compress_tokens(t, pe, w1, w2):
    Bsz, S, G, d = t.shape
    n_sub = CMP_BLOCK // CMP_STRIDE
    sub = t.reshape(Bsz, S // CMP_STRIDE, CMP_STRIDE, G, d)
    nc = S // CMP_STRIDE - n_sub + 1
    blocks = jnp.concatenate([sub[:, j:j + nc] for j in range(n_sub)], axis=2)
    blocks = blocks + pe[None, None, :, None, :]
    flat = jnp.moveaxis(blocks, 3, 2).reshape(Bsz, nc, G, CMP_BLOCK * d)
    return jax.nn.silu(flat @ w1) @ w2


def cmp_to_sel_map(nc, ns):
    i = np.arange(nc)[:, None]
    j = np.arange(ns)[None, :]
    c_start = i * CMP_STRIDE
    s_start = j * SEL_BLOCK
    m = (c_start < s_start + SEL_BLOCK) & (c_start + CMP_BLOCK > s_start)
    return jnp.asarray(m.astype(np.float32))


def nsa_mixer(q, k_c, v_c, k_s, v_s, k_w, v_w, g_branch, pe_k, pe_v, wk1, wk2, wv1, wv2):
    f32 = jnp.float32
    Bsz, S, _ = q.shape
    G, d = ATTN_KV_GROUPS, ATTN_HEAD_DIM
    R = ATTN_HEADS // G
    scale = 1.0 / math.sqrt(d)
    pos = jnp.arange(S)
    qf = rope_partial(q.reshape(Bsz, S, ATTN_HEADS, d), pos).astype(f32) * scale
    qf = qf.reshape(Bsz, S, G, R, d)
    kvs = lambda t: t.reshape(Bsz, S, G, d)
    k_c = rope_partial(kvs(k_c), pos)
    k_s = rope_partial(kvs(k_s), pos).astype(f32)
    k_w = rope_partial(kvs(k_w), pos).astype(f32)
    v_s = kvs(v_s).astype(f32)
    v_w = kvs(v_w).astype(f32)

    kc = compress_tokens(k_c, pe_k, wk1, wk2).astype(f32)
    vc = compress_tokens(kvs(v_c), pe_v, wv1, wv2).astype(f32)
    nc = kc.shape[1]
    sc = jnp.einsum('bsgrd,bngd->bsgrn', qf, kc)
    cmask = (jnp.arange(nc) * CMP_STRIDE + CMP_BLOCK - 1)[None, :] <= pos[:, None]
    cmask = cmask[None, :, None, None, :]
    p_cmp = jax.nn.softmax(jnp.where(cmask, sc, NEG_INF), axis=-1) * cmask
    o_cmp = jnp.einsum('bsgrn,bngd->bsgrd', p_cmp, vc)

    ns = S // SEL_BLOCK
    imp = jnp.einsum('bsgrn,nj->bsgj', p_cmp, cmp_to_sel_map(nc, ns))
    blk = jnp.arange(ns)[None, :]
    cur = (pos // SEL_BLOCK)[:, None]
    causal_blk = (blk <= cur)[None, :, None, :]
    forced = ((blk == 0) | (blk == cur) | (blk == cur - 1))[None, :, None, :]
    score = jnp.where(forced, FORCE_SCORE, jnp.where(causal_blk, imp, NEG_INF))
    n_top = min(N_SEL, ns)
    _, sel_idx = lax.top_k(score, n_top)
    kb = k_s.reshape(Bsz, ns, SEL_BLOCK, G, d).transpose(0, 3, 1, 2, 4)
    vb = v_s.reshape(Bsz, ns, SEL_BLOCK, G, d).transpose(0, 3, 1, 2, 4)
    bi = jnp.arange(Bsz)[:, None, None, None]
    gi = jnp.arange(G)[None, None, :, None]
    off = jnp.arange(SEL_BLOCK)

    def sel_block(args):
        qb, ib, tb = args
        kg = kb[bi, gi, ib]
        vg = vb[bi, gi, ib]
        s = jnp.einsum('bqgrd,bqgkld->bqgrkl', qb, kg)
        kpos = ib[..., None] * SEL_BLOCK + off
        m = (kpos <= tb[None, :, None, None, None])[:, :, :, None]
        s = jnp.where(m, s, NEG_INF).reshape(Bsz, SEL_Q_BLOCK, G, R, n_top * SEL_BLOCK)
        pr = jax.nn.softmax(s, axis=-1).reshape(Bsz, SEL_Q_BLOCK, G, R, n_top, SEL_BLOCK)
        return jnp.einsum('bqgrkl,bqgkld->bqgrd', pr, vg)

    nqs = S // SEL_Q_BLOCK
    qs = qf.reshape(Bsz, nqs, SEL_Q_BLOCK, G, R, d).swapaxes(0, 1)
    isb = sel_idx.reshape(Bsz, nqs, SEL_Q_BLOCK, G, n_top).swapaxes(0, 1)
    tsb = pos.reshape(nqs, SEL_Q_BLOCK)
    o_sel = lax.map(sel_block, (qs, isb, tsb)).swapaxes(0, 1).reshape(Bsz, S, G, R, d)

    kwp = jnp.pad(k_w, ((0, 0), (WINDOW, 0), (0, 0), (0, 0)))
    vwp = jnp.pad(v_w, ((0, 0), (WINDOW, 0), (0, 0), (0, 0)))
    span = WINDOW + Q_BLOCK
    qi = jnp.arange(Q_BLOCK)[:, None]
    kj = jnp.arange(span)[None, :]
    diff = qi + WINDOW - kj

    def win_block(args):
        qb, n = args
        start = n * Q_BLOCK
        kk = lax.dynamic_slice_in_dim(kwp, start, span, axis=1)
        vv = lax.dynamic_slice_in_dim(vwp, start, span, axis=1)
        s = jnp.einsum('bqgrd,bkgd->bqgrk', qb, kk)
        m = (diff >= 0) & (diff < WINDOW) & (start - WINDOW + kj >= 0)
        m = m[None, :, None, None, :]
        pr = jax.nn.softmax(jnp.where(m, s, NEG_INF), axis=-1)
        return jnp.einsum('bqgrk,bkgd->bqgrd', pr, vv)

    nqw = S // Q_BLOCK
    qw = qf.reshape(Bsz, nqw, Q_BLOCK, G, R, d).swapaxes(0, 1)
    o_win = lax.map(win_block, (qw, jnp.arange(nqw))).swapaxes(0, 1).reshape(Bsz, S, G, R, d)

    g = jax.nn.sigmoid(g_branch.astype(f32)).reshape(Bsz, S, 3, G, R)[..., None]
    o = g[:, :, 0] * o_cmp + g[:, :, 1] * o_sel + g[:, :, 2] * o_win
    return o.reshape(Bsz, S, ATTN_WIDTH).astype(q.dtype)


def setup_inputs(seed: int = 0) -> dict:
    key = jax.random.key(seed)
    keys = iter(jax.random.split(key, 48))
    nrm = lambda shape, s: jax.random.normal(next(keys), shape, jnp.float32) * s
    gain = lambda shape: 1.0 + nrm(shape, 0.05)
    L = DEPTH
    x = nrm((BATCH, SEQ, D_MODEL), 1.0)
    p = nrm((DEPTH, BATCH, SEQ, PLE_DIM), 1.0)
    dt0 = jnp.exp(jax.random.uniform(next(keys), (L, SSM_HEADS), jnp.float32,
                                     minval=math.log(1e-3), maxval=math.log(1e-1)))
    ssm_dt_bias = dt0 + jnp.log(-jnp.expm1(-dt0))
    ssm_a_log = jnp.log(jax.random.uniform(next(keys), (L, SSM_HEADS), jnp.float32, minval=1.0, maxval=16.0))
    return {
        "x": x,
        "p": p,
        "norm_mix_w": gain((L, D_MODEL)),
        "w_in": nrm((L, D_MODEL, N_IN), D_MODEL ** -0.5),
        "ssm_conv_w": nrm((L, SSM_CONV, SSM_CONV_DIM), SSM_CONV ** -0.5),
        "ssm_conv_b": nrm((L, SSM_CONV_DIM), 0.01),
        "ssm_dt_bias": ssm_dt_bias,
        "ssm_a_log": ssm_a_log,
        "ssm_d": gain((L, SSM_HEADS)),
        "ssm_norm_w": gain((L, D_INNER)),
        "cmp_pe_k": nrm((L, CMP_BLOCK, ATTN_HEAD_DIM), 0.02),
        "cmp_pe_v": nrm((L, CMP_BLOCK, ATTN_HEAD_DIM), 0.02),
        "cmp_wk1": nrm((L, CMP_BLOCK * ATTN_HEAD_DIM, ATTN_HEAD_DIM), (CMP_BLOCK * ATTN_HEAD_DIM) ** -0.5),
        "cmp_wk2": nrm((L, ATTN_HEAD_DIM, ATTN_HEAD_DIM), ATTN_HEAD_DIM ** -0.5),
        "cmp_wv1": nrm((L, CMP_BLOCK * ATTN_HEAD_DIM, ATTN_HEAD_DIM), (CMP_BLOCK * ATTN_HEAD_DIM) ** -0.5),
        "cmp_wv2": nrm((L, ATTN_HEAD_DIM, ATTN_HEAD_DIM), ATTN_HEAD_DIM ** -0.5),
        "w_ssm_branch": nrm((L, D_INNER, D_MODEL), D_INNER ** -0.5),
        "w_attn_branch": nrm((L, ATTN_WIDTH, D_MODEL), ATTN_WIDTH ** -0.5),
        "w_mix_out": nrm((L, D_MODEL, D_MODEL), D_MODEL ** -0.5),
        "norm_ffn_w": gain((L, D_MODEL)),
        "ffn_w_gate": nrm((L, D_MODEL, D_FF), D_MODEL ** -0.5),
        "ffn_w_up": nrm((L, D_MODEL, D_FF), D_MODEL ** -0.5),
        "ffn_conv_w": nrm((L, FFN_CONV, D_FF), FFN_CONV ** -0.5),
        "ffn_conv_b": nrm((L, D_FF), 0.01),
        "ffn_w_down": nrm((L, D_FF, D_MODEL), D_FF ** -0.5),
        "ple_norm_w": gain((L, D_MODEL)),
        "ple_w_gate": nrm((L, D_MODEL, D_MODEL), D_MODEL ** -0.5),
        "ple_w_proj": nrm((L, PLE_DIM, D_MODEL), PLE_DIM ** -0.5),
        "final_norm_w": gain((D_MODEL,)),
    }


def reference(x, p, norm_mix_w, w_in, ssm_conv_w, ssm_conv_b, ssm_dt_bias, ssm_a_log, ssm_d,
              ssm_norm_w, cmp_pe_k, cmp_pe_v, cmp_wk1, cmp_wk2, cmp_wv1, cmp_wv2,
              w_ssm_branch, w_attn_branch, w_mix_out, norm_ffn_w, ffn_w_gate, ffn_w_up,
              ffn_conv_w, ffn_conv_b, ffn_w_down, ple_norm_w, ple_w_gate, ple_w_proj,
              final_norm_w):
    for i in range(DEPTH):
        h = rmsnorm(x, norm_mix_w[i])
        w = w_in[i]
        parts = []
        col = 0
        for sz in IN_SIZES:
            parts.append(h @ w[:, col:col + sz])
            col += sz
        (z, xbc, dt_raw, q, k_c, v_c, k_s, v_s, k_w, v_w,
         g_nsa, g_ssm_merge, g_attn_merge) = parts
        y_ssm = mamba2_mixer(z, xbc, dt_raw, ssm_conv_w[i], ssm_conv_b[i], ssm_dt_bias[i],
                             ssm_a_log[i], ssm_d[i], ssm_norm_w[i])
        y_attn = nsa_mixer(q, k_c, v_c, k_s, v_s, k_w, v_w, g_nsa, cmp_pe_k[i], cmp_pe_v[i],
                           cmp_wk1[i], cmp_wk2[i], cmp_wv1[i], cmp_wv2[i])
        merged = (jax.nn.sigmoid(g_ssm_merge) * (y_ssm @ w_ssm_branch[i])
                  + jax.nn.sigmoid(g_attn_merge) * (y_attn @ w_attn_branch[i]))
        x = x + merged @ w_mix_out[i]
        h = rmsnorm(x, norm_ffn_w[i])
        a = causal_dwconv(h @ ffn_w_gate[i], ffn_conv_w[i], ffn_conv_b[i])
        x = x + (jax.nn.silu(a) * (h @ ffn_w_up[i])) @ ffn_w_down[i]
        ple_gate = jax.nn.sigmoid(rmsnorm(x, ple_norm_w[i]) @ ple_w_gate[i])
        x = x + ple_gate * (p[i] @ ple_w_proj[i])
    return rmsnorm(x, final_norm_w)
```

```python
import functools
import math

import jax
import jax.numpy as jnp
from jax import lax
from jax.experimental import pallas as pl
from jax.experimental.pallas import tpu as pltpu

F32 = jnp.float32
BF16 = jnp.bfloat16

D_MODEL = 2048
PLE_DIM = 256
EPS = 1e-6
D_INNER = 4096
SSM_HEAD_DIM = 64
SSM_HEADS = 64
SSM_GROUPS = 8
SSM_STATE = 128
SSM_CONV = 4
SSM_CHUNK = 128
SSM_BC = SSM_GROUPS * SSM_STATE
SSM_CONV_DIM = D_INNER + 2 * SSM_BC
ATTN_HEADS = 16
ATTN_HEAD_DIM = 128
ATTN_KV_GROUPS = 4
ATTN_REP = ATTN_HEADS // ATTN_KV_GROUPS
ATTN_WIDTH = ATTN_HEADS * ATTN_HEAD_DIM
KV_WIDTH = ATTN_KV_GROUPS * ATTN_HEAD_DIM
CMP_BLOCK = 32
CMP_STRIDE = 16
SEL_BLOCK = 64
N_SEL = 16
WINDOW = 512
ROPE_THETA = 500000.0
ROPE_DIM = ATTN_HEAD_DIM // 4
D_FF = 5632
FFN_CONV = 3
NEG_INF = -1e30
FORCE_SCORE = 1e4

COL_Z = 0
COL_XBC = COL_Z + D_INNER
COL_Q = COL_XBC + SSM_CONV_DIM
COL_KV = COL_Q + ATTN_WIDTH
COL_GS = COL_KV + 6 * KV_WIDTH
COL_GA = COL_GS + D_MODEL
N_WIDE = COL_GA + D_MODEL
N_SMALL = 128

VMEM_LIMIT = 56 * 1024 * 1024
TAIL = 8


def _cparams(sem):
    return pltpu.CompilerParams(dimension_semantics=sem, vmem_limit_bytes=VMEM_LIMIT)


def _rms(xf, w):
    return xf * lax.rsqrt(jnp.mean(xf * xf, axis=-1, keepdims=True) + EPS) * w


def _silu(v):
    return v * jax.nn.sigmoid(v)


def _norm_small_kernel(x_ref, nw_ref, ws_ref, h_ref, small_ref):
    h = _rms(x_ref[...], nw_ref[...])
    h_ref[...] = h.astype(BF16)
    small_ref[...] = jnp.dot(h, ws_ref[...], preferred_element_type=F32, precision=lax.Precision.HIGHEST)


def _norm_small(x2d, norm_w, w_small, tm=512):
    T = x2d.shape[0]
    return pl.pallas_call(
        _norm_small_kernel,
        grid=(T // tm,),
        in_specs=[pl.BlockSpec((tm, D_MODEL), lambda i: (i, 0)),
                  pl.BlockSpec((1, D_MODEL), lambda i: (0, 0)),
                  pl.BlockSpec((D_MODEL, N_SMALL), lambda i: (0, 0))],
        out_specs=[pl.BlockSpec((tm, D_MODEL), lambda i: (i, 0)),
                   pl.BlockSpec((tm, N_SMALL), lambda i: (i, 0))],
        out_shape=[jax.ShapeDtypeStruct((T, D_MODEL), BF16),
                   jax.ShapeDtypeStruct((T, N_SMALL), F32)],
        compiler_params=_cparams(("parallel",)),
        name="norm_small",
    )(x2d, norm_w, w_small)


def _matmul_kernel(a_ref, b_ref, o_ref):
    o_ref[...] = jnp.dot(a_ref[...], b_ref[...], preferred_element_type=F32).astype(o_ref.dtype)


def _matmul(a, b, out_dtype, tm=1024, tn=1024):
    M, K = a.shape
    N = b.shape[1]
    return pl.pallas_call(
        _matmul_kernel,
        grid=(N // tn, M // tm),
        in_specs=[pl.BlockSpec((tm, K), lambda n, m: (m, 0)),
                  pl.BlockSpec((K, tn), lambda n, m: (0, n))],
        out_specs=pl.BlockSpec((tm, tn), lambda n, m: (m, n)),
        out_shape=jax.ShapeDtypeStruct((M, N), out_dtype),
        compiler_params=_cparams(("parallel", "parallel")),
        name="in_proj",
    )(a, b)


def _rope_kernel(x_ref, ca_ref, cb_ref, cc_ref, o_ref, *, scale):
    j = pl.program_id(1)
    sc = jnp.where(j < ATTN_REP, scale, 1.0).astype(F32)
    ca, cb, cc = ca_ref[...], cb_ref[...], cc_ref[...]
    d = ATTN_HEAD_DIM
    half = ROPE_DIM // 2
    for r in range(KV_WIDTH // d):
        xh = x_ref[:, r * d:(r + 1) * d].astype(F32)
        up = pltpu.roll(xh, d - half, 1)
        dn = pltpu.roll(xh, half, 1)
        o_ref[:, r * d:(r + 1) * d] = ((xh * ca + up * cb + dn * cc) * sc).astype(o_ref.dtype)


def _rope_tables(S):
    half = ROPE_DIM // 2
    pos = jnp.arange(S)
    inv = ROPE_THETA ** (-(jnp.arange(half, dtype=F32) * 2.0 / ROPE_DIM))
    ang = pos.astype(F32)[:, None] * inv[None, :]
    cos, sin = jnp.cos(ang), jnp.sin(ang)
    rest = ATTN_HEAD_DIM - ROPE_DIM
    ca = jnp.concatenate([cos, cos, jnp.ones((S, rest), F32)], axis=1)
    cb = jnp.concatenate([-sin, jnp.zeros((S, ATTN_HEAD_DIM - half), F32)], axis=1)
    cc = jnp.concatenate([jnp.zeros((S, half), F32), sin, jnp.zeros((S, rest), F32)], axis=1)
    return ca, cb, cc


N_ROPE_BLOCKS = ATTN_REP + 3
ROPE_KC, ROPE_KS, ROPE_KW = ATTN_WIDTH, ATTN_WIDTH + KV_WIDTH, ATTN_WIDTH + 2 * KV_WIDTH


def _rope(proj, S, ts=512):
    T = proj.shape[0]
    ca, cb, cc = _rope_tables(S)
    wb = KV_WIDTH
    q0 = COL_Q // wb
    k0 = COL_KV // wb

    def in_map(i, j):
        return (i, jnp.where(j < ATTN_REP, q0 + j, k0 + 2 * (j - ATTN_REP)))

    tab = pl.BlockSpec((ts, ATTN_HEAD_DIM), lambda i, j: (i % (S // ts), 0))
    return pl.pallas_call(
        functools.partial(_rope_kernel, scale=1.0 / math.sqrt(ATTN_HEAD_DIM)),
        grid=(T // ts, N_ROPE_BLOCKS),
        in_specs=[pl.BlockSpec((ts, wb), in_map), tab, tab, tab],
        out_specs=pl.BlockSpec((ts, wb), lambda i, j: (i, j)),
        out_shape=jax.ShapeDtypeStruct((T, N_ROPE_BLOCKS * wb), BF16),
        compiler_params=_cparams(("parallel", "parallel")),
        name="rope",
    )(proj, ca, cb, cc)


def _ssd_kernel(x_ref, b_ref, c_ref, z_ref, wx_ref, wb_ref, wc_ref, bx_ref, bb_ref, bc_ref,
                dtr_ref, dtrT_ref, dtb_ref, dtbT_ref, alog_ref, alogT_ref, dsk_ref, nw_ref,
                y_ref, h_scr, xbuf, bbuf, cbuf):
    L = SSM_CHUNK
    P = SSM_HEAD_DIM
    c = pl.program_id(2)

    @pl.when(c == 0)
    def _():
        h_scr[...] = jnp.zeros_like(h_scr)
        xbuf[0:TAIL, :] = jnp.zeros((TAIL, xbuf.shape[1]), F32)
        bbuf[0:TAIL, :] = jnp.zeros((TAIL, bbuf.shape[1]), F32)
        cbuf[0:TAIL, :] = jnp.zeros((TAIL, cbuf.shape[1]), F32)

    def conv_silu(buf, src_ref, w_ref, bias_ref):
        buf[TAIL:TAIL + L, :] = src_ref[...].astype(F32)
        acc = bias_ref[...]
        for k in range(SSM_CONV):
            off = TAIL - (SSM_CONV - 1) + k
            acc = acc + w_ref[k:k + 1, :] * buf[off:off + L, :]
        buf[0:TAIL, :] = buf[L:L + TAIL, :]
        return _silu(acc)

    xs = conv_silu(xbuf, x_ref, wx_ref, bx_ref)
    bm = conv_silu(bbuf, b_ref, wb_ref, bb_ref)
    cm = conv_silu(cbuf, c_ref, wc_ref, bc_ref)

    dt = jax.nn.softplus(dtr_ref[0, 0] + dtb_ref[0])
    dtT = jax.nn.softplus(dtrT_ref[0] + dtbT_ref[0])
    a = -jnp.exp(alog_ref[0])
    aT = -jnp.exp(alogT_ref[0])
    ri = lax.broadcasted_iota(jnp.int32, (L, L), 0)
    ci = lax.broadcasted_iota(jnp.int32, (L, L), 1)
    tril = ci <= ri
    hp = lax.Precision.HIGHEST
    acum = jnp.dot(tril.astype(F32), dt * a, preferred_element_type=F32, precision=hp)
    acumT = jnp.dot(dtT * aT, (ri <= ci).astype(F32), preferred_element_type=F32, precision=hp)

    cmb = cm.astype(BF16)
    cb = lax.dot_general(cmb, bm.astype(BF16), (((1,), (1,)), ((), ())), preferred_element_type=F32)
    yoff = jnp.dot(cmb, h_scr[...].astype(BF16), preferred_element_type=F32)

    lane = lax.broadcasted_iota(jnp.int32, (L, 2 * P), 1)
    lo = lane < P

    def pair(v0, v1):
        return jnp.where(lo, v0, v1)

    dsk = dsk_ref[...]
    ys, xds, cds = [], [], []
    for pr in range(SSM_HEADS // SSM_GROUPS // 2):
        r0, r1 = 2 * pr, 2 * pr + 1
        sl = slice(pr * 2 * P, (pr + 1) * 2 * P)
        xp = xs[:, sl]
        xdt = xp * pair(dt[:, r0:r0 + 1], dt[:, r1:r1 + 1])
        xdtb = xdt.astype(BF16)
        yd = []
        for r in (r0, r1):
            seg = acum[:, r:r + 1] - acumT[r:r + 1, :]
            dec = jnp.exp(jnp.where(tril, seg, -jnp.inf))
            yd.append(jnp.dot((cb * dec).astype(BF16), xdtb, preferred_element_type=F32))
        ea = pair(jnp.exp(acum[:, r0:r0 + 1]), jnp.exp(acum[:, r1:r1 + 1]))
        ys.append(pair(yd[0], yd[1]) + yoff[:, sl] * ea + dsk[:, sl] * xp)
        al0, al1 = acum[L - 1:L, r0:r0 + 1], acum[L - 1:L, r1:r1 + 1]
        xds.append(xdt * pair(jnp.exp(al0 - acum[:, r0:r0 + 1]), jnp.exp(al1 - acum[:, r1:r1 + 1])))
        cds.append(jnp.where(lo[0:1, :], jnp.exp(al0), jnp.exp(al1)))
    xds = jnp.concatenate(xds, axis=1).astype(BF16)
    cd = jnp.concatenate(cds, axis=1)
    st = jnp.dot(bm.T.astype(BF16), xds, preferred_element_type=F32)
    h_scr[...] = h_scr[...] * cd + st

    y = jnp.concatenate(ys, axis=1) * _silu(z_ref[...].astype(F32))
    y_ref[...] = _rms(y, nw_ref[...]).astype(y_ref.dtype)


def _ssd(proj, small, conv_w, conv_b, dt_bias, a_log, d_skip, norm_w, B, S):
    T = B * S
    L, G, N = SSM_CHUNK, SSM_GROUPS, SSM_STATE
    R = SSM_HEADS // G
    gw = D_INNER // G
    nc = S // L
    dtr = small[:, :SSM_HEADS].reshape(B, S, G, R)
    dtr_g = dtr.transpose(0, 2, 1, 3)
    dtr_t = dtr.reshape(B, S, SSM_HEADS).transpose(0, 2, 1)
    dtb = dt_bias.reshape(G, 1, R)
    dtbT = dt_bias.reshape(G, R, 1)
    alog = a_log.reshape(G, 1, R)
    alogT = a_log.reshape(G, R, 1)
    dsk = jnp.repeat(d_skip, SSM_HEAD_DIM).reshape(1, D_INNER)
    cw = conv_w
    cbias = conv_b.reshape(1, SSM_CONV_DIM)
    nw = norm_w.reshape(1, D_INNER)

    row = lambda b, g, c: b * nc + c
    xb0 = COL_XBC // gw
    bb0 = (COL_XBC + D_INNER) // N
    cb0 = (COL_XBC + D_INNER + SSM_BC) // N
    in_specs = [
        pl.BlockSpec((L, gw), lambda b, g, c: (row(b, g, c), xb0 + g)),
        pl.BlockSpec((L, N), lambda b, g, c: (row(b, g, c), bb0 + g)),
        pl.BlockSpec((L, N), lambda b, g, c: (row(b, g, c), cb0 + g)),
        pl.BlockSpec((L, gw), lambda b, g, c: (row(b, g, c), g)),
        pl.BlockSpec((SSM_CONV, gw), lambda b, g, c: (0, g)),
        pl.BlockSpec((SSM_CONV, N), lambda b, g, c: (0, D_INNER // N + g)),
        pl.BlockSpec((SSM_CONV, N), lambda b, g, c: (0, (D_INNER + SSM_BC) // N + g)),
        pl.BlockSpec((1, gw), lambda b, g, c: (0, g)),
        pl.BlockSpec((1, N), lambda b, g, c: (0, D_INNER // N + g)),
        pl.BlockSpec((1, N), lambda b, g, c: (0, (D_INNER + SSM_BC) // N + g)),
        pl.BlockSpec((1, 1, L, R), lambda b, g, c: (b, g, c, 0)),
        pl.BlockSpec((1, R, L), lambda b, g, c: (b, g, c)),
        pl.BlockSpec((1, 1, R), lambda b, g, c: (g, 0, 0)),
        pl.BlockSpec((1, R, 1), lambda b, g, c: (g, 0, 0)),
        pl.BlockSpec((1, 1, R), lambda b, g, c: (g, 0, 0)),
        pl.BlockSpec((1, R, 1), lambda b, g, c: (g, 0, 0)),
        pl.BlockSpec((1, gw), lambda b, g, c: (0, g)),
        pl.BlockSpec((1, gw), lambda b, g, c: (0, g)),
    ]
    return pl.pallas_call(
        _ssd_kernel,
        grid=(B, G, nc),
        in_specs=in_specs,
        out_specs=pl.BlockSpec((L, gw), lambda b, g, c: (row(b, g, c), g)),
        out_shape=jax.ShapeDtypeStruct((T, D_INNER), BF16),
        scratch_shapes=[pltpu.VMEM((N, gw), F32),
                        pltpu.VMEM((L + TAIL, gw), F32),
                        pltpu.VMEM((L + TAIL, N), F32),
                        pltpu.VMEM((L + TAIL, N), F32)],
        compiler_params=_cparams(("parallel", "parallel", "arbitrary")),
        name="ssd",
    )(proj, proj, proj, proj, cw, cw, cw, cbias, cbias, cbias,
      dtr_g, dtr_t, dtb, dtbT, alog, alogT, dsk, nw)


def _compress_kernel(sub_ref, pe_ref, w1a_ref, w1b_ref, w2_ref, o_ref):
    sub = sub_ref[0, 0].astype(F32)
    nsub = sub.shape[0]
    u = jnp.dot((sub + pe_ref[0:1, :]).astype(BF16), w1a_ref[...], preferred_element_type=F32)
    v = jnp.dot((sub + pe_ref[1:2, :]).astype(BF16), w1b_ref[...], preferred_element_type=F32)
    pre = u + pltpu.roll(v, nsub - 1, 0)
    o_ref[0, 0] = jnp.dot(_silu(pre).astype(BF16), w2_ref[...], preferred_element_type=F32).astype(o_ref.dtype)


def _compress(tok, pe, w1, w2, B, S):
    G, d = ATTN_KV_GROUPS, ATTN_HEAD_DIM
    nsub = S // CMP_STRIDE
    wsub = CMP_STRIDE * d
    sub = tok.reshape(B, S, G, d).transpose(0, 2, 1, 3).reshape(B, G, nsub, wsub)
    pe2 = pe.reshape(CMP_BLOCK // CMP_STRIDE, wsub)
    w1b16 = w1.astype(BF16)
    return pl.pallas_call(
        _compress_kernel,
        grid=(B, G),
        in_specs=[pl.BlockSpec((1, 1, nsub, wsub), lambda b, g: (b, g, 0, 0)),
                  pl.BlockSpec((CMP_BLOCK // CMP_STRIDE, wsub), lambda b, g: (0, 0)),
                  pl.BlockSpec((wsub, d), lambda b, g: (0, 0)),
                  pl.BlockSpec((wsub, d), lambda b, g: (1, 0)),
                  pl.BlockSpec((d, d), lambda b, g: (0, 0))],
        out_specs=pl.BlockSpec((1, 1, nsub, d), lambda b, g: (b, g, 0, 0)),
        out_shape=jax.ShapeDtypeStruct((B, G, nsub, d), BF16),
        compiler_params=_cparams(("parallel", "parallel")),
        name="compress",
    )(sub, pe2, w1b16, w1b16, w2.astype(BF16))


def _cmp_kernel(q_ref, kc_ref, vc_ref, gate_ref, o_ref, sel_ref, *, tq, n_cmp, n_blk):
    i = pl.program_id(2)
    d = ATTN_HEAD_DIM
    kc = kc_ref[0, 0]
    vc = vc_ref[0, 0]
    nk = kc.shape[0]
    t = i * tq + lax.broadcasted_iota(jnp.int32, (tq, nk), 0)
    n = lax.broadcasted_iota(jnp.int32, (tq, nk), 1)
    cmask = (n * CMP_STRIDE + CMP_BLOCK - 1 <= t) & (n < n_cmp)
    gate = jax.nn.sigmoid(gate_ref[0, 0])
    psum = jnp.zeros((tq, nk), F32)
    for r in range(ATTN_REP):
        q = q_ref[:, r * d:(r + 1) * d]
        s = lax.dot_general(q, kc, (((1,), (1,)), ((), ())), preferred_element_type=F32)
        s = jnp.where(cmask, s, NEG_INF)
        e = jnp.exp(s - jnp.max(s, axis=-1, keepdims=True))
        p = jnp.where(cmask, e / jnp.sum(e, axis=-1, keepdims=True), 0.0)
        o = jnp.dot(p.astype(BF16), vc, preferred_element_type=F32)
        o_ref[:, r * d:(r + 1) * d] = (gate[:, r:r + 1] * o).astype(o_ref.dtype)
        psum = psum + p

    ci = lax.broadcasted_iota(jnp.int32, (nk, nk), 0)
    cj = lax.broadcasted_iota(jnp.int32, (nk, nk), 1)
    ovl = ((ci * CMP_STRIDE < cj * SEL_BLOCK + SEL_BLOCK) & (ci * CMP_STRIDE + CMP_BLOCK > cj * SEL_BLOCK)
           & (ci < n_cmp) & (cj < n_blk))
    imp = jnp.dot(psum, ovl.astype(F32), preferred_element_type=F32, precision=lax.Precision.HIGHEST)
    cur = t // SEL_BLOCK
    forced = (n == 0) | (n == cur) | (n == cur - 1)
    score = jnp.where(forced, FORCE_SCORE, jnp.where(n <= cur, imp, NEG_INF))
    st = score.T[0:n_blk, :]
    jidx = lax.broadcasted_iota(jnp.int32, (n_blk, tq), 0)
    cnt = jnp.zeros((n_blk, tq), F32)
    for k in range(n_blk):
        sk = st[k:k + 1, :]
        beats = (sk > st) | ((sk == st) & (k < jidx))
        cnt = cnt + jnp.where(beats, 1.0, 0.0)
    selt = jnp.where(cnt < N_SEL, 1.0, 0.0)
    full = jnp.concatenate([selt, jnp.zeros((nk - n_blk, tq), F32)], axis=0)
    sel_ref[0, 0] = full.T.astype(sel_ref.dtype)


def _cmp_attention(rq, kc, vc, gates, B, S, tq=256):
    T = B * S
    G, d = ATTN_KV_GROUPS, ATTN_HEAD_DIM
    nq = S // tq
    nk = kc.shape[2]
    n_cmp = S // CMP_STRIDE - CMP_BLOCK // CMP_STRIDE + 1
    n_blk = S // SEL_BLOCK
    return pl.pallas_call(
        functools.partial(_cmp_kernel, tq=tq, n_cmp=n_cmp, n_blk=n_blk),
        grid=(B, G, nq),
        in_specs=[pl.BlockSpec((tq, KV_WIDTH), lambda b, g, i: (b * nq + i, g)),
                  pl.BlockSpec((1, 1, nk, d), lambda b, g, i: (b, g, 0, 0)),
                  pl.BlockSpec((1, 1, nk, d), lambda b, g, i: (b, g, 0, 0)),
                  pl.BlockSpec((1, 1, tq, 16), lambda b, g, i: (b, g, i, 0))],
        out_specs=[pl.BlockSpec((tq, KV_WIDTH), lambda b, g, i: (b * nq + i, g)),
                   pl.BlockSpec((1, 1, tq, nk), lambda b, g, i: (b, g, i, 0))],
        out_shape=[jax.ShapeDtypeStruct((T, ATTN_WIDTH), BF16),
                   jax.ShapeDtypeStruct((B, G, S, nk), BF16)],
        compiler_params=_cparams(("parallel", "parallel", "parallel")),
        name="cmp_attn",
    )(rq, kc, vc, gates)


def _flash_kernel(q_ref, k_ref, v_ref, sel_ref, gate_ref, prev_ref, o_ref, qs, m_scr, l_scr, acc_scr,
                  *, mode, tq, tk, nsteps, gate_col):
    i = pl.program_id(2)
    jj = pl.program_id(3)
    d = ATTN_HEAD_DIM
    R = ATTN_REP
    if mode == "sel":
        kv = jj
        last = jnp.minimum(nsteps - 1, (i * tq + tq - 1) // tk)
        valid = jj <= last
    else:
        kv = i - (nsteps - 1) + jj
        last = nsteps - 1
        valid = kv >= 0

    @pl.when(jj == 0)
    def _():
        m_scr[...] = jnp.full(m_scr.shape, NEG_INF, F32)
        l_scr[...] = jnp.zeros_like(l_scr)
        acc_scr[...] = jnp.zeros_like(acc_scr)
        for r in range(R):
            qs[r * tq:(r + 1) * tq, :] = q_ref[:, r * d:(r + 1) * d]

    @pl.when(valid)
    def _():
        s = lax.dot_general(qs[...], k_ref[...], (((1,), (1,)), ((), ())), preferred_element_type=F32)
        row = i * tq + lax.broadcasted_iota(jnp.int32, (tq, tk), 0)
        col = kv * tk + lax.broadcasted_iota(jnp.int32, (tq, tk), 1)
        if mode == "sel":
            nk = sel_ref.shape[-1]
            blk = lax.broadcasted_iota(jnp.int32, (nk, tk), 0)
            key = kv * tk + lax.broadcasted_iota(jnp.int32, (nk, tk), 1)
            expand = jnp.where(blk == key // SEL_BLOCK, 1.0, 0.0).astype(BF16)
            picked = jnp.dot(sel_ref[0, 0], expand, preferred_element_type=F32)
            allowed = (picked > 0.5) & (col <= row)
        else:
            allowed = (row - col >= 0) & (row - col < WINDOW)
        s3 = jnp.where(allowed[None], s.reshape(R, tq, tk), NEG_INF).reshape(R * tq, tk)
        m_prev = m_scr[...]
        m_new = jnp.maximum(m_prev, jnp.max(s3, axis=-1, keepdims=True))
        alpha = jnp.exp(m_prev - m_new)
        p = jnp.exp(s3 - m_new)
        l_scr[...] = alpha * l_scr[...] + jnp.sum(p, axis=-1, keepdims=True)
        acc_scr[...] = alpha * acc_scr[...] + jnp.dot(p.astype(BF16), v_ref[...], preferred_element_type=F32)
        m_scr[...] = m_new

    @pl.when(jj == last)
    def _():
        o = acc_scr[...] / l_scr[...]
        gate = jax.nn.sigmoid(gate_ref[0, 0])
        for r in range(R):
            g = gate[:, gate_col + r:gate_col + r + 1]
            prev = prev_ref[:, r * d:(r + 1) * d].astype(F32)
            o_ref[:, r * d:(r + 1) * d] = (prev + g * o[r * tq:(r + 1) * tq, :]).astype(o_ref.dtype)


def _flash(mode, rq, k_arr, k_col0, v_arr, v_col0, sel, gates, prev, B, S, tq, tk):
    T = B * S
    G, d = ATTN_KV_GROUPS, ATTN_HEAD_DIM
    nq = S // tq
    nkt = S // tk
    if mode == "sel":
        nsteps = nkt
        gate_col = ATTN_REP

        def kv_idx(i, jj):
            return jnp.minimum(jj, (i * tq + tq - 1) // tk)
    else:
        assert tq == tk
        nsteps = WINDOW // tk + 1
        gate_col = 2 * ATTN_REP

        def kv_idx(i, jj):
            return jnp.maximum(i - (nsteps - 1) + jj, 0)

    kc0 = k_col0 // d
    vc0 = v_col0 // d
    nk = sel.shape[-1]
    return pl.pallas_call(
        functools.partial(_flash_kernel, mode=mode, tq=tq, tk=tk, nsteps=nsteps, gate_col=gate_col),
        grid=(B, G, nq, nsteps),
        in_specs=[pl.BlockSpec((tq, KV_WIDTH), lambda b, g, i, j: (b * nq + i, g)),
                  pl.BlockSpec((tk, d), lambda b, g, i, j: (b * nkt + kv_idx(i, j), kc0 + g)),
                  pl.BlockSpec((tk, d), lambda b, g, i, j: (b * nkt + kv_idx(i, j), vc0 + g)),
                  pl.BlockSpec((1, 1, tq, nk), lambda b, g, i, j: (b, g, i, 0)),
                  pl.BlockSpec((1, 1, tq, 16), lambda b, g, i, j: (b, g, i, 0)),
                  pl.BlockSpec((tq, KV_WIDTH), lambda b, g, i, j: (b * nq + i, g))],
        out_specs=pl.BlockSpec((tq, KV_WIDTH), lambda b, g, i, j: (b * nq + i, g)),
        out_shape=jax.ShapeDtypeStruct((T, ATTN_WIDTH), BF16),
        scratch_shapes=[pltpu.VMEM((ATTN_REP * tq, d), BF16),
                        pltpu.VMEM((ATTN_REP * tq, 1), F32),
                        pltpu.VMEM((ATTN_REP * tq, 1), F32),
                        pltpu.VMEM((ATTN_REP * tq, d), F32)],
        compiler_params=_cparams(("parallel", "parallel", "parallel", "arbitrary")),
        name="flash_" + mode,
    )(rq, k_arr, v_arr, sel, gates, prev)


def _merge_kernel(ys_ref, ws_ref, ya_ref, wa_ref, gs_ref, ga_ref, o_ref):
    s = jnp.dot(ys_ref[...], ws_ref[...], preferred_element_type=F32)
    a = jnp.dot(ya_ref[...], wa_ref[...], preferred_element_type=F32)
    gs = jax.nn.sigmoid(gs_ref[...].astype(F32))
    ga = jax.nn.sigmoid(ga_ref[...].astype(F32))
    o_ref[...] = (gs * s + ga * a).astype(o_ref.dtype)


def _merge(y_ssm, w_s, y_attn, w_a, proj, tm=512, tn=512):
    T = y_ssm.shape[0]
    return pl.pallas_call(
        _merge_kernel,
        grid=(D_MODEL // tn, T // tm),
        in_specs=[pl.BlockSpec((tm, D_INNER), lambda n, m: (m, 0)),
                  pl.BlockSpec((D_INNER, tn), lambda n, m: (0, n)),
                  pl.BlockSpec((tm, ATTN_WIDTH), lambda n, m: (m, 0)),
                  pl.BlockSpec((ATTN_WIDTH, tn), lambda n, m: (0, n)),
                  pl.BlockSpec((tm, tn), lambda n, m: (m, COL_GS // tn + n)),
                  pl.BlockSpec((tm, tn), lambda n, m: (m, COL_GA // tn + n))],
        out_specs=pl.BlockSpec((tm, tn), lambda n, m: (m, n)),
        out_shape=jax.ShapeDtypeStruct((T, D_MODEL), BF16),
        compiler_params=_cparams(("parallel", "parallel")),
        name="merge",
    )(y_ssm, w_s, y_attn, w_a, proj, proj)


def _mix_out_kernel(mg_ref, w_ref, x_ref, nw_ref, x1_ref, h_ref):
    x1 = x_ref[...] + jnp.dot(mg_ref[...], w_ref[...], preferred_element_type=F32)
    x1_ref[...] = x1
    h_ref[...] = _rms(x1, nw_ref[...]).astype(h_ref.dtype)


def _mix_out(merged, w, x2d, norm_w, tm=256):
    T = x2d.shape[0]
    return pl.pallas_call(
        _mix_out_kernel,
        grid=(T // tm,),
        in_specs=[pl.BlockSpec((tm, D_MODEL), lambda m: (m, 0)),
                  pl.BlockSpec((D_MODEL, D_MODEL), lambda m: (0, 0)),
                  pl.BlockSpec((tm, D_MODEL), lambda m: (m, 0)),
                  pl.BlockSpec((1, D_MODEL), lambda m: (0, 0))],
        out_specs=[pl.BlockSpec((tm, D_MODEL), lambda m: (m, 0)),
                   pl.BlockSpec((tm, D_MODEL), lambda m: (m, 0))],
        out_shape=[jax.ShapeDtypeStruct((T, D_MODEL), F32),
                   jax.ShapeDtypeStruct((T, D_MODEL), BF16)],
        compiler_params=_cparams(("parallel",)),
        name="mix_out",
    )(merged, w, x2d, norm_w)


def _ffn_up_kernel(h_ref, hp_ref, wg_ref, wu_ref, cw_ref, cb_ref, o_ref, gbuf, *, tm, tiles_per_seq):
    m = pl.program_id(1)
    wg = wg_ref[...]
    h = h_ref[...]
    gbuf[TAIL:TAIL + tm, :] = jnp.dot(h, wg, preferred_element_type=F32)
    gprev = jnp.dot(hp_ref[...], wg, preferred_element_type=F32)
    gbuf[0:TAIL, :] = jnp.where(m % tiles_per_seq == 0, 0.0, gprev)
    acc = cb_ref[...]
    for k in range(FFN_CONV):
        off = TAIL - (FFN_CONV - 1) + k
        acc = acc + cw_ref[k:k + 1, :] * gbuf[off:off + tm, :]
    u = jnp.dot(h, wu_ref[...], preferred_element_type=F32)
    o_ref[...] = (_silu(acc) * u).astype(o_ref.dtype)


def _ffn_up(h, wg, wu, conv_w, conv_b, S, tm=512, tn=512):
    T = h.shape[0]
    tps = S // tm
    return pl.pallas_call(
        functools.partial(_ffn_up_kernel, tm=tm, tiles_per_seq=tps),
        grid=(D_FF // tn, T // tm),
        in_specs=[pl.BlockSpec((tm, D_MODEL), lambda n, m: (m, 0)),
                  pl.BlockSpec((TAIL, D_MODEL), lambda n, m: (jnp.maximum(m * (tm // TAIL) - 1, 0), 0)),
                  pl.BlockSpec((D_MODEL, tn), lambda n, m: (0, n)),
                  pl.BlockSpec((D_MODEL, tn), lambda n, m: (0, n)),
                  pl.BlockSpec((FFN_CONV, tn), lambda n, m: (0, n)),
                  pl.BlockSpec((1, tn), lambda n, m: (0, n))],
        out_specs=pl.BlockSpec((tm, tn), lambda n, m: (m, n)),
        out_shape=jax.ShapeDtypeStruct((T, D_FF), BF16),
        scratch_shapes=[pltpu.VMEM((tm + TAIL, tn), F32)],
        compiler_params=_cparams(("parallel", "parallel")),
        name="ffn_up",
    )(h, h, wg, wu, conv_w, conv_b.reshape(1, D_FF))


def _ffn_down_kernel(a_ref, w_ref, x_ref, nw_ref, x2_ref, h_ref, acc):
    k = pl.program_id(1)

    @pl.when(k == 0)
    def _():
        acc[...] = x_ref[...]

    acc[...] += jnp.dot(a_ref[...], w_ref[...], preferred_element_type=F32)

    @pl.when(k == pl.num_programs(1) - 1)
    def _():
        x2 = acc[...]
        x2_ref[...] = x2
        h_ref[...] = _rms(x2, nw_ref[...]).astype(h_ref.dtype)


def _ffn_down(act, w, x1, norm_w, tm=512, tk=512):
    T = x1.shape[0]
    return pl.pallas_call(
        _ffn_down_kernel,
        grid=(T // tm, D_FF // tk),
        in_specs=[pl.BlockSpec((tm, tk), lambda m, k: (m, k)),
                  pl.BlockSpec((tk, D_MODEL), lambda m, k: (k, 0)),
                  pl.BlockSpec((tm, D_MODEL), lambda m, k: (m, 0)),
                  pl.BlockSpec((1, D_MODEL), lambda m, k: (0, 0))],
        out_specs=[pl.BlockSpec((tm, D_MODEL), lambda m, k: (m, 0)),
                   pl.BlockSpec((tm, D_MODEL), lambda m, k: (m, 0))],
        out_shape=[jax.ShapeDtypeStruct((T, D_MODEL), F32),
                   jax.ShapeDtypeStruct((T, D_MODEL), BF16)],
        scratch_shapes=[pltpu.VMEM((tm, D_MODEL), F32)],
        compiler_params=_cparams(("parallel", "arbitrary")),
        name="ffn_down",
    )(act, w, x1, norm_w)


def _ple_kernel(h_ref, wg_ref, p_ref, wp_ref, x_ref, nw_ref, o_ref):
    gate = jax.nn.sigmoid(jnp.dot(h_ref[...], wg_ref[...], preferred_element_type=F32))
    emb = jnp.dot(p_ref[...].astype(BF16), wp_ref[...], preferred_element_type=F32)
    o_ref[...] = _rms(x_ref[...] + gate * emb, nw_ref[...])


def _ple(h, wg, p2d, wp, x2, norm_w, tm=256):
    T = x2.shape[0]
    return pl.pallas_call(
        _ple_kernel,
        grid=(T // tm,),
        in_specs=[pl.BlockSpec((tm, D_MODEL), lambda m: (m, 0)),
                  pl.BlockSpec((D_MODEL, D_MODEL), lambda m: (0, 0)),
                  pl.BlockSpec((tm, PLE_DIM), lambda m: (m, 0)),
                  pl.BlockSpec((PLE_DIM, D_MODEL), lambda m: (0, 0)),
                  pl.BlockSpec((tm, D_MODEL), lambda m: (m, 0)),
                  pl.BlockSpec((1, D_MODEL), lambda m: (0, 0))],
        out_specs=pl.BlockSpec((tm, D_MODEL), lambda m: (m, 0)),
        out_shape=jax.ShapeDtypeStruct((T, D_MODEL), F32),
        compiler_params=_cparams(("parallel",)),
        name="ple",
    )(h, wg, p2d, wp, x2, norm_w)


def _split_w_in(w):
    sizes = (D_INNER, SSM_CONV_DIM, SSM_HEADS, ATTN_WIDTH) + (KV_WIDTH,) * 6 + (3 * ATTN_HEADS, D_MODEL, D_MODEL)
    offs = [0]
    for s in sizes:
        offs.append(offs[-1] + s)
    seg = lambda i: w[:, offs[i]:offs[i + 1]]
    wide = jnp.concatenate([seg(0), seg(1)] + [seg(i) for i in range(3, 10)] + [seg(11), seg(12)], axis=1)
    pad = N_SMALL - SSM_HEADS - 3 * ATTN_HEADS
    small = jnp.concatenate([seg(2), seg(10), jnp.zeros((w.shape[0], pad), w.dtype)], axis=1)
    return wide.astype(BF16), small


def _layer(x2d, p2d, B, S, norm_mix_w, w_in, ssm_conv_w, ssm_conv_b, ssm_dt_bias, ssm_a_log, ssm_d,
           ssm_norm_w, cmp_pe_k, cmp_pe_v, cmp_wk1, cmp_wk2, cmp_wv1, cmp_wv2,
           w_ssm_branch, w_attn_branch, w_mix_out, norm_ffn_w, ffn_w_gate, ffn_w_up,
           ffn_conv_w, ffn_conv_b, ffn_w_down, ple_norm_w, ple_w_gate, ple_w_proj, out_norm_w):
    G, R = ATTN_KV_GROUPS, ATTN_REP
    w_wide, w_small = _split_w_in(w_in)
    h, small = _norm_small(x2d, norm_mix_w.reshape(1, D_MODEL), w_small)
    proj = _matmul(h, w_wide, BF16)

    y_ssm = _ssd(proj, small, ssm_conv_w, ssm_conv_b, ssm_dt_bias, ssm_a_log, ssm_d, ssm_norm_w, B, S)

    rq = _rope(proj, S)
    kc = _compress(rq[:, ROPE_KC:ROPE_KC + KV_WIDTH], cmp_pe_k, cmp_wk1, cmp_wk2, B, S)
    vc = _compress(proj[:, COL_KV + KV_WIDTH:COL_KV + 2 * KV_WIDTH], cmp_pe_v, cmp_wv1, cmp_wv2, B, S)
    g_nsa = small[:, SSM_HEADS:SSM_HEADS + 3 * ATTN_HEADS].reshape(B, S, 3, G, R)
    gates = g_nsa.transpose(0, 3, 1, 2, 4).reshape(B, G, S, 3 * R)
    gates = jnp.pad(gates, ((0, 0), (0, 0), (0, 0), (0, 16 - 3 * R)))
    o_cmp, sel = _cmp_attention(rq, kc, vc, gates, B, S)
    o_sel = _flash("sel", rq, rq, ROPE_KS, proj, COL_KV + 3 * KV_WIDTH, sel, gates, o_cmp, B, S, tq=256, tk=512)
    y_attn = _flash("win", rq, rq, ROPE_KW, proj, COL_KV + 5 * KV_WIDTH, sel, gates, o_sel, B, S, tq=256, tk=256)

    merged = _merge(y_ssm, w_ssm_branch.astype(BF16), y_attn, w_attn_branch.astype(BF16), proj)
    x1, h2 = _mix_out(merged, w_mix_out.astype(BF16), x2d, norm_ffn_w.reshape(1, D_MODEL))

    act = _ffn_up(h2, ffn_w_gate.astype(BF16), ffn_w_up.astype(BF16), ffn_conv_w, ffn_conv_b, S)
    x2, h3 = _ffn_down(act, ffn_w_down.astype(BF16), x1, ple_norm_w.reshape(1, D_MODEL))
    return _ple(h3, ple_w_gate.astype(BF16), p2d, ple_w_proj.astype(BF16), x2, out_norm_w.reshape(1, D_MODEL))


def kernel(x, p, norm_mix_w, w_in, ssm_conv_w, ssm_conv_b, ssm_dt_bias, ssm_a_log, ssm_d, ssm_norm_w, cmp_pe_k, cmp_pe_v, cmp_wk1, cmp_wk2, cmp_wv1, cmp_wv2, w_ssm_branch, w_attn_branch, w_mix_out, norm_ffn_w, ffn_w_gate, ffn_w_up, ffn_conv_w, ffn_conv_b, ffn_w_down, ple_norm_w, ple_w_gate, ple_w_proj, final_norm_w):
    B, S, D = x.shape
    depth = w_in.shape[0]
    assert depth == 1, "the final norm is fused into the (single) layer's last kernel"
    x2d = x.reshape(B * S, D)
    out = _layer(x2d, p[0].reshape(B * S, PLE_DIM), B, S, norm_mix_w[0], w_in[0], ssm_conv_w[0], ssm_conv_b[0],
                 ssm_dt_bias[0], ssm_a_log[0], ssm_d[0], ssm_norm_w[0], cmp_pe_k[0], cmp_pe_v[0],
                 cmp_wk1[0], cmp_wk2[0], cmp_wv1[0], cmp_wv2[0], w_ssm_branch[0], w_attn_branch[0],
                 w_mix_out[0], norm_ffn_w[0], ffn_w_gate[0], ffn_w_up[0], ffn_conv_w[0], ffn_conv_b[0],
                 ffn_w_down[0], ple_norm_w[0], ple_w_gate[0], ple_w_proj[0], final_norm_w)
    return out.reshape(B, S, D)
```

```python
import functools
import math

import jax
import jax.numpy as jnp
from jax import lax
from jax.experimental import pallas as pl
from jax.experimental.pallas import tpu as pltpu

F32 = jnp.float32
BF16 = jnp.bfloat16

D_MODEL = 2048
PLE_DIM = 256
EPS = 1e-6
D_INNER = 4096
SSM_HEAD_DIM = 64
SSM_HEADS = 64
SSM_GROUPS = 8
SSM_STATE = 128
SSM_CONV = 4
SSM_CHUNK = 128
SSM_BC = SSM_GROUPS * SSM_STATE
SSM_CONV_DIM = D_INNER + 2 * SSM_BC
ATTN_HEADS = 16
ATTN_HEAD_DIM = 128
ATTN_KV_GROUPS = 4
ATTN_REP = ATTN_HEADS // ATTN_KV_GROUPS
ATTN_WIDTH = ATTN_HEADS * ATTN_HEAD_DIM
KV_WIDTH = ATTN_KV_GROUPS * ATTN_HEAD_DIM
CMP_BLOCK = 32
CMP_STRIDE = 16
SEL_BLOCK = 64
N_SEL = 16
WINDOW = 512
ROPE_THETA = 500000.0
ROPE_DIM = ATTN_HEAD_DIM // 4
D_FF = 5632
FFN_CONV = 3
NEG_INF = -1e30
FORCE_SCORE = 1e4

COL_Z = 0
COL_XBC = COL_Z + D_INNER
COL_Q = COL_XBC + SSM_CONV_DIM
COL_KV = COL_Q + ATTN_WIDTH
COL_GS = COL_KV + 6 * KV_WIDTH
COL_GA = COL_GS + D_MODEL
N_WIDE = COL_GA + D_MODEL
N_SMALL = 128

VMEM_LIMIT = 56 * 1024 * 1024
TAIL = 8


def _cparams(sem):
    return pltpu.CompilerParams(dimension_semantics=sem, vmem_limit_bytes=VMEM_LIMIT)


def _rms(xf, w):
    return xf * lax.rsqrt(jnp.mean(xf * xf, axis=-1, keepdims=True) + EPS) * w


def _silu(v):
    return v * jax.nn.sigmoid(v)


def _norm_small_kernel(x_ref, nw_ref, ws_ref, h_ref, small_ref):
    h = _rms(x_ref[...], nw_ref[...])
    h_ref[...] = h.astype(BF16)
    small_ref[...] = jnp.dot(h, ws_ref[...], preferred_element_type=F32, precision=lax.Precision.HIGHEST)


def _norm_small(x2d, norm_w, w_small, tm=512):
    T = x2d.shape[0]
    return pl.pallas_call(
        _norm_small_kernel,
        grid=(T // tm,),
        in_specs=[pl.BlockSpec((tm, D_MODEL), lambda i: (i, 0)),
                  pl.BlockSpec((1, D_MODEL), lambda i: (0, 0)),
                  pl.BlockSpec((D_MODEL, N_SMALL), lambda i: (0, 0))],
        out_specs=[pl.BlockSpec((tm, D_MODEL), lambda i: (i, 0)),
                   pl.BlockSpec((tm, N_SMALL), lambda i: (i, 0))],
        out_shape=[jax.ShapeDtypeStruct((T, D_MODEL), BF16),
                   jax.ShapeDtypeStruct((T, N_SMALL), F32)],
        compiler_params=_cparams(("parallel",)),
        name="norm_small",
    )(x2d, norm_w, w_small)


def _matmul_kernel(a_ref, b_ref, o_ref):
    o_ref[...] = jnp.dot(a_ref[...], b_ref[...], preferred_element_type=F32).astype(o_ref.dtype)


def _matmul(a, b, out_dtype, tm=1024, tn=1024):
    M, K = a.shape
    N = b.shape[1]
    return pl.pallas_call(
        _matmul_kernel,
        grid=(N // tn, M // tm),
        in_specs=[pl.BlockSpec((tm, K), lambda n, m: (m, 0)),
                  pl.BlockSpec((K, tn), lambda n, m: (0, n))],
        out_specs=pl.BlockSpec((tm, tn), lambda n, m: (m, n)),
        out_shape=jax.ShapeDtypeStruct((M, N), out_dtype),
        compiler_params=_cparams(("parallel", "parallel")),
        name="in_proj",
    )(a, b)


def _rope_kernel(x_ref, ca_ref, cb_ref, cc_ref, o_ref, *, scale):
    j = pl.program_id(1)
    sc = jnp.where(j < ATTN_REP, scale, 1.0).astype(F32)
    ca, cb, cc = ca_ref[...], cb_ref[...], cc_ref[...]
    d = ATTN_HEAD_DIM
    half = ROPE_DIM // 2
    for r in range(KV_WIDTH // d):
        xh = x_ref[:, r * d:(r + 1) * d].astype(F32)
        up = pltpu.roll(xh, d - half, 1)
        dn = pltpu.roll(xh, half, 1)
        o_ref[:, r * d:(r + 1) * d] = ((xh * ca + up * cb + dn * cc) * sc).astype(o_ref.dtype)


def _rope_tables(S):
    half = ROPE_DIM // 2
    pos = jnp.arange(S)
    inv = ROPE_THETA ** (-(jnp.arange(half, dtype=F32) * 2.0 / ROPE_DIM))
    ang = pos.astype(F32)[:, None] * inv[None, :]
    cos, sin = jnp.cos(ang), jnp.sin(ang)
    rest = ATTN_HEAD_DIM - ROPE_DIM
    ca = jnp.concatenate([cos, cos, jnp.ones((S, rest), F32)], axis=1)
    cb = jnp.concatenate([-sin, jnp.zeros((S, ATTN_HEAD_DIM - half), F32)], axis=1)
    cc = jnp.concatenate([jnp.zeros((S, half), F32), sin, jnp.zeros((S, rest), F32)], axis=1)
    return ca, cb, cc


N_ROPE_BLOCKS = ATTN_REP + 3
ROPE_KC, ROPE_KS, ROPE_KW = ATTN_WIDTH, ATTN_WIDTH + KV_WIDTH, ATTN_WIDTH + 2 * KV_WIDTH


def _rope(proj, S, ts=512):
    T = proj.shape[0]
    ca, cb, cc = _rope_tables(S)
    wb = KV_WIDTH
    q0 = COL_Q // wb
    k0 = COL_KV // wb

    def in_map(i, j):
        return (i, jnp.where(j < ATTN_REP, q0 + j, k0 + 2 * (j - ATTN_REP)))

    tab = pl.BlockSpec((ts, ATTN_HEAD_DIM), lambda i, j: (i % (S // ts), 0))
    return pl.pallas_call(
        functools.partial(_rope_kernel, scale=math.log2(math.e) / math.sqrt(ATTN_HEAD_DIM)),
        grid=(T // ts, N_ROPE_BLOCKS),
        in_specs=[pl.BlockSpec((ts, wb), in_map), tab, tab, tab],
        out_specs=pl.BlockSpec((ts, wb), lambda i, j: (i, j)),
        out_shape=jax.ShapeDtypeStruct((T, N_ROPE_BLOCKS * wb), BF16),
        compiler_params=_cparams(("parallel", "parallel")),
        name="rope",
    )(proj, ca, cb, cc)


def _ssd_kernel(x_ref, b_ref, c_ref, z_ref, wx_ref, wb_ref, wc_ref, bx_ref, bb_ref, bc_ref,
                dtr_ref, dtrT_ref, dtb_ref, dtbT_ref, alog_ref, alogT_ref, dsk_ref, nw_ref,
                y_ref, h_scr, xbuf, bbuf, cbuf):
    L = SSM_CHUNK
    P = SSM_HEAD_DIM
    c = pl.program_id(2)

    @pl.when(c == 0)
    def _():
        h_scr[...] = jnp.zeros_like(h_scr)
        xbuf[0:TAIL, :] = jnp.zeros((TAIL, xbuf.shape[1]), F32)
        bbuf[0:TAIL, :] = jnp.zeros((TAIL, bbuf.shape[1]), F32)
        cbuf[0:TAIL, :] = jnp.zeros((TAIL, cbuf.shape[1]), F32)

    def conv_silu(buf, src_ref, w_ref, bias_ref):
        buf[TAIL:TAIL + L, :] = src_ref[...].astype(F32)
        acc = bias_ref[...]
        for k in range(SSM_CONV):
            off = TAIL - (SSM_CONV - 1) + k
            acc = acc + w_ref[k:k + 1, :] * buf[off:off + L, :]
        buf[0:TAIL, :] = buf[L:L + TAIL, :]
        return _silu(acc)

    xs = conv_silu(xbuf, x_ref, wx_ref, bx_ref)
    bm = conv_silu(bbuf, b_ref, wb_ref, bb_ref)
    cm = conv_silu(cbuf, c_ref, wc_ref, bc_ref)

    dt = jax.nn.softplus(dtr_ref[0, 0] + dtb_ref[0])
    dtT = jax.nn.softplus(dtrT_ref[0] + dtbT_ref[0])
    a = -jnp.exp(alog_ref[0])
    aT = -jnp.exp(alogT_ref[0])
    ri = lax.broadcasted_iota(jnp.int32, (L, L), 0)
    ci = lax.broadcasted_iota(jnp.int32, (L, L), 1)
    tril = ci <= ri
    hp = lax.Precision.HIGHEST
    acum = jnp.dot(tril.astype(F32), dt * a, preferred_element_type=F32, precision=hp)
    acumT = jnp.dot(dtT * aT, (ri <= ci).astype(F32), preferred_element_type=F32, precision=hp)

    cmb = cm.astype(BF16)
    cb = lax.dot_general(cmb, bm.astype(BF16), (((1,), (1,)), ((), ())), preferred_element_type=F32)
    yoff = jnp.dot(cmb, h_scr[...].astype(BF16), preferred_element_type=F32)

    lane = lax.broadcasted_iota(jnp.int32, (L, 2 * P), 1)
    lo = lane < P

    def pair(v0, v1):
        return jnp.where(lo, v0, v1)

    dsk = dsk_ref[...]
    ys, xds, cds = [], [], []
    for pr in range(SSM_HEADS // SSM_GROUPS // 2):
        r0, r1 = 2 * pr, 2 * pr + 1
        sl = slice(pr * 2 * P, (pr + 1) * 2 * P)
        xp = xs[:, sl]
        xdt = xp * pair(dt[:, r0:r0 + 1], dt[:, r1:r1 + 1])
        xdtb = xdt.astype(BF16)
        yd = []
        for r in (r0, r1):
            seg = acum[:, r:r + 1] - acumT[r:r + 1, :]
            dec = jnp.exp(jnp.where(tril, seg, -jnp.inf))
            yd.append(jnp.dot((cb * dec).astype(BF16), xdtb, preferred_element_type=F32))
        ea = pair(jnp.exp(acum[:, r0:r0 + 1]), jnp.exp(acum[:, r1:r1 + 1]))
        ys.append(pair(yd[0], yd[1]) + yoff[:, sl] * ea + dsk[:, sl] * xp)
        al0, al1 = acum[L - 1:L, r0:r0 + 1], acum[L - 1:L, r1:r1 + 1]
        xds.append(xdt * pair(jnp.exp(al0 - acum[:, r0:r0 + 1]), jnp.exp(al1 - acum[:, r1:r1 + 1])))
        cds.append(jnp.where(lo[0:1, :], jnp.exp(al0), jnp.exp(al1)))
    xds = jnp.concatenate(xds, axis=1).astype(BF16)
    cd = jnp.concatenate(cds, axis=1)
    st = jnp.dot(bm.T.astype(BF16), xds, preferred_element_type=F32)
    h_scr[...] = h_scr[...] * cd + st

    y = jnp.concatenate(ys, axis=1) * _silu(z_ref[...].astype(F32))
    y_ref[...] = _rms(y, nw_ref[...]).astype(y_ref.dtype)


def _ssd(proj, small, conv_w, conv_b, dt_bias, a_log, d_skip, norm_w, B, S):
    T = B * S
    L, G, N = SSM_CHUNK, SSM_GROUPS, SSM_STATE
    R = SSM_HEADS // G
    gw = D_INNER // G
    nc = S // L
    dtr = small[:, :SSM_HEADS].reshape(B, S, G, R)
    dtr_g = dtr.transpose(0, 2, 1, 3)
    dtr_t = dtr.reshape(B, S, SSM_HEADS).transpose(0, 2, 1)
    dtb = dt_bias.reshape(G, 1, R)
    dtbT = dt_bias.reshape(G, R, 1)
    alog = a_log.reshape(G, 1, R)
    alogT = a_log.reshape(G, R, 1)
    dsk = jnp.repeat(d_skip, SSM_HEAD_DIM).reshape(1, D_INNER)
    cw = conv_w
    cbias = conv_b.reshape(1, SSM_CONV_DIM)
    nw = norm_w.reshape(1, D_INNER)

    row = lambda b, g, c: b * nc + c
    xb0 = COL_XBC // gw
    bb0 = (COL_XBC + D_INNER) // N
    cb0 = (COL_XBC + D_INNER + SSM_BC) // N
    in_specs = [
        pl.BlockSpec((L, gw), lambda b, g, c: (row(b, g, c), xb0 + g)),
        pl.BlockSpec((L, N), lambda b, g, c: (row(b, g, c), bb0 + g)),
        pl.BlockSpec((L, N), lambda b, g, c: (row(b, g, c), cb0 + g)),
        pl.BlockSpec((L, gw), lambda b, g, c: (row(b, g, c), g)),
        pl.BlockSpec((SSM_CONV, gw), lambda b, g, c: (0, g)),
        pl.BlockSpec((SSM_CONV, N), lambda b, g, c: (0, D_INNER // N + g)),
        pl.BlockSpec((SSM_CONV, N), lambda b, g, c: (0, (D_INNER + SSM_BC) // N + g)),
        pl.BlockSpec((1, gw), lambda b, g, c: (0, g)),
        pl.BlockSpec((1, N), lambda b, g, c: (0, D_INNER // N + g)),
        pl.BlockSpec((1, N), lambda b, g, c: (0, (D_INNER + SSM_BC) // N + g)),
        pl.BlockSpec((1, 1, L, R), lambda b, g, c: (b, g, c, 0)),
        pl.BlockSpec((1, R, L), lambda b, g, c: (b, g, c)),
        pl.BlockSpec((1, 1, R), lambda b, g, c: (g, 0, 0)),
        pl.BlockSpec((1, R, 1), lambda b, g, c: (g, 0, 0)),
        pl.BlockSpec((1, 1, R), lambda b, g, c: (g, 0, 0)),
        pl.BlockSpec((1, R, 1), lambda b, g, c: (g, 0, 0)),
        pl.BlockSpec((1, gw), lambda b, g, c: (0, g)),
        pl.BlockSpec((1, gw), lambda b, g, c: (0, g)),
    ]
    return pl.pallas_call(
        _ssd_kernel,
        grid=(B, G, nc),
        in_specs=in_specs,
        out_specs=pl.BlockSpec((L, gw), lambda b, g, c: (row(b, g, c), g)),
        out_shape=jax.ShapeDtypeStruct((T, D_INNER), BF16),
        scratch_shapes=[pltpu.VMEM((N, gw), F32),
                        pltpu.VMEM((L + TAIL, gw), F32),
                        pltpu.VMEM((L + TAIL, N), F32),
                        pltpu.VMEM((L + TAIL, N), F32)],
        compiler_params=_cparams(("parallel", "parallel", "arbitrary")),
        name="ssd",
    )(proj, proj, proj, proj, cw, cw, cw, cbias, cbias, cbias,
      dtr_g, dtr_t, dtb, dtbT, alog, alogT, dsk, nw)


def _compress_kernel(sub_ref, pe_ref, w1a_ref, w1b_ref, w2_ref, o_ref):
    sub = sub_ref[0, 0].astype(F32)
    nsub = sub.shape[0]
    u = jnp.dot((sub + pe_ref[0:1, :]).astype(BF16), w1a_ref[...], preferred_element_type=F32)
    v = jnp.dot((sub + pe_ref[1:2, :]).astype(BF16), w1b_ref[...], preferred_element_type=F32)
    pre = u + pltpu.roll(v, nsub - 1, 0)
    o_ref[0, 0] = jnp.dot(_silu(pre).astype(BF16), w2_ref[...], preferred_element_type=F32).astype(o_ref.dtype)


def _compress(tok, pe, w1, w2, B, S):
    G, d = ATTN_KV_GROUPS, ATTN_HEAD_DIM
    nsub = S // CMP_STRIDE
    wsub = CMP_STRIDE * d
    sub = tok.reshape(B, S, G, d).transpose(0, 2, 1, 3).reshape(B, G, nsub, wsub)
    pe2 = pe.reshape(CMP_BLOCK // CMP_STRIDE, wsub)
    w1b16 = w1.astype(BF16)
    return pl.pallas_call(
        _compress_kernel,
        grid=(B, G),
        in_specs=[pl.BlockSpec((1, 1, nsub, wsub), lambda b, g: (b, g, 0, 0)),
                  pl.BlockSpec((CMP_BLOCK // CMP_STRIDE, wsub), lambda b, g: (0, 0)),
                  pl.BlockSpec((wsub, d), lambda b, g: (0, 0)),
                  pl.BlockSpec((wsub, d), lambda b, g: (1, 0)),
                  pl.BlockSpec((d, d), lambda b, g: (0, 0))],
        out_specs=pl.BlockSpec((1, 1, nsub, d), lambda b, g: (b, g, 0, 0)),
        out_shape=jax.ShapeDtypeStruct((B, G, nsub, d), BF16),
        compiler_params=_cparams(("parallel", "parallel")),
        name="compress",
    )(sub, pe2, w1b16, w1b16, w2.astype(BF16))


def _nsa_kernel(q_ref, kc_ref, vc_ref, ks_ref, vs_ref, kw_ref, vw_ref, gt_ref, o_ref,
                vct, vst, vwt, qt, m_scr, l_scr, acc_scr, out_scr, *, tq, tks, n_cmp, n_blk):
    i = pl.program_id(2)
    d = ATTN_HEAD_DIM
    R = ATTN_REP
    heads = [slice(r * tq, (r + 1) * tq) for r in range(R)]

    def transposed(v):
        return v.astype(F32).T.astype(BF16)

    @pl.when(i == 0)
    def _():
        vct[...] = transposed(vc_ref[0, 0])
        vst[...] = transposed(vs_ref[...])
        vwt[...] = transposed(vw_ref[...])

    for r in range(R):
        qt[:, heads[r]] = transposed(q_ref[:, r * d:(r + 1) * d])
    gate = jax.nn.sigmoid(gt_ref[0, 0])

    nk = kc_ref.shape[2]
    kc = kc_ref[0, 0]
    n_idx = lax.broadcasted_iota(jnp.int32, (nk, tq), 0)
    t_cmp = i * tq + lax.broadcasted_iota(jnp.int32, (nk, tq), 1)
    cmask = (n_idx * CMP_STRIDE + CMP_BLOCK - 1 <= t_cmp) & (n_idx < n_cmp)
    psum = jnp.zeros((nk, tq), F32)
    for r in range(R):
        s = jnp.where(cmask, jnp.dot(kc, qt[:, heads[r]], preferred_element_type=F32), NEG_INF)
        e = jnp.exp2(s - jnp.max(s, axis=0, keepdims=True))
        p = jnp.where(cmask, e * (1.0 / jnp.sum(e, axis=0, keepdims=True)), 0.0)
        out_scr[:, heads[r]] = gate[r:r + 1, :] * jnp.dot(vct[...], p.astype(BF16), preferred_element_type=F32)
        psum = psum + p
    bi = lax.broadcasted_iota(jnp.int32, (n_blk, nk), 0)
    ni = lax.broadcasted_iota(jnp.int32, (n_blk, nk), 1)
    ovl = ((ni * CMP_STRIDE < bi * SEL_BLOCK + SEL_BLOCK) & (ni * CMP_STRIDE + CMP_BLOCK > bi * SEL_BLOCK)
           & (ni < n_cmp))
    imp = jnp.dot(jnp.where(ovl, 1.0, 0.0), psum, preferred_element_type=F32, precision=lax.Precision.HIGHEST)
    blk = lax.broadcasted_iota(jnp.int32, (n_blk, tq), 0)
    cur = (i * tq + lax.broadcasted_iota(jnp.int32, (n_blk, tq), 1)) // SEL_BLOCK
    forced = (blk == 0) | (blk == cur) | (blk == cur - 1)
    score = jnp.where(forced, FORCE_SCORE, jnp.where(blk <= cur, imp, NEG_INF))
    rank = jnp.zeros((n_blk, tq), F32)
    for k in range(n_blk):
        sk = score[k:k + 1, :]
        rank = rank + jnp.where((sk > score) | ((sk == score) & (k < blk)), 1.0, 0.0)
    sel = jnp.where(rank < N_SEL, 1.0, 0.0).astype(BF16)

    def reset():
        m_scr[...] = jnp.full(m_scr.shape, NEG_INF, F32)
        l_scr[...] = jnp.zeros_like(l_scr)
        acc_scr[...] = jnp.zeros_like(acc_scr)

    def update(k_chunk, vt_chunk, bias):
        for r in range(R):
            s = jnp.dot(k_chunk, qt[:, heads[r]], preferred_element_type=F32) + bias
            m_prev = m_scr[r:r + 1, :]
            m_new = jnp.maximum(m_prev, jnp.max(s, axis=0, keepdims=True))
            alpha = jnp.exp2(m_prev - m_new)
            p = jnp.exp2(s - m_new)
            l_scr[r:r + 1, :] = alpha * l_scr[r:r + 1, :] + jnp.sum(p, axis=0, keepdims=True)
            acc_scr[:, heads[r]] = (alpha * acc_scr[:, heads[r]]
                                    + jnp.dot(vt_chunk, p.astype(BF16), preferred_element_type=F32))
            m_scr[r:r + 1, :] = m_new

    def sel_body(c, carry):
        k0 = pl.multiple_of(c * tks, tks)
        key = k0 + lax.broadcasted_iota(jnp.int32, (tks, tq), 0)
        t = i * tq + lax.broadcasted_iota(jnp.int32, (tks, tq), 1)
        in_blk = (lax.broadcasted_iota(jnp.int32, (tks, n_blk), 1)
                  == (k0 + lax.broadcasted_iota(jnp.int32, (tks, n_blk), 0)) // SEL_BLOCK)
        picked = jnp.dot(jnp.where(in_blk, 1.0, 0.0).astype(BF16), sel, preferred_element_type=F32)
        bias = jnp.where((picked > 0.5) & (key <= t), 0.0, NEG_INF)
        update(ks_ref[pl.ds(k0, tks), :], vst[:, pl.ds(k0, tks)], bias)
        return carry

    reset()
    lax.fori_loop(0, (i * tq + tq - 1) // tks + 1, sel_body, 0)
    for r in range(R):
        o = acc_scr[:, heads[r]] * (1.0 / l_scr[r:r + 1, :])
        out_scr[:, heads[r]] += gate[R + r:R + r + 1, :] * o

    span = WINDOW + tq
    k0 = pl.multiple_of(jnp.maximum(i * tq - WINDOW, 0), tq)
    dist = (i * tq + lax.broadcasted_iota(jnp.int32, (span, tq), 1)
            - (k0 + lax.broadcasted_iota(jnp.int32, (span, tq), 0)))
    bias = jnp.where((dist >= 0) & (dist < WINDOW), 0.0, NEG_INF)
    k_win = kw_ref[pl.ds(k0, span), :]
    vt_win = vwt[:, pl.ds(k0, span)]
    for r in range(R):
        s = jnp.dot(k_win, qt[:, heads[r]], preferred_element_type=F32) + bias
        p = jnp.exp2(s - jnp.max(s, axis=0, keepdims=True))
        o = jnp.dot(vt_win, p.astype(BF16), preferred_element_type=F32) * (1.0 / jnp.sum(p, axis=0, keepdims=True))
        out_scr[:, heads[r]] += gate[2 * R + r:2 * R + r + 1, :] * o

    for r in range(R):
        o_ref[:, r * d:(r + 1) * d] = out_scr[:, heads[r]].T.astype(o_ref.dtype)


def _nsa(rq, proj, kc, vc, gates_t, B, S, tq=256, tks=512):
    assert WINDOW % tq == 0 and S >= WINDOW + tq
    T = B * S
    G, d, R = ATTN_KV_GROUPS, ATTN_HEAD_DIM, ATTN_REP
    nq = S // tq
    nk = kc.shape[2]
    n_cmp = S // CMP_STRIDE - CMP_BLOCK // CMP_STRIDE + 1
    n_blk = S // SEL_BLOCK
    vs0 = (COL_KV + 3 * KV_WIDTH) // d
    vw0 = (COL_KV + 5 * KV_WIDTH) // d
    seq = lambda c0: pl.BlockSpec((S, d), lambda b, g, i: (b, c0 + g))
    cmp_spec = pl.BlockSpec((1, 1, nk, d), lambda b, g, i: (b, g, 0, 0))
    return pl.pallas_call(
        functools.partial(_nsa_kernel, tq=tq, tks=tks, n_cmp=n_cmp, n_blk=n_blk),
        grid=(B, G, nq),
        in_specs=[pl.BlockSpec((tq, KV_WIDTH), lambda b, g, i: (b * nq + i, g)),
                  cmp_spec, cmp_spec,
                  seq(ROPE_KS // d), seq(vs0), seq(ROPE_KW // d), seq(vw0),
                  pl.BlockSpec((1, 1, 16, tq), lambda b, g, i: (b, g, 0, i))],
        out_specs=pl.BlockSpec((tq, KV_WIDTH), lambda b, g, i: (b * nq + i, g)),
        out_shape=jax.ShapeDtypeStruct((T, ATTN_WIDTH), BF16),
        scratch_shapes=[pltpu.VMEM((d, nk), BF16),
                        pltpu.VMEM((d, S), BF16),
                        pltpu.VMEM((d, S), BF16),
                        pltpu.VMEM((d, R * tq), BF16),
                        pltpu.VMEM((8, tq), F32),
                        pltpu.VMEM((8, tq), F32),
                        pltpu.VMEM((d, R * tq), F32),
                        pltpu.VMEM((d, R * tq), F32)],
        compiler_params=_cparams(("parallel", "parallel", "arbitrary")),
        name="nsa",
    )(rq, kc, vc, rq, proj, rq, proj, gates_t)


def _merge_kernel(ys_ref, ws_ref, ya_ref, wa_ref, gs_ref, ga_ref, o_ref):
    s = jnp.dot(ys_ref[...], ws_ref[...], preferred_element_type=F32)
    a = jnp.dot(ya_ref[...], wa_ref[...], preferred_element_type=F32)
    gs = jax.nn.sigmoid(gs_ref[...].astype(F32))
    ga = jax.nn.sigmoid(ga_ref[...].astype(F32))
    o_ref[...] = (gs * s + ga * a).astype(o_ref.dtype)


def _merge(y_ssm, w_s, y_attn, w_a, proj, tm=512, tn=512):
    T = y_ssm.shape[0]
    return pl.pallas_call(
        _merge_kernel,
        grid=(D_MODEL // tn, T // tm),
        in_specs=[pl.BlockSpec((tm, D_INNER), lambda n, m: (m, 0)),
                  pl.BlockSpec((D_INNER, tn), lambda n, m: (0, n)),
                  pl.BlockSpec((tm, ATTN_WIDTH), lambda n, m: (m, 0)),
                  pl.BlockSpec((ATTN_WIDTH, tn), lambda n, m: (0, n)),
                  pl.BlockSpec((tm, tn), lambda n, m: (m, COL_GS // tn + n)),
                  pl.BlockSpec((tm, tn), lambda n, m: (m, COL_GA // tn + n))],
        out_specs=pl.BlockSpec((tm, tn), lambda n, m: (m, n)),
        out_shape=jax.ShapeDtypeStruct((T, D_MODEL), BF16),
        compiler_params=_cparams(("parallel", "parallel")),
        name="merge",
    )(y_ssm, w_s, y_attn, w_a, proj, proj)


def _mix_out_kernel(mg_ref, w_ref, x_ref, nw_ref, x1_ref, h_ref):
    x1 = x_ref[...] + jnp.dot(mg_ref[...], w_ref[...], preferred_element_type=F32)
    x1_ref[...] = x1
    h_ref[...] = _rms(x1, nw_ref[...]).astype(h_ref.dtype)


def _mix_out(merged, w, x2d, norm_w, tm=256):
    T = x2d.shape[0]
    return pl.pallas_call(
        _mix_out_kernel,
        grid=(T // tm,),
        in_specs=[pl.BlockSpec((tm, D_MODEL), lambda m: (m, 0)),
                  pl.BlockSpec((D_MODEL, D_MODEL), lambda m: (0, 0)),
                  pl.BlockSpec((tm, D_MODEL), lambda m: (m, 0)),
                  pl.BlockSpec((1, D_MODEL), lambda m: (0, 0))],
        out_specs=[pl.BlockSpec((tm, D_MODEL), lambda m: (m, 0)),
                   pl.BlockSpec((tm, D_MODEL), lambda m: (m, 0))],
        out_shape=[jax.ShapeDtypeStruct((T, D_MODEL), F32),
                   jax.ShapeDtypeStruct((T, D_MODEL), BF16)],
        compiler_params=_cparams(("parallel",)),
        name="mix_out",
    )(merged, w, x2d, norm_w)


def _ffn_up_kernel(h_ref, hp_ref, wg_ref, wu_ref, cw_ref, cb_ref, o_ref, gbuf, *, tm, tiles_per_seq):
    m = pl.program_id(1)
    wg = wg_ref[...]
    h = h_ref[...]
    gbuf[TAIL:TAIL + tm, :] = jnp.dot(h, wg, preferred_element_type=F32)
    gprev = jnp.dot(hp_ref[...], wg, preferred_element_type=F32)
    gbuf[0:TAIL, :] = jnp.where(m % tiles_per_seq == 0, 0.0, gprev)
    acc = cb_ref[...]
    for k in range(FFN_CONV):
        off = TAIL - (FFN_CONV - 1) + k
        acc = acc + cw_ref[k:k + 1, :] * gbuf[off:off + tm, :]
    u = jnp.dot(h, wu_ref[...], preferred_element_type=F32)
    o_ref[...] = (_silu(acc) * u).astype(o_ref.dtype)


def _ffn_up(h, wg, wu, conv_w, conv_b, S, tm=512, tn=512):
    T = h.shape[0]
    tps = S // tm
    return pl.pallas_call(
        functools.partial(_ffn_up_kernel, tm=tm, tiles_per_seq=tps),
        grid=(D_FF // tn, T // tm),
        in_specs=[pl.BlockSpec((tm, D_MODEL), lambda n, m: (m, 0)),
                  pl.BlockSpec((TAIL, D_MODEL), lambda n, m: (jnp.maximum(m * (tm // TAIL) - 1, 0), 0)),
                  pl.BlockSpec((D_MODEL, tn), lambda n, m: (0, n)),
                  pl.BlockSpec((D_MODEL, tn), lambda n, m: (0, n)),
                  pl.BlockSpec((FFN_CONV, tn), lambda n, m: (0, n)),
                  pl.BlockSpec((1, tn), lambda n, m: (0, n))],
        out_specs=pl.BlockSpec((tm, tn), lambda n, m: (m, n)),
        out_shape=jax.ShapeDtypeStruct((T, D_FF), BF16),
        scratch_shapes=[pltpu.VMEM((tm + TAIL, tn), F32)],
        compiler_params=_cparams(("parallel", "parallel")),
        name="ffn_up",
    )(h, h, wg, wu, conv_w, conv_b.reshape(1, D_FF))


def _ffn_down_kernel(a_ref, w_ref, x_ref, nw_ref, x2_ref, h_ref, acc):
    k = pl.program_id(1)

    @pl.when(k == 0)
    def _():
        acc[...] = x_ref[...]

    acc[...] += jnp.dot(a_ref[...], w_ref[...], preferred_element_type=F32)

    @pl.when(k == pl.num_programs(1) - 1)
    def _():
        x2 = acc[...]
        x2_ref[...] = x2
        h_ref[...] = _rms(x2, nw_ref[...]).astype(h_ref.dtype)


def _ffn_down(act, w, x1, norm_w, tm=512, tk=512):
    T = x1.shape[0]
    return pl.pallas_call(
        _ffn_down_kernel,
        grid=(T // tm, D_FF // tk),
        in_specs=[pl.BlockSpec((tm, tk), lambda m, k: (m, k)),
                  pl.BlockSpec((tk, D_MODEL), lambda m, k: (k, 0)),
                  pl.BlockSpec((tm, D_MODEL), lambda m, k: (m, 0)),
                  pl.BlockSpec((1, D_MODEL), lambda m, k: (0, 0))],
        out_specs=[pl.BlockSpec((tm, D_MODEL), lambda m, k: (m, 0)),
                   pl.BlockSpec((tm, D_MODEL), lambda m, k: (m, 0))],
        out_shape=[jax.ShapeDtypeStruct((T, D_MODEL), F32),
                   jax.ShapeDtypeStruct((T, D_MODEL), BF16)],
        scratch_shapes=[pltpu.VMEM((tm, D_MODEL), F32)],
        compiler_params=_cparams(("parallel", "arbitrary")),
        name="ffn_down",
    )(act, w, x1, norm_w)


def _ple_kernel(h_ref, wg_ref, p_ref, wp_ref, x_ref, nw_ref, o_ref):
    gate = jax.nn.sigmoid(jnp.dot(h_ref[...], wg_ref[...], preferred_element_type=F32))
    emb = jnp.dot(p_ref[...].astype(BF16), wp_ref[...], preferred_element_type=F32)
    o_ref[...] = _rms(x_ref[...] + gate * emb, nw_ref[...])


def _ple(h, wg, p2d, wp, x2, norm_w, tm=256):
    T = x2.shape[0]
    return pl.pallas_call(
        _ple_kernel,
        grid=(T // tm,),
        in_specs=[pl.BlockSpec((tm, D_MODEL), lambda m: (m, 0)),
                  pl.BlockSpec((D_MODEL, D_MODEL), lambda m: (0, 0)),
                  pl.BlockSpec((tm, PLE_DIM), lambda m: (m, 0)),
                  pl.BlockSpec((PLE_DIM, D_MODEL), lambda m: (0, 0)),
                  pl.BlockSpec((tm, D_MODEL), lambda m: (m, 0)),
                  pl.BlockSpec((1, D_MODEL), lambda m: (0, 0))],
        out_specs=pl.BlockSpec((tm, D_MODEL), lambda m: (m, 0)),
        out_shape=jax.ShapeDtypeStruct((T, D_MODEL), F32),
        compiler_params=_cparams(("parallel",)),
        name="ple",
    )(h, wg, p2d, wp, x2, norm_w)


def _split_w_in(w):
    sizes = (D_INNER, SSM_CONV_DIM, SSM_HEADS, ATTN_WIDTH) + (KV_WIDTH,) * 6 + (3 * ATTN_HEADS, D_MODEL, D_MODEL)
    offs = [0]
    for s in sizes:
        offs.append(offs[-1] + s)
    seg = lambda i: w[:, offs[i]:offs[i + 1]]
    wide = jnp.concatenate([seg(0), seg(1)] + [seg(i) for i in range(3, 10)] + [seg(11), seg(12)], axis=1)
    pad = N_SMALL - SSM_HEADS - 3 * ATTN_HEADS
    small = jnp.concatenate([seg(2), seg(10), jnp.zeros((w.shape[0], pad), w.dtype)], axis=1)
    return wide.astype(BF16), small


def _layer(x2d, p2d, B, S, norm_mix_w, w_in, ssm_conv_w, ssm_conv_b, ssm_dt_bias, ssm_a_log, ssm_d,
           ssm_norm_w, cmp_pe_k, cmp_pe_v, cmp_wk1, cmp_wk2, cmp_wv1, cmp_wv2,
           w_ssm_branch, w_attn_branch, w_mix_out, norm_ffn_w, ffn_w_gate, ffn_w_up,
           ffn_conv_w, ffn_conv_b, ffn_w_down, ple_norm_w, ple_w_gate, ple_w_proj, out_norm_w):
    G, R = ATTN_KV_GROUPS, ATTN_REP
    w_wide, w_small = _split_w_in(w_in)
    h, small = _norm_small(x2d, norm_mix_w.reshape(1, D_MODEL), w_small)
    proj = _matmul(h, w_wide, BF16)

    y_ssm = _ssd(proj, small, ssm_conv_w, ssm_conv_b, ssm_dt_bias, ssm_a_log, ssm_d, ssm_norm_w, B, S)

    rq = _rope(proj, S)
    kc = _compress(rq[:, ROPE_KC:ROPE_KC + KV_WIDTH], cmp_pe_k, cmp_wk1, cmp_wk2, B, S)
    vc = _compress(proj[:, COL_KV + KV_WIDTH:COL_KV + 2 * KV_WIDTH], cmp_pe_v, cmp_wv1, cmp_wv2, B, S)
    g_nsa = small[:, SSM_HEADS:SSM_HEADS + 3 * ATTN_HEADS].reshape(B, S, 3, G, R)
    gates_t = g_nsa.transpose(0, 3, 2, 4, 1).reshape(B, G, 3 * R, S)
    gates_t = jnp.pad(gates_t, ((0, 0), (0, 0), (0, 16 - 3 * R), (0, 0)))
    y_attn = _nsa(rq, proj, kc, vc, gates_t, B, S)

    merged = _merge(y_ssm, w_ssm_branch.astype(BF16), y_attn, w_attn_branch.astype(BF16), proj)
    x1, h2 = _mix_out(merged, w_mix_out.astype(BF16), x2d, norm_ffn_w.reshape(1, D_MODEL))

    act = _ffn_up(h2, ffn_w_gate.astype(BF16), ffn_w_up.astype(BF16), ffn_conv_w, ffn_conv_b, S)
    x2, h3 = _ffn_down(act, ffn_w_down.astype(BF16), x1, ple_norm_w.reshape(1, D_MODEL))
    return _ple(h3, ple_w_gate.astype(BF16), p2d, ple_w_proj.astype(BF16), x2, out_norm_w.reshape(1, D_MODEL))


def kernel(x, p, norm_mix_w, w_in, ssm_conv_w, ssm_conv_b, ssm_dt_bias, ssm_a_log, ssm_d, ssm_norm_w, cmp_pe_k, cmp_pe_v, cmp_wk1, cmp_wk2, cmp_wv1, cmp_wv2, w_ssm_branch, w_attn_branch, w_mix_out, norm_ffn_w, ffn_w_gate, ffn_w_up, ffn_conv_w, ffn_conv_b, ffn_w_down, ple_norm_w, ple_w_gate, ple_w_proj, final_norm_w):
    B, S, D = x.shape
    depth = w_in.shape[0]
    assert depth == 1, "the final norm is fused into the (single) layer's last kernel"
    x2d = x.reshape(B * S, D)
    out = _layer(x2d, p[0].reshape(B * S, PLE_DIM), B, S, norm_mix_w[0], w_in[0], ssm_conv_w[0], ssm_conv_b[0],
                 ssm_dt_bias[0], ssm_a_log[0], ssm_d[0], ssm_norm_w[0], cmp_pe_k[0], cmp_pe_v[0],
                 cmp_wk1[0], cmp_wk2[0], cmp_wv1[0], cmp_wv2[0], w_ssm_branch[0], w_attn_branch[0],
                 w_mix_out[0], norm_ffn_w[0], ffn_w_gate[0], ffn_w_up[0], ffn_conv_w[0], ffn_conv_b[0],
                 ffn_w_down[0], ple_norm_w[0], ple_w_gate[0], ple_w_proj[0], final_norm_w)
    return out.reshape(B, S, D)
```

```python
import functools
import math

import jax
import jax.numpy as jnp
from jax import lax
from jax.experimental import pallas as pl
from jax.experimental.pallas import tpu as pltpu

F32 = jnp.float32
BF16 = jnp.bfloat16

D_MODEL = 2048
PLE_DIM = 256
EPS = 1e-6
D_INNER = 4096
SSM_HEAD_DIM = 64
SSM_HEADS = 64
SSM_GROUPS = 8
SSM_STATE = 128
SSM_CONV = 4
SSM_CHUNK = 128
SSM_BC = SSM_GROUPS * SSM_STATE
SSM_CONV_DIM = D_INNER + 2 * SSM_BC
ATTN_HEADS = 16
ATTN_HEAD_DIM = 128
ATTN_KV_GROUPS = 4
ATTN_REP = ATTN_HEADS // ATTN_KV_GROUPS
ATTN_WIDTH = ATTN_HEADS * ATTN_HEAD_DIM
KV_WIDTH = ATTN_KV_GROUPS * ATTN_HEAD_DIM
CMP_BLOCK = 32
CMP_STRIDE = 16
SEL_BLOCK = 64
N_SEL = 16
WINDOW = 512
ROPE_THETA = 500000.0
ROPE_DIM = ATTN_HEAD_DIM // 4
D_FF = 5632
FFN_CONV = 3
NEG_INF = -1e30
FORCE_SCORE = 1e4

LANES = 128
TAIL = 8
FIX = 16
VMEM_LIMIT = 56 * 1024 * 1024

NAT_DT = D_INNER + SSM_CONV_DIM
NAT_Q = NAT_DT + SSM_HEADS
NAT_GN = NAT_Q + ATTN_WIDTH + 6 * KV_WIDTH
NAT_GM = NAT_GN + 3 * ATTN_HEADS
COL_XBC = D_INNER
COL_QKV = COL_XBC + SSM_CONV_DIM
QKV_WIDTH = ATTN_WIDTH + 6 * KV_WIDTH
COL_GM = COL_QKV + QKV_WIDTH
N_WIDE = COL_GM + 2 * D_MODEL
WTILE = 1024
N_SMALL = 128
GATE_ROWS = 16
QKV_KC, QKV_VC, QKV_KS, QKV_VS, QKV_KW, QKV_VW = (
    (ATTN_WIDTH + i * KV_WIDTH) // ATTN_HEAD_DIM for i in range(6))


def _cparams(sem):
    return pltpu.CompilerParams(dimension_semantics=sem, vmem_limit_bytes=VMEM_LIMIT)


def _resident(shape):
    return pl.BlockSpec(shape, lambda *_: (0,) * len(shape), pipeline_mode=pl.Buffered(1))


def _rms(xf, w):
    return xf * lax.rsqrt(jnp.mean(xf * xf, axis=-1, keepdims=True) + EPS) * w


def _silu(v):
    return v * jax.nn.sigmoid(v)


def _causal_conv(x, top, w_ref, bias_ref):
    taps = w_ref.shape[0]
    bias = bias_ref[...]
    y = bias + w_ref[taps - 1:taps, :] * x
    for s in range(1, taps):
        y = y + w_ref[taps - 1 - s:taps - s, :] * pltpu.roll(x, s, 0)
    top[TAIL:TAIL + FIX, :] = x[0:FIX, :]
    y_top = bias
    for k in range(taps):
        off = TAIL - (taps - 1) + k
        y_top = y_top + w_ref[k:k + 1, :] * top[off:off + FIX, :]
    top[0:TAIL, :] = x[x.shape[0] - TAIL:, :]
    return y, y_top


def _regroup_kernel(a_ref, b_ref, o_ref):
    n = pl.program_id(1)
    nblk = WTILE // LANES

    def shifted(sh):
        lane = lax.broadcasted_iota(jnp.int32, (a_ref.shape[0], LANES), 1)
        for k in range(nblk):
            lo = a_ref[:, k * LANES:(k + 1) * LANES]
            hi = a_ref[:, (k + 1) * LANES:(k + 2) * LANES] if k + 1 < nblk else b_ref[...]
            v = jnp.where(lane < LANES - sh, pltpu.roll(lo, LANES - sh, 1), pltpu.roll(hi, LANES - sh, 1))
            o_ref[:, k * LANES:(k + 1) * LANES] = v.astype(o_ref.dtype)

    @pl.when(n < COL_QKV // WTILE)
    def _():
        o_ref[...] = a_ref[...].astype(o_ref.dtype)

    @pl.when((n >= COL_QKV // WTILE) & (n < COL_GM // WTILE))
    def _():
        shifted(NAT_Q - COL_QKV)

    @pl.when(n >= COL_GM // WTILE)
    def _():
        shifted(NAT_GM - COL_GM)


def _regroup(w, tk=512):
    K = w.shape[0]
    per = WTILE // LANES
    return pl.pallas_call(
        _regroup_kernel,
        grid=(K // tk, N_WIDE // WTILE),
        in_specs=[pl.BlockSpec((tk, WTILE), lambda k, n: (k, n)),
                  pl.BlockSpec((tk, LANES), lambda k, n: (k, per * (n + 1)))],
        out_specs=pl.BlockSpec((tk, WTILE), lambda k, n: (k, n)),
        out_shape=jax.ShapeDtypeStruct((K, N_WIDE), BF16),
        compiler_params=_cparams(("parallel", "parallel")),
        name="regroup",
    )(w, w)


def _small_weight(w):
    G, R = ATTN_KV_GROUPS, ATTN_REP
    w_dt = w[:, NAT_DT:NAT_DT + SSM_HEADS]
    w_g = w[:, NAT_GN:NAT_GN + 3 * ATTN_HEADS].reshape(-1, 3, G, R).transpose(0, 2, 1, 3).reshape(-1, G, 3 * R)
    w_g = jnp.pad(w_g, ((0, 0), (0, 0), (0, GATE_ROWS - 3 * R))).reshape(-1, G * GATE_ROWS)
    return jnp.concatenate([w_dt, w_g], axis=1)


def _norm_small_kernel(x_ref, nw_ref, ws_ref, h_ref, small_ref, smallt_ref):
    h = _rms(x_ref[...], nw_ref[...])
    h_ref[...] = h.astype(BF16)
    small = jnp.dot(h, ws_ref[...], preferred_element_type=F32, precision=lax.Precision.HIGHEST)
    small_ref[...] = small
    smallt_ref[0] = small.T


def _norm_small(x2d, norm_w, w_small, B, S, tm=512):
    T = x2d.shape[0]
    per = S // tm
    return pl.pallas_call(
        _norm_small_kernel,
        grid=(T // tm,),
        in_specs=[pl.BlockSpec((tm, D_MODEL), lambda i: (i, 0)),
                  pl.BlockSpec((1, D_MODEL), lambda i: (0, 0)),
                  pl.BlockSpec((D_MODEL, N_SMALL), lambda i: (0, 0))],
        out_specs=[pl.BlockSpec((tm, D_MODEL), lambda i: (i, 0)),
                   pl.BlockSpec((tm, N_SMALL), lambda i: (i, 0)),
                   pl.BlockSpec((1, N_SMALL, tm), lambda i: (i // per, 0, i % per))],
        out_shape=[jax.ShapeDtypeStruct((T, D_MODEL), BF16),
                   jax.ShapeDtypeStruct((T, N_SMALL), F32),
                   jax.ShapeDtypeStruct((B, N_SMALL, S), F32)],
        compiler_params=_cparams(("parallel",)),
        name="norm_small",
    )(x2d, norm_w, w_small)


def _proj_act_kernel(h_ref, w_ref, o_ref, *, act):
    acc = jnp.dot(h_ref[...], w_ref[...], preferred_element_type=F32)
    o_ref[...] = act(acc).astype(o_ref.dtype)


def _proj_conv_kernel(h_ref, w_ref, cw_ref, cb_ref, o_ref, buf, *, tm, tiles_per_seq):
    m = pl.program_id(1)

    @pl.when(m % tiles_per_seq == 0)
    def _():
        buf[0:TAIL, :] = jnp.zeros((TAIL, buf.shape[1]), F32)

    acc = jnp.dot(h_ref[...], w_ref[...], preferred_element_type=F32)
    y, y_top = _causal_conv(acc, buf, cw_ref, cb_ref)
    o_ref[...] = _silu(y).astype(o_ref.dtype)
    o_ref[0:FIX, :] = _silu(y_top).astype(o_ref.dtype)


def _proj_rope_kernel(h_ref, w_ref, ca_ref, cb_ref, cc_ref, o_ref, *, scale):
    n = pl.program_id(0)
    d = ATTN_HEAD_DIM
    half = ROPE_DIM // 2
    is_q = n < ATTN_WIDTH // WTILE
    acc = jnp.dot(h_ref[...], w_ref[...], preferred_element_type=F32)
    ca, cb, cc = ca_ref[...], cb_ref[...], cc_ref[...]
    q_scale = jnp.where(is_q, scale, 1.0).astype(F32)
    for hd in range(WTILE // d):
        xh = acc[:, hd * d:(hd + 1) * d]
        up = pltpu.roll(xh, d - half, 1)
        dn = pltpu.roll(xh, half, 1)
        roped = (xh * ca + up * cb + dn * cc) * q_scale
        if hd >= KV_WIDTH // d:
            roped = jnp.where(is_q, roped, xh)
        o_ref[:, hd * d:(hd + 1) * d] = roped.astype(o_ref.dtype)


def _rope_tables(S):
    half = ROPE_DIM // 2
    pos = jnp.arange(S)
    inv = ROPE_THETA ** (-(jnp.arange(half, dtype=F32) * 2.0 / ROPE_DIM))
    ang = pos.astype(F32)[:, None] * inv[None, :]
    cos, sin = jnp.cos(ang), jnp.sin(ang)
    rest = ATTN_HEAD_DIM - ROPE_DIM
    ca = jnp.concatenate([cos, cos, jnp.ones((S, rest), F32)], axis=1)
    cb = jnp.concatenate([-sin, jnp.zeros((S, ATTN_HEAD_DIM - half), F32)], axis=1)
    cc = jnp.concatenate([jnp.zeros((S, half), F32), sin, jnp.zeros((S, rest), F32)], axis=1)
    return ca, cb, cc


def _proj(kind, h, w_wide, col0, width, S, extra=(), tm=1024):
    T, K = h.shape
    tn = WTILE
    c0 = col0 // tn
    per = S // tm
    in_specs = [pl.BlockSpec((tm, K), lambda n, m: (m, 0)),
                pl.BlockSpec((K, tn), lambda n, m: (0, c0 + n))]
    scratch = []
    sem = ("parallel", "parallel")
    if kind == "silu":
        body = functools.partial(_proj_act_kernel, act=_silu)
    elif kind == "sigmoid":
        body = functools.partial(_proj_act_kernel, act=jax.nn.sigmoid)
    elif kind == "conv":
        body = functools.partial(_proj_conv_kernel, tm=tm, tiles_per_seq=per)
        taps = extra[0].shape[0]
        in_specs += [pl.BlockSpec((taps, tn), lambda n, m: (0, n)),
                     pl.BlockSpec((1, tn), lambda n, m: (0, n))]
        scratch = [pltpu.VMEM((TAIL + FIX, tn), F32)]
        sem = ("parallel", "arbitrary")
    else:
        body = functools.partial(_proj_rope_kernel, scale=math.log2(math.e) / math.sqrt(ATTN_HEAD_DIM))
        tab = pl.BlockSpec((tm, ATTN_HEAD_DIM), lambda n, m: (m % per, 0))
        in_specs += [tab, tab, tab]
    return pl.pallas_call(
        body,
        grid=(width // tn, T // tm),
        in_specs=in_specs,
        out_specs=pl.BlockSpec((tm, tn), lambda n, m: (m, n)),
        out_shape=jax.ShapeDtypeStruct((T, width), BF16),
        scratch_shapes=scratch,
        compiler_params=_cparams(sem),
        name="proj_" + kind,
    )(h, w_wide, *extra)


def _ssd_kernel(x_ref, b_ref, c_ref, z_ref, dtr_ref, dtrT_ref, dtb_ref, dtbT_ref, alog_ref, alogT_ref,
                dsk_ref, nw_ref, y_ref, h_scr):
    L = SSM_CHUNK
    P = SSM_HEAD_DIM
    c = pl.program_id(2)

    @pl.when(c == 0)
    def _():
        h_scr[...] = jnp.zeros_like(h_scr)

    xs = x_ref[...].astype(F32)
    bmb = b_ref[...]
    cmb = c_ref[...]

    dt = jax.nn.softplus(dtr_ref[0, 0] + dtb_ref[0])
    dtT = jax.nn.softplus(dtrT_ref[0] + dtbT_ref[0])
    a = -jnp.exp(alog_ref[0])
    aT = -jnp.exp(alogT_ref[0])
    ri = lax.broadcasted_iota(jnp.int32, (L, L), 0)
    ci = lax.broadcasted_iota(jnp.int32, (L, L), 1)
    tril = ci <= ri
    hp = lax.Precision.HIGHEST
    acum = jnp.dot(tril.astype(F32), dt * a, preferred_element_type=F32, precision=hp)
    acumT = jnp.dot(dtT * aT, (ri <= ci).astype(F32), preferred_element_type=F32, precision=hp)

    cb = lax.dot_general(cmb, bmb, (((1,), (1,)), ((), ())), preferred_element_type=F32)
    yoff = jnp.dot(cmb, h_scr[...].astype(BF16), preferred_element_type=F32)

    lane = lax.broadcasted_iota(jnp.int32, (L, 2 * P), 1)
    lo = lane < P

    def pair(v0, v1):
        return jnp.where(lo, v0, v1)

    dsk = dsk_ref[...]
    ys, xds, cds = [], [], []
    for pr in range(SSM_HEADS // SSM_GROUPS // 2):
        r0, r1 = 2 * pr, 2 * pr + 1
        sl = slice(pr * 2 * P, (pr + 1) * 2 * P)
        xp = xs[:, sl]
        xdt = xp * pair(dt[:, r0:r0 + 1], dt[:, r1:r1 + 1])
        xdtb = xdt.astype(BF16)
        yd = []
        for r in (r0, r1):
            seg = acum[:, r:r + 1] - acumT[r:r + 1, :]
            dec = jnp.exp(jnp.where(tril, seg, -jnp.inf))
            yd.append(jnp.dot((cb * dec).astype(BF16), xdtb, preferred_element_type=F32))
        ea = pair(jnp.exp(acum[:, r0:r0 + 1]), jnp.exp(acum[:, r1:r1 + 1]))
        ys.append(pair(yd[0], yd[1]) + yoff[:, sl] * ea + dsk[:, sl] * xp)
        al0, al1 = acum[L - 1:L, r0:r0 + 1], acum[L - 1:L, r1:r1 + 1]
        xds.append(xdt * pair(jnp.exp(al0 - acum[:, r0:r0 + 1]), jnp.exp(al1 - acum[:, r1:r1 + 1])))
        cds.append(jnp.where(lo[0:1, :], jnp.exp(al0), jnp.exp(al1)))
    xds = jnp.concatenate(xds, axis=1).astype(BF16)
    cd = jnp.concatenate(cds, axis=1)
    st = jnp.dot(bmb.astype(F32).T.astype(BF16), xds, preferred_element_type=F32)
    h_scr[...] = h_scr[...] * cd + st

    y = jnp.concatenate(ys, axis=1) * z_ref[...].astype(F32)
    y_ref[...] = _rms(y, nw_ref[...]).astype(y_ref.dtype)


def _ssd(xbc, zs, small, small_t, dt_bias, a_log, d_skip, norm_w, B, S):
    T = B * S
    L, G, N = SSM_CHUNK, SSM_GROUPS, SSM_STATE
    R = SSM_HEADS // G
    gw = D_INNER // G
    nc = S // L
    dtr_g = small[:, :SSM_HEADS].reshape(B, S, G, R).transpose(0, 2, 1, 3)
    dtb = dt_bias.reshape(G, 1, R)
    dtbT = dt_bias.reshape(G, R, 1)
    alog = a_log.reshape(G, 1, R)
    alogT = a_log.reshape(G, R, 1)
    dsk = jnp.repeat(d_skip, SSM_HEAD_DIM).reshape(1, D_INNER)
    nw = norm_w.reshape(1, D_INNER)

    row = lambda b, g, c: b * nc + c
    in_specs = [
        pl.BlockSpec((L, gw), lambda b, g, c: (row(b, g, c), g)),
        pl.BlockSpec((L, N), lambda b, g, c: (row(b, g, c), D_INNER // N + g)),
        pl.BlockSpec((L, N), lambda b, g, c: (row(b, g, c), (D_INNER + SSM_BC) // N + g)),
        pl.BlockSpec((L, gw), lambda b, g, c: (row(b, g, c), g)),
        pl.BlockSpec((1, 1, L, R), lambda b, g, c: (b, g, c, 0)),
        pl.BlockSpec((1, R, L), lambda b, g, c: (b, g, c)),
        pl.BlockSpec((1, 1, R), lambda b, g, c: (g, 0, 0)),
        pl.BlockSpec((1, R, 1), lambda b, g, c: (g, 0, 0)),
        pl.BlockSpec((1, 1, R), lambda b, g, c: (g, 0, 0)),
        pl.BlockSpec((1, R, 1), lambda b, g, c: (g, 0, 0)),
        pl.BlockSpec((1, gw), lambda b, g, c: (0, g)),
        pl.BlockSpec((1, gw), lambda b, g, c: (0, g)),
    ]
    return pl.pallas_call(
        _ssd_kernel,
        grid=(B, G, nc),
        in_specs=in_specs,
        out_specs=pl.BlockSpec((L, gw), lambda b, g, c: (row(b, g, c), g)),
        out_shape=jax.ShapeDtypeStruct((T, D_INNER), BF16),
        scratch_shapes=[pltpu.VMEM((N, gw), F32)],
        compiler_params=_cparams(("parallel", "parallel", "arbitrary")),
        name="ssd",
    )(xbc, xbc, xbc, zs, dtr_g, small_t, dtb, dtbT, alog, alogT, dsk, nw)


def _compress_kernel(k_ref, v_ref, pek_ref, pev_ref, wk1_ref, wv1_ref, wk2_ref, wv2_ref, kc_ref, vct_ref, xf):
    d = ATTN_HEAD_DIM
    nsub = xf.shape[0] // CMP_STRIDE

    def mlp(src_ref, pe_ref, w1_ref, w2_ref):
        xf[...] = src_ref[...].astype(F32)
        u = jnp.zeros((nsub, d), F32)
        v = jnp.zeros((nsub, d), F32)
        for l in range(CMP_STRIDE):
            tok = xf[pl.ds(l, nsub, stride=CMP_STRIDE), :]
            l2 = CMP_STRIDE + l
            u = u + jnp.dot((tok + pe_ref[l:l + 1, :]).astype(BF16), w1_ref[l * d:(l + 1) * d, :],
                            preferred_element_type=F32)
            v = v + jnp.dot((tok + pe_ref[l2:l2 + 1, :]).astype(BF16), w1_ref[l2 * d:(l2 + 1) * d, :],
                            preferred_element_type=F32)
        pre = u + pltpu.roll(v, nsub - 1, 0)
        return jnp.dot(_silu(pre).astype(BF16), w2_ref[...], preferred_element_type=F32)

    kc_ref[0, 0] = mlp(k_ref, pek_ref, wk1_ref, wk2_ref).astype(kc_ref.dtype)
    vct_ref[0, 0] = mlp(v_ref, pev_ref, wv1_ref, wv2_ref).T.astype(vct_ref.dtype)


def _compress(qkv, pe_k, pe_v, wk1, wv1, wk2, wv2, B, S):
    G, d = ATTN_KV_GROUPS, ATTN_HEAD_DIM
    nsub = S // CMP_STRIDE
    full = lambda shape: pl.BlockSpec(shape, lambda b, g: (0, 0))
    out = pl.BlockSpec((1, 1, nsub, d), lambda b, g: (b, g, 0, 0))
    return pl.pallas_call(
        _compress_kernel,
        grid=(B, G),
        in_specs=[pl.BlockSpec((S, d), lambda b, g: (b, QKV_KC + g)),
                  pl.BlockSpec((S, d), lambda b, g: (b, QKV_VC + g)),
                  full((CMP_BLOCK, d)), full((CMP_BLOCK, d)),
                  full((CMP_BLOCK * d, d)), full((CMP_BLOCK * d, d)),
                  full((d, d)), full((d, d))],
        out_specs=[out, out],
        out_shape=[jax.ShapeDtypeStruct((B, G, nsub, d), BF16),
                   jax.ShapeDtypeStruct((B, G, d, nsub), BF16)],
        scratch_shapes=[pltpu.VMEM((S, d), F32)],
        compiler_params=_cparams(("parallel", "parallel")),
        name="compress",
    )(qkv, qkv, pe_k, pe_v, wk1.astype(BF16), wv1.astype(BF16), wk2.astype(BF16), wv2.astype(BF16))


def _nsa_kernel(q_ref, kc_ref, vct_ref, ks_ref, vs_ref, kw_ref, vw_ref, gt_ref, o_ref,
                vst, vwt, qt, m_scr, l_scr, acc_scr, out_scr, *, tq, tks, n_cmp, n_blk):
    i = pl.program_id(2)
    d = ATTN_HEAD_DIM
    R = ATTN_REP
    heads = [slice(r * tq, (r + 1) * tq) for r in range(R)]

    def transposed(v):
        return v.astype(F32).T.astype(BF16)

    @pl.when(i == 0)
    def _():
        vst[...] = transposed(vs_ref[...])
        vwt[...] = transposed(vw_ref[...])

    for r in range(R):
        qt[:, heads[r]] = transposed(q_ref[:, r * d:(r + 1) * d])
    gate = jax.nn.sigmoid(gt_ref[0])

    nk = kc_ref.shape[2]
    kc = kc_ref[0, 0]
    vct = vct_ref[0, 0]
    n_idx = lax.broadcasted_iota(jnp.int32, (nk, tq), 0)
    t_cmp = i * tq + lax.broadcasted_iota(jnp.int32, (nk, tq), 1)
    cmask = (n_idx * CMP_STRIDE + CMP_BLOCK - 1 <= t_cmp) & (n_idx < n_cmp)
    psum = jnp.zeros((nk, tq), F32)
    for r in range(R):
        s = jnp.where(cmask, jnp.dot(kc, qt[:, heads[r]], preferred_element_type=F32), NEG_INF)
        e = jnp.exp2(s - jnp.max(s, axis=0, keepdims=True))
        p = jnp.where(cmask, e * (1.0 / jnp.sum(e, axis=0, keepdims=True)), 0.0)
        out_scr[:, heads[r]] = gate[r:r + 1, :] * jnp.dot(vct, p.astype(BF16), preferred_element_type=F32)
        psum = psum + p
    bi = lax.broadcasted_iota(jnp.int32, (n_blk, nk), 0)
    ni = lax.broadcasted_iota(jnp.int32, (n_blk, nk), 1)
    ovl = ((ni * CMP_STRIDE < bi * SEL_BLOCK + SEL_BLOCK) & (ni * CMP_STRIDE + CMP_BLOCK > bi * SEL_BLOCK)
           & (ni < n_cmp))
    imp = jnp.dot(jnp.where(ovl, 1.0, 0.0), psum, preferred_element_type=F32, precision=lax.Precision.HIGHEST)
    blk = lax.broadcasted_iota(jnp.int32, (n_blk, tq), 0)
    cur = (i * tq + lax.broadcasted_iota(jnp.int32, (n_blk, tq), 1)) // SEL_BLOCK
    forced = (blk == 0) | (blk == cur) | (blk == cur - 1)
    score = jnp.where(forced, FORCE_SCORE, jnp.where(blk <= cur, imp, NEG_INF))
    rank = jnp.zeros((n_blk, tq), F32)
    for k in range(n_blk):
        sk = score[k:k + 1, :]
        rank = rank + jnp.where((sk > score) | ((sk == score) & (k < blk)), 1.0, 0.0)
    sel = jnp.where(rank < N_SEL, 1.0, 0.0).astype(BF16)

    m_scr[...] = jnp.full(m_scr.shape, NEG_INF, F32)
    l_scr[...] = jnp.zeros_like(l_scr)
    acc_scr[...] = jnp.zeros_like(acc_scr)

    def sel_body(c, carry):
        k0 = pl.multiple_of(c * tks, tks)
        key = k0 + lax.broadcasted_iota(jnp.int32, (tks, tq), 0)
        t = i * tq + lax.broadcasted_iota(jnp.int32, (tks, tq), 1)
        in_blk = (lax.broadcasted_iota(jnp.int32, (tks, n_blk), 1)
                  == (k0 + lax.broadcasted_iota(jnp.int32, (tks, n_blk), 0)) // SEL_BLOCK)
        picked = jnp.dot(jnp.where(in_blk, 1.0, 0.0).astype(BF16), sel, preferred_element_type=F32)
        bias = jnp.where((picked > 0.5) & (key <= t), 0.0, NEG_INF)
        k_chunk = ks_ref[pl.ds(k0, tks), :]
        vt_chunk = vst[:, pl.ds(k0, tks)]
        for r in range(R):
            s = jnp.dot(k_chunk, qt[:, heads[r]], preferred_element_type=F32) + bias
            m_prev = m_scr[r:r + 1, :]
            m_new = jnp.maximum(m_prev, jnp.max(s, axis=0, keepdims=True))
            alpha = jnp.exp2(m_prev - m_new)
            p = jnp.exp2(s - m_new)
            l_scr[r:r + 1, :] = alpha * l_scr[r:r + 1, :] + jnp.sum(p, axis=0, keepdims=True)
            acc_scr[:, heads[r]] = (alpha * acc_scr[:, heads[r]]
                                    + jnp.dot(vt_chunk, p.astype(BF16), preferred_element_type=F32))
            m_scr[r:r + 1, :] = m_new
        return carry

    lax.fori_loop(0, (i * tq + tq - 1) // tks + 1, sel_body, 0)
    for r in range(R):
        o = acc_scr[:, heads[r]] * (1.0 / l_scr[r:r + 1, :])
        out_scr[:, heads[r]] += gate[R + r:R + r + 1, :] * o

    span = WINDOW + tq
    k0 = pl.multiple_of(jnp.maximum(i * tq - WINDOW, 0), tq)
    dist = (i * tq + lax.broadcasted_iota(jnp.int32, (span, tq), 1)
            - (k0 + lax.broadcasted_iota(jnp.int32, (span, tq), 0)))
    bias = jnp.where((dist >= 0) & (dist < WINDOW), 0.0, NEG_INF)
    k_win = kw_ref[pl.ds(k0, span), :]
    vt_win = vwt[:, pl.ds(k0, span)]
    for r in range(R):
        s = jnp.dot(k_win, qt[:, heads[r]], preferred_element_type=F32) + bias
        p = jnp.exp2(s - jnp.max(s, axis=0, keepdims=True))
        o = jnp.dot(vt_win, p.astype(BF16), preferred_element_type=F32) * (1.0 / jnp.sum(p, axis=0, keepdims=True))
        out_scr[:, heads[r]] += gate[2 * R + r:2 * R + r + 1, :] * o

    for r in range(R):
        o_ref[:, r * d:(r + 1) * d] = out_scr[:, heads[r]].T.astype(o_ref.dtype)


def _nsa(qkv, kc, vct, small_t, B, S, tq=256, tks=512):
    assert WINDOW % tq == 0 and S >= WINDOW + tq
    T = B * S
    G, d, R = ATTN_KV_GROUPS, ATTN_HEAD_DIM, ATTN_REP
    nq = S // tq
    nk = kc.shape[2]
    n_cmp = S // CMP_STRIDE - CMP_BLOCK // CMP_STRIDE + 1
    n_blk = S // SEL_BLOCK
    seq = lambda c0: pl.BlockSpec((S, d), lambda b, g, i: (b, c0 + g))
    cmp_spec = pl.BlockSpec((1, 1, nk, d), lambda b, g, i: (b, g, 0, 0))
    gate0 = SSM_HEADS // GATE_ROWS
    return pl.pallas_call(
        functools.partial(_nsa_kernel, tq=tq, tks=tks, n_cmp=n_cmp, n_blk=n_blk),
        grid=(B, G, nq),
        in_specs=[pl.BlockSpec((tq, KV_WIDTH), lambda b, g, i: (b * nq + i, g)),
                  cmp_spec, cmp_spec,
                  seq(QKV_KS), seq(QKV_VS), seq(QKV_KW), seq(QKV_VW),
                  pl.BlockSpec((1, GATE_ROWS, tq), lambda b, g, i: (b, gate0 + g, i))],
        out_specs=pl.BlockSpec((tq, KV_WIDTH), lambda b, g, i: (b * nq + i, g)),
        out_shape=jax.ShapeDtypeStruct((T, ATTN_WIDTH), BF16),
        scratch_shapes=[pltpu.VMEM((d, S), BF16),
                        pltpu.VMEM((d, S), BF16),
                        pltpu.VMEM((d, R * tq), BF16),
                        pltpu.VMEM((8, tq), F32),
                        pltpu.VMEM((8, tq), F32),
                        pltpu.VMEM((d, R * tq), F32),
                        pltpu.VMEM((d, R * tq), F32)],
        compiler_params=_cparams(("parallel", "parallel", "arbitrary")),
        name="nsa",
    )(qkv, kc, vct, qkv, qkv, qkv, qkv, small_t)


def _merge_kernel(ys_ref, ws_ref, ya_ref, wa_ref, gs_ref, ga_ref, o_ref):
    s = jnp.dot(ys_ref[...], ws_ref[...], preferred_element_type=F32)
    a = jnp.dot(ya_ref[...], wa_ref[...], preferred_element_type=F32)
    o_ref[...] = (gs_ref[...].astype(F32) * s + ga_ref[...].astype(F32) * a).astype(o_ref.dtype)


def _merge(y_ssm, w_s, y_attn, w_a, gm, tm=512, tn=512):
    T = y_ssm.shape[0]
    return pl.pallas_call(
        _merge_kernel,
        grid=(D_MODEL // tn, T // tm),
        in_specs=[pl.BlockSpec((tm, D_INNER), lambda n, m: (m, 0)),
                  pl.BlockSpec((D_INNER, tn), lambda n, m: (0, n)),
                  pl.BlockSpec((tm, ATTN_WIDTH), lambda n, m: (m, 0)),
                  pl.BlockSpec((ATTN_WIDTH, tn), lambda n, m: (0, n)),
                  pl.BlockSpec((tm, tn), lambda n, m: (m, n)),
                  pl.BlockSpec((tm, tn), lambda n, m: (m, D_MODEL // tn + n))],
        out_specs=pl.BlockSpec((tm, tn), lambda n, m: (m, n)),
        out_shape=jax.ShapeDtypeStruct((T, D_MODEL), BF16),
        compiler_params=_cparams(("parallel", "parallel")),
        name="merge",
    )(y_ssm, w_s, y_attn, w_a, gm, gm)


def _mix_out_kernel(mg_ref, w_ref, x_ref, nw_ref, x1_ref, h_ref):
    x1 = x_ref[...] + jnp.dot(mg_ref[...], w_ref[...], preferred_element_type=F32)
    x1_ref[...] = x1
    h_ref[...] = _rms(x1, nw_ref[...]).astype(h_ref.dtype)


def _mix_out(merged, w, x2d, norm_w, tm=512):
    T = x2d.shape[0]
    return pl.pallas_call(
        _mix_out_kernel,
        grid=(T // tm,),
        in_specs=[pl.BlockSpec((tm, D_MODEL), lambda m: (m, 0)),
                  _resident((D_MODEL, D_MODEL)),
                  pl.BlockSpec((tm, D_MODEL), lambda m: (m, 0)),
                  pl.BlockSpec((1, D_MODEL), lambda m: (0, 0))],
        out_specs=[pl.BlockSpec((tm, D_MODEL), lambda m: (m, 0)),
                   pl.BlockSpec((tm, D_MODEL), lambda m: (m, 0))],
        out_shape=[jax.ShapeDtypeStruct((T, D_MODEL), F32),
                   jax.ShapeDtypeStruct((T, D_MODEL), BF16)],
        compiler_params=_cparams(("parallel",)),
        name="mix_out",
    )(merged, w, x2d, norm_w)


def _ffn_up_kernel(h_ref, wg_ref, wu_ref, cw_ref, cb_ref, o_ref, gbuf, *, tm, tiles_per_seq):
    m = pl.program_id(1)

    @pl.when(m % tiles_per_seq == 0)
    def _():
        gbuf[0:TAIL, :] = jnp.zeros((TAIL, gbuf.shape[1]), F32)

    h = h_ref[...]
    g = jnp.dot(h, wg_ref[...], preferred_element_type=F32)
    u = jnp.dot(h, wu_ref[...], preferred_element_type=F32)
    y, y_top = _causal_conv(g, gbuf, cw_ref, cb_ref)
    o_ref[...] = (_silu(y) * u).astype(o_ref.dtype)
    o_ref[0:FIX, :] = (_silu(y_top) * u[0:FIX, :]).astype(o_ref.dtype)


def _ffn_up(h, wg, wu, conv_w, conv_b, S, tm=1024, tn=512):
    T = h.shape[0]
    return pl.pallas_call(
        functools.partial(_ffn_up_kernel, tm=tm, tiles_per_seq=S // tm),
        grid=(D_FF // tn, T // tm),
        in_specs=[pl.BlockSpec((tm, D_MODEL), lambda n, m: (m, 0)),
                  pl.BlockSpec((D_MODEL, tn), lambda n, m: (0, n)),
                  pl.BlockSpec((D_MODEL, tn), lambda n, m: (0, n)),
                  pl.BlockSpec((FFN_CONV, tn), lambda n, m: (0, n)),
                  pl.BlockSpec((1, tn), lambda n, m: (0, n))],
        out_specs=pl.BlockSpec((tm, tn), lambda n, m: (m, n)),
        out_shape=jax.ShapeDtypeStruct((T, D_FF), BF16),
        scratch_shapes=[pltpu.VMEM((TAIL + FIX, tn), F32)],
        compiler_params=_cparams(("parallel", "arbitrary")),
        name="ffn_up",
    )(h, wg, wu, conv_w, conv_b.reshape(1, D_FF))


def _ffn_down_kernel(a_ref, w_ref, x_ref, nw_ref, x2_ref, h_ref):
    x2 = x_ref[...] + jnp.dot(a_ref[...], w_ref[...], preferred_element_type=F32)
    x2_ref[...] = x2
    h_ref[...] = _rms(x2, nw_ref[...]).astype(h_ref.dtype)


def _ffn_down(act, w, x1, norm_w, tm=256):
    T = x1.shape[0]
    return pl.pallas_call(
        _ffn_down_kernel,
        grid=(T // tm,),
        in_specs=[pl.BlockSpec((tm, D_FF), lambda m: (m, 0)),
                  _resident((D_FF, D_MODEL)),
                  pl.BlockSpec((tm, D_MODEL), lambda m: (m, 0)),
                  _resident((1, D_MODEL))],
        out_specs=[pl.BlockSpec((tm, D_MODEL), lambda m: (m, 0)),
                   pl.BlockSpec((tm, D_MODEL), lambda m: (m, 0))],
        out_shape=[jax.ShapeDtypeStruct((T, D_MODEL), F32),
                   jax.ShapeDtypeStruct((T, D_MODEL), BF16)],
        compiler_params=_cparams(("parallel",)),
        name="ffn_down",
    )(act, w, x1, norm_w)


def _ple_kernel(h_ref, wg_ref, p_ref, wp_ref, x_ref, nw_ref, o_ref):
    gate = jax.nn.sigmoid(jnp.dot(h_ref[...], wg_ref[...], preferred_element_type=F32))
    emb = jnp.dot(p_ref[...].astype(BF16), wp_ref[...], preferred_element_type=F32)
    o_ref[...] = _rms(x_ref[...] + gate * emb, nw_ref[...])


def _ple(h, wg, p2d, wp, x2, norm_w, tm=512):
    T = x2.shape[0]
    return pl.pallas_call(
        _ple_kernel,
        grid=(T // tm,),
        in_specs=[pl.BlockSpec((tm, D_MODEL), lambda m: (m, 0)),
                  _resident((D_MODEL, D_MODEL)),
                  pl.BlockSpec((tm, PLE_DIM), lambda m: (m, 0)),
                  _resident((PLE_DIM, D_MODEL)),
                  pl.BlockSpec((tm, D_MODEL), lambda m: (m, 0)),
                  pl.BlockSpec((1, D_MODEL), lambda m: (0, 0))],
        out_specs=pl.BlockSpec((tm, D_MODEL), lambda m: (m, 0)),
        out_shape=jax.ShapeDtypeStruct((T, D_MODEL), F32),
        compiler_params=_cparams(("parallel",)),
        name="ple",
    )(h, wg, p2d, wp, x2, norm_w)


def _mixer_inputs(x2d, norm_w, w, ssm_conv_w, ssm_conv_b, B, S):
    w_wide = _regroup(w)
    h, small, small_t = _norm_small(x2d, norm_w.reshape(1, D_MODEL), _small_weight(w), B, S)
    zs = _proj("silu", h, w_wide, 0, D_INNER, S)
    xbc = _proj("conv", h, w_wide, COL_XBC, SSM_CONV_DIM, S,
                extra=(ssm_conv_w, ssm_conv_b.reshape(1, SSM_CONV_DIM)))
    qkv = _proj("rope", h, w_wide, COL_QKV, QKV_WIDTH, S, extra=_rope_tables(S))
    gm = _proj("sigmoid", h, w_wide, COL_GM, 2 * D_MODEL, S)
    return small, small_t, zs, xbc, qkv, gm


def _layer(x2d, p2d, B, S, norm_mix_w, w_in, ssm_conv_w, ssm_conv_b, ssm_dt_bias, ssm_a_log, ssm_d,
           ssm_norm_w, cmp_pe_k, cmp_pe_v, cmp_wk1, cmp_wk2, cmp_wv1, cmp_wv2,
           w_ssm_branch, w_attn_branch, w_mix_out, norm_ffn_w, ffn_w_gate, ffn_w_up,
           ffn_conv_w, ffn_conv_b, ffn_w_down, ple_norm_w, ple_w_gate, ple_w_proj, out_norm_w):
    small, small_t, zs, xbc, qkv, gm = _mixer_inputs(x2d, norm_mix_w, w_in, ssm_conv_w, ssm_conv_b, B, S)
    y_ssm = _ssd(xbc, zs, small, small_t, ssm_dt_bias, ssm_a_log, ssm_d, ssm_norm_w, B, S)
    kc, vct = _compress(qkv, cmp_pe_k, cmp_pe_v, cmp_wk1, cmp_wv1, cmp_wk2, cmp_wv2, B, S)
    y_attn = _nsa(qkv, kc, vct, small_t, B, S)

    merged = _merge(y_ssm, w_ssm_branch.astype(BF16), y_attn, w_attn_branch.astype(BF16), gm)
    x1, h2 = _mix_out(merged, w_mix_out.astype(BF16), x2d, norm_ffn_w.reshape(1, D_MODEL))

    act = _ffn_up(h2, ffn_w_gate.astype(BF16), ffn_w_up.astype(BF16), ffn_conv_w, ffn_conv_b, S)
    x2, h3 = _ffn_down(act, ffn_w_down.astype(BF16), x1, ple_norm_w.reshape(1, D_MODEL))
    return _ple(h3, ple_w_gate.astype(BF16), p2d, ple_w_proj.astype(BF16), x2, out_norm_w.reshape(1, D_MODEL))


def kernel(x, p, norm_mix_w, w_in, ssm_conv_w, ssm_conv_b, ssm_dt_bias, ssm_a_log, ssm_d, ssm_norm_w, cmp_pe_k, cmp_pe_v, cmp_wk1, cmp_wk2, cmp_wv1, cmp_wv2, w_ssm_branch, w_attn_branch, w_mix_out, norm_ffn_w, ffn_w_gate, ffn_w_up, ffn_conv_w, ffn_conv_b, ffn_w_down, ple_norm_w, ple_w_gate, ple_w_proj, final_norm_w):
    B, S, D = x.shape
    depth = w_in.shape[0]
    assert depth == 1, "the final norm is fused into the (single) layer's last kernel"
    x2d = x.reshape(B * S, D)
    out = _layer(x2d, p[0].reshape(B * S, PLE_DIM), B, S, norm_mix_w[0], w_in[0], ssm_conv_w[0], ssm_conv_b[0],
                 ssm_dt_bias[0], ssm_a_log[0], ssm_d[0], ssm_norm_w[0], cmp_pe_k[0], cmp_pe_v[0],
                 cmp_wk1[0], cmp_wk2[0], cmp_wv1[0], cmp_wv2[0], w_ssm_branch[0], w_attn_branch[0],
                 w_mix_out[0], norm_ffn_w[0], ffn_w_gate[0], ffn_w_up[0], ffn_conv_w[0], ffn_conv_b[0],
                 ffn_w_down[0], ple_norm_w[0], ple_w_gate[0], ple_w_proj[0], final_norm_w)
    return out.reshape(B, S, D)
```

```python
import functools
import math

import jax
import jax.numpy as jnp
from jax import lax
from jax.experimental import pallas as pl
from jax.experimental.pallas import tpu as pltpu

F32 = jnp.float32
BF16 = jnp.bfloat16

D_MODEL = 2048
PLE_DIM = 256
EPS = 1e-6
D_INNER = 4096
SSM_HEAD_DIM = 64
SSM_HEADS = 64
SSM_GROUPS = 8
SSM_STATE = 128
SSM_CONV = 4
SSM_CHUNK = 128
SSM_BC = SSM_GROUPS * SSM_STATE
SSM_CONV_DIM = D_INNER + 2 * SSM_BC
ATTN_HEADS = 16
ATTN_HEAD_DIM = 128
ATTN_KV_GROUPS = 4
ATTN_REP = ATTN_HEADS // ATTN_KV_GROUPS
ATTN_WIDTH = ATTN_HEADS * ATTN_HEAD_DIM
KV_WIDTH = ATTN_KV_GROUPS * ATTN_HEAD_DIM
CMP_BLOCK = 32
CMP_STRIDE = 16
SEL_BLOCK = 64
N_SEL = 16
WINDOW = 512
ROPE_THETA = 500000.0
ROPE_DIM = ATTN_HEAD_DIM // 4
D_FF = 5632
FFN_CONV = 3
NEG_INF = -1e30
FORCE_SCORE = 1e4

LANES = 128
TAIL = 8
FIX = 16
VMEM_LIMIT = 56 * 1024 * 1024

NAT_DT = D_INNER + SSM_CONV_DIM
NAT_Q = NAT_DT + SSM_HEADS
NAT_GN = NAT_Q + ATTN_WIDTH + 6 * KV_WIDTH
NAT_GM = NAT_GN + 3 * ATTN_HEADS
QKV_WIDTH = ATTN_WIDTH + 6 * KV_WIDTH
WTILE = 1024
N_SMALL = 128
GATE_ROWS = 16
QKV_KC, QKV_VC, QKV_KS, QKV_VS, QKV_KW, QKV_VW = (
    (ATTN_WIDTH + i * KV_WIDTH) // ATTN_HEAD_DIM for i in range(6))


def _cparams(sem):
    return pltpu.CompilerParams(dimension_semantics=sem, vmem_limit_bytes=VMEM_LIMIT)


def _resident(shape):
    return pl.BlockSpec(shape, lambda *_: (0,) * len(shape), pipeline_mode=pl.Buffered(1))


def _rms(xf, w):
    return xf * lax.rsqrt(jnp.mean(xf * xf, axis=-1, keepdims=True) + EPS) * w


def _silu(v):
    return v * jax.nn.sigmoid(v)


def _causal_conv(x, top, w_ref, bias_ref):
    taps = w_ref.shape[0]
    bias = bias_ref[...]
    y = bias + w_ref[taps - 1:taps, :] * x
    for s in range(1, taps):
        y = y + w_ref[taps - 1 - s:taps - s, :] * pltpu.roll(x, s, 0)
    top[TAIL:TAIL + FIX, :] = x[0:FIX, :]
    y_top = bias
    for k in range(taps):
        off = TAIL - (taps - 1) + k
        y_top = y_top + w_ref[k:k + 1, :] * top[off:off + FIX, :]
    top[0:TAIL, :] = x[x.shape[0] - TAIL:, :]
    return y, y_top


def _cast_per_column_tile(pairs):
    @pl.when(pl.program_id(1) == 0)
    def _():
        for src, dst in pairs:
            dst[...] = src[...].astype(BF16)


def _wide_dot(h_ref, wt_ref, wb):
    _cast_per_column_tile(((wt_ref, wb),))
    return lax.dot_general(h_ref[...], wb[...], (((1,), (1,)), ((), ())), preferred_element_type=F32)


def _small_weight(wt):
    G, R = ATTN_KV_GROUPS, ATTN_REP
    K = wt.shape[1]
    w_dt = wt[NAT_DT:NAT_DT + SSM_HEADS]
    w_g = wt[NAT_GN:NAT_GN + 3 * ATTN_HEADS].reshape(3, G, R, K).transpose(1, 0, 2, 3).reshape(G, 3 * R, K)
    w_g = jnp.pad(w_g, ((0, 0), (0, GATE_ROWS - 3 * R), (0, 0))).reshape(G * GATE_ROWS, K)
    return jnp.concatenate([w_dt, w_g], axis=0)


def _norm_small_kernel(x_ref, nw_ref, ws_ref, h_ref, small_ref, smallt_ref):
    h = _rms(x_ref[...], nw_ref[...])
    h_ref[...] = h.astype(BF16)
    small = lax.dot_general(h, ws_ref[...], (((1,), (1,)), ((), ())), preferred_element_type=F32,
                            precision=lax.Precision.HIGHEST)
    small_ref[...] = small
    smallt_ref[0] = small.T


def _norm_small(x2d, norm_w, w_small, B, S, tm=512):
    T = x2d.shape[0]
    per = S // tm
    return pl.pallas_call(
        _norm_small_kernel,
        grid=(T // tm,),
        in_specs=[pl.BlockSpec((tm, D_MODEL), lambda i: (i, 0)),
                  pl.BlockSpec((1, D_MODEL), lambda i: (0, 0)),
                  pl.BlockSpec((N_SMALL, D_MODEL), lambda i: (0, 0))],
        out_specs=[pl.BlockSpec((tm, D_MODEL), lambda i: (i, 0)),
                   pl.BlockSpec((tm, N_SMALL), lambda i: (i, 0)),
                   pl.BlockSpec((1, N_SMALL, tm), lambda i: (i // per, 0, i % per))],
        out_shape=[jax.ShapeDtypeStruct((T, D_MODEL), BF16),
                   jax.ShapeDtypeStruct((T, N_SMALL), F32),
                   jax.ShapeDtypeStruct((B, N_SMALL, S), F32)],
        compiler_params=_cparams(("parallel",)),
        name="norm_small",
    )(x2d, norm_w, w_small)


def _proj_act_kernel(h_ref, wt_ref, o_ref, wb, *, act):
    o_ref[...] = act(_wide_dot(h_ref, wt_ref, wb)).astype(o_ref.dtype)


def _proj_conv_kernel(h_ref, wt_ref, cw_ref, cb_ref, o_ref, wb, buf, *, tiles_per_seq):
    m = pl.program_id(1)

    @pl.when(m % tiles_per_seq == 0)
    def _():
        buf[0:TAIL, :] = jnp.zeros((TAIL, buf.shape[1]), F32)

    y, y_top = _causal_conv(_wide_dot(h_ref, wt_ref, wb), buf, cw_ref, cb_ref)
    o_ref[...] = _silu(y).astype(o_ref.dtype)
    o_ref[0:FIX, :] = _silu(y_top).astype(o_ref.dtype)


def _proj_rope_kernel(h_ref, wt_ref, ca_ref, cb_ref, cc_ref, o_ref, wb, *, scale):
    n = pl.program_id(0)
    d = ATTN_HEAD_DIM
    half = ROPE_DIM // 2
    is_q = n < ATTN_WIDTH // WTILE
    acc = _wide_dot(h_ref, wt_ref, wb)
    ca, cb, cc = ca_ref[...], cb_ref[...], cc_ref[...]
    q_scale = jnp.where(is_q, scale, 1.0).astype(F32)
    for hd in range(WTILE // d):
        xh = acc[:, hd * d:(hd + 1) * d]
        up = pltpu.roll(xh, d - half, 1)
        dn = pltpu.roll(xh, half, 1)
        roped = (xh * ca + up * cb + dn * cc) * q_scale
        if hd >= KV_WIDTH // d:
            roped = jnp.where(is_q, roped, xh)
        o_ref[:, hd * d:(hd + 1) * d] = roped.astype(o_ref.dtype)


def _rope_tables(S):
    half = ROPE_DIM // 2
    pos = jnp.arange(S)
    inv = ROPE_THETA ** (-(jnp.arange(half, dtype=F32) * 2.0 / ROPE_DIM))
    ang = pos.astype(F32)[:, None] * inv[None, :]
    cos, sin = jnp.cos(ang), jnp.sin(ang)
    rest = ATTN_HEAD_DIM - ROPE_DIM
    ca = jnp.concatenate([cos, cos, jnp.ones((S, rest), F32)], axis=1)
    cb = jnp.concatenate([-sin, jnp.zeros((S, ATTN_HEAD_DIM - half), F32)], axis=1)
    cc = jnp.concatenate([jnp.zeros((S, half), F32), sin, jnp.zeros((S, rest), F32)], axis=1)
    return ca, cb, cc


def _proj(kind, h, wt, row0, width, S, extra=(), tm=1024):
    T, K = h.shape
    tn = WTILE
    per = S // tm
    in_specs = [pl.BlockSpec((tm, K), lambda n, m: (m, 0)),
                pl.BlockSpec((pl.Element(tn), pl.Element(K)),
                             lambda n, m: (pl.multiple_of(row0 + n * tn, TAIL), 0))]
    scratch = [pltpu.VMEM((tn, K), BF16)]
    if kind == "silu":
        body = functools.partial(_proj_act_kernel, act=_silu)
    elif kind == "sigmoid":
        body = functools.partial(_proj_act_kernel, act=jax.nn.sigmoid)
    elif kind == "conv":
        body = functools.partial(_proj_conv_kernel, tiles_per_seq=per)
        taps = extra[0].shape[0]
        in_specs += [pl.BlockSpec((taps, tn), lambda n, m: (0, n)),
                     pl.BlockSpec((1, tn), lambda n, m: (0, n))]
        scratch += [pltpu.VMEM((TAIL + FIX, tn), F32)]
    else:
        body = functools.partial(_proj_rope_kernel, scale=math.log2(math.e) / math.sqrt(ATTN_HEAD_DIM))
        tab = pl.BlockSpec((tm, ATTN_HEAD_DIM), lambda n, m: (m % per, 0))
        in_specs += [tab, tab, tab]
    return pl.pallas_call(
        body,
        grid=(width // tn, T // tm),
        in_specs=in_specs,
        out_specs=pl.BlockSpec((tm, tn), lambda n, m: (m, n)),
        out_shape=jax.ShapeDtypeStruct((T, width), BF16),
        scratch_shapes=scratch,
        compiler_params=_cparams(("parallel", "arbitrary")),
        name="proj_" + kind,
    )(h, wt, *extra)


def _ssd_kernel(x_ref, b_ref, c_ref, z_ref, dtr_ref, dtrT_ref, dtb_ref, dtbT_ref, alog_ref, alogT_ref,
                dsk_ref, nw_ref, y_ref, h_scr):
    L = SSM_CHUNK
    P = SSM_HEAD_DIM
    c = pl.program_id(2)

    @pl.when(c == 0)
    def _():
        h_scr[...] = jnp.zeros_like(h_scr)

    xb = x_ref[...]
    bmb = b_ref[...]
    cmb = c_ref[...]
    n_heads = SSM_HEADS // SSM_GROUPS

    dt = jax.nn.softplus(dtr_ref[0, 0] + dtb_ref[0])
    dtT = jax.nn.softplus(dtrT_ref[0] + dtbT_ref[0])
    a = -jnp.exp(alog_ref[0])
    aT = -jnp.exp(alogT_ref[0])
    ri = lax.broadcasted_iota(jnp.int32, (L, L), 0)
    ci = lax.broadcasted_iota(jnp.int32, (L, L), 1)
    tril = ci <= ri
    hp = lax.Precision.HIGHEST
    acum = jnp.dot(tril.astype(F32), dt * a, preferred_element_type=F32, precision=hp)
    acumT = jnp.dot(dtT * aT, (ri <= ci).astype(F32), preferred_element_type=F32, precision=hp)

    cb = lax.dot_general(cmb, bmb, (((1,), (1,)), ((), ())), preferred_element_type=F32)
    yoff = lax.dot_general(cmb, h_scr[...].astype(BF16), (((1,), (1,)), ((), ())),
                           preferred_element_type=F32)

    lane = lax.broadcasted_iota(jnp.int32, (L, 2 * P), 1)
    lo = lane < P

    def pair(v0, v1):
        return jnp.where(lo, v0, v1)

    dsk = dsk_ref[...]
    ys = []
    for pr in range(n_heads // 2):
        sl = slice(pr * 2 * P, (pr + 1) * 2 * P)
        xp = xb[:, sl]
        yd, ea = [], []
        for r in (2 * pr, 2 * pr + 1):
            acol = jnp.broadcast_to(acum[:, r:r + 1], (L, L))
            dec = jnp.exp(jnp.where(tril, acol - acumT[r:r + 1, :], -jnp.inf))
            yd.append(jnp.dot((cb * dec * dtT[r:r + 1, :]).astype(BF16), xp, preferred_element_type=F32))
            ea.append(jnp.exp(acol))
        ys.append(pair(yd[0], yd[1]) + yoff[:, sl] * pair(ea[0], ea[1]) + dsk[:, sl] * xp.astype(F32))

    a_last = acumT[:, L - 1:L]
    wT = dtT * jnp.exp(a_last - acumT)
    cdT = jnp.exp(a_last)
    xT = xb.astype(F32).T
    xds = jnp.concatenate([xT[r * P:(r + 1) * P, :] * wT[r:r + 1, :] for r in range(n_heads)], axis=0)
    cd = jnp.concatenate([jnp.broadcast_to(cdT[r:r + 1, :], (P, h_scr.shape[1])) for r in range(n_heads)], axis=0)
    st = jnp.dot(xds.astype(BF16), bmb, preferred_element_type=F32)
    h_scr[...] = h_scr[...] * cd + st

    y = jnp.concatenate(ys, axis=1) * z_ref[...].astype(F32)
    y_ref[...] = _rms(y, nw_ref[...]).astype(y_ref.dtype)


def _ssd(xbc, zs, small, small_t, dt_bias, a_log, d_skip, norm_w, B, S):
    T = B * S
    L, G, N = SSM_CHUNK, SSM_GROUPS, SSM_STATE
    R = SSM_HEADS // G
    gw = D_INNER // G
    nc = S // L
    dtr_g = small[:, :SSM_HEADS].reshape(B, S, G, R).transpose(0, 2, 1, 3)
    dtb = dt_bias.reshape(G, 1, R)
    dtbT = dt_bias.reshape(G, R, 1)
    alog = a_log.reshape(G, 1, R)
    alogT = a_log.reshape(G, R, 1)
    dsk = jnp.repeat(d_skip, SSM_HEAD_DIM).reshape(1, D_INNER)
    nw = norm_w.reshape(1, D_INNER)

    row = lambda b, g, c: b * nc + c
    in_specs = [
        pl.BlockSpec((L, gw), lambda b, g, c: (row(b, g, c), g)),
        pl.BlockSpec((L, N), lambda b, g, c: (row(b, g, c), D_INNER // N + g)),
        pl.BlockSpec((L, N), lambda b, g, c: (row(b, g, c), (D_INNER + SSM_BC) // N + g)),
        pl.BlockSpec((L, gw), lambda b, g, c: (row(b, g, c), g)),
        pl.BlockSpec((1, 1, L, R), lambda b, g, c: (b, g, c, 0)),
        pl.BlockSpec((1, R, L), lambda b, g, c: (b, g, c)),
        pl.BlockSpec((1, 1, R), lambda b, g, c: (g, 0, 0)),
        pl.BlockSpec((1, R, 1), lambda b, g, c: (g, 0, 0)),
        pl.BlockSpec((1, 1, R), lambda b, g, c: (g, 0, 0)),
        pl.BlockSpec((1, R, 1), lambda b, g, c: (g, 0, 0)),
        pl.BlockSpec((1, gw), lambda b, g, c: (0, g)),
        pl.BlockSpec((1, gw), lambda b, g, c: (0, g)),
    ]
    return pl.pallas_call(
        _ssd_kernel,
        grid=(B, G, nc),
        in_specs=in_specs,
        out_specs=pl.BlockSpec((L, gw), lambda b, g, c: (row(b, g, c), g)),
        out_shape=jax.ShapeDtypeStruct((T, D_INNER), BF16),
        scratch_shapes=[pltpu.VMEM((gw, N), F32)],
        compiler_params=_cparams(("parallel", "parallel", "arbitrary")),
        name="ssd",
    )(xbc, xbc, xbc, zs, dtr_g, small_t, dtb, dtbT, alog, alogT, dsk, nw)


def _compress_kernel(k_ref, v_ref, pek_ref, pev_ref, wk1_ref, wv1_ref, wk2_ref, wv2_ref, kc_ref, vct_ref, xf):
    d = ATTN_HEAD_DIM
    nsub = xf.shape[0] // CMP_STRIDE

    def mlp(src_ref, pe_ref, w1_ref, w2_ref):
        xf[...] = src_ref[...].astype(F32)
        u = jnp.zeros((nsub, d), F32)
        v = jnp.zeros((nsub, d), F32)
        for l in range(CMP_STRIDE):
            tok = xf[pl.ds(l, nsub, stride=CMP_STRIDE), :]
            l2 = CMP_STRIDE + l
            u = u + jnp.dot((tok + pe_ref[l:l + 1, :]).astype(BF16), w1_ref[l * d:(l + 1) * d, :],
                            preferred_element_type=F32)
            v = v + jnp.dot((tok + pe_ref[l2:l2 + 1, :]).astype(BF16), w1_ref[l2 * d:(l2 + 1) * d, :],
                            preferred_element_type=F32)
        pre = u + pltpu.roll(v, nsub - 1, 0)
        return jnp.dot(_silu(pre).astype(BF16), w2_ref[...], preferred_element_type=F32)

    kc_ref[0, 0] = mlp(k_ref, pek_ref, wk1_ref, wk2_ref).astype(kc_ref.dtype)
    vct_ref[0, 0] = mlp(v_ref, pev_ref, wv1_ref, wv2_ref).T.astype(vct_ref.dtype)


def _compress(qkv, pe_k, pe_v, wk1, wv1, wk2, wv2, B, S):
    G, d = ATTN_KV_GROUPS, ATTN_HEAD_DIM
    nsub = S // CMP_STRIDE
    full = lambda shape: pl.BlockSpec(shape, lambda b, g: (0, 0))
    out = pl.BlockSpec((1, 1, nsub, d), lambda b, g: (b, g, 0, 0))
    return pl.pallas_call(
        _compress_kernel,
        grid=(B, G),
        in_specs=[pl.BlockSpec((S, d), lambda b, g: (b, QKV_KC + g)),
                  pl.BlockSpec((S, d), lambda b, g: (b, QKV_VC + g)),
                  full((CMP_BLOCK, d)), full((CMP_BLOCK, d)),
                  full((CMP_BLOCK * d, d)), full((CMP_BLOCK * d, d)),
                  full((d, d)), full((d, d))],
        out_specs=[out, out],
        out_shape=[jax.ShapeDtypeStruct((B, G, nsub, d), BF16),
                   jax.ShapeDtypeStruct((B, G, d, nsub), BF16)],
        scratch_shapes=[pltpu.VMEM((S, d), F32)],
        compiler_params=_cparams(("parallel", "parallel")),
        name="compress",
    )(qkv, qkv, pe_k, pe_v, wk1.astype(BF16), wv1.astype(BF16), wk2.astype(BF16), wv2.astype(BF16))


def _nsa_kernel(q_ref, kc_ref, vct_ref, ks_ref, vs_ref, kw_ref, vw_ref, gt_ref, o_ref,
                vst, vwt, qt, m_scr, l_scr, acc_scr, out_scr, *, tq, tks, n_cmp, n_blk):
    i = pl.program_id(2)
    d = ATTN_HEAD_DIM
    R = ATTN_REP
    heads = [slice(r * tq, (r + 1) * tq) for r in range(R)]

    def transposed(v):
        return v.astype(F32).T.astype(BF16)

    @pl.when(i == 0)
    def _():
        for c in range(vst.shape[0]):
            vst[c] = transposed(vs_ref[c * tks:(c + 1) * tks, :])
        vwt[...] = transposed(vw_ref[...])

    for r in range(R):
        qt[:, heads[r]] = transposed(q_ref[:, r * d:(r + 1) * d])
    gate = jax.nn.sigmoid(gt_ref[0])

    nk = kc_ref.shape[2]
    kc = kc_ref[0, 0]
    vct = vct_ref[0, 0]
    n_idx = lax.broadcasted_iota(jnp.int32, (nk, tq), 0)
    t_cmp = i * tq + lax.broadcasted_iota(jnp.int32, (nk, tq), 1)
    cmask = (n_idx * CMP_STRIDE + CMP_BLOCK - 1 <= t_cmp) & (n_idx < n_cmp)
    psum = jnp.zeros((nk, tq), F32)
    for r in range(R):
        s = jnp.where(cmask, jnp.dot(kc, qt[:, heads[r]], preferred_element_type=F32), NEG_INF)
        e = jnp.exp2(s - jnp.max(s, axis=0, keepdims=True))
        p = jnp.where(cmask, e * (1.0 / jnp.sum(e, axis=0, keepdims=True)), 0.0)
        out_scr[:, heads[r]] = gate[r:r + 1, :] * jnp.dot(vct, p.astype(BF16), preferred_element_type=F32)
        psum = psum + p
    bi = lax.broadcasted_iota(jnp.int32, (n_blk, nk), 0)
    ni = lax.broadcasted_iota(jnp.int32, (n_blk, nk), 1)
    ovl = ((ni * CMP_STRIDE < bi * SEL_BLOCK + SEL_BLOCK) & (ni * CMP_STRIDE + CMP_BLOCK > bi * SEL_BLOCK)
           & (ni < n_cmp))
    imp = jnp.dot(jnp.where(ovl, 1.0, 0.0), psum, preferred_element_type=F32, precision=lax.Precision.HIGHEST)
    blk = lax.broadcasted_iota(jnp.int32, (n_blk, tq), 0)
    cur = (i * tq + lax.broadcasted_iota(jnp.int32, (n_blk, tq), 1)) // SEL_BLOCK
    forced = (blk == 0) | (blk == cur) | (blk == cur - 1)
    score = jnp.where(forced, FORCE_SCORE, jnp.where(blk <= cur, imp, NEG_INF))
    rank = jnp.zeros((n_blk, tq), F32)
    for k in range(n_blk):
        sk = score[k:k + 1, :]
        rank = rank + jnp.where((sk > score) | ((sk == score) & (k < blk)), 1.0, 0.0)
    sel = jnp.where(rank < N_SEL, 1.0, 0.0).astype(BF16)

    m_scr[...] = jnp.full(m_scr.shape, NEG_INF, F32)
    l_scr[...] = jnp.zeros_like(l_scr)
    acc_scr[...] = jnp.zeros_like(acc_scr)

    def sel_body(c, carry):
        k0 = pl.multiple_of(c * tks, tks)
        key = k0 + lax.broadcasted_iota(jnp.int32, (tks, tq), 0)
        t = i * tq + lax.broadcasted_iota(jnp.int32, (tks, tq), 1)
        in_blk = (lax.broadcasted_iota(jnp.int32, (tks, n_blk), 1)
                  == (k0 + lax.broadcasted_iota(jnp.int32, (tks, n_blk), 0)) // SEL_BLOCK)
        picked = jnp.dot(jnp.where(in_blk, 1.0, 0.0).astype(BF16), sel, preferred_element_type=F32)
        bias = jnp.where((picked > 0.5) & (key <= t), 0.0, NEG_INF)
        k_chunk = ks_ref[pl.ds(k0, tks), :]
        vt_chunk = vst[c]
        for r in range(R):
            s = jnp.dot(k_chunk, qt[:, heads[r]], preferred_element_type=F32) + bias
            m_prev = m_scr[r:r + 1, :]
            m_new = jnp.maximum(m_prev, jnp.max(s, axis=0, keepdims=True))
            alpha = jnp.exp2(m_prev - m_new)
            p = jnp.exp2(s - m_new)
            l_scr[r:r + 1, :] = alpha * l_scr[r:r + 1, :] + jnp.sum(p, axis=0, keepdims=True)
            acc_scr[:, heads[r]] = (alpha * acc_scr[:, heads[r]]
                                    + jnp.dot(vt_chunk, p.astype(BF16), preferred_element_type=F32))
            m_scr[r:r + 1, :] = m_new
        return carry

    lax.fori_loop(0, (i * tq + tq - 1) // tks + 1, sel_body, 0)
    for r in range(R):
        o = acc_scr[:, heads[r]] * (1.0 / l_scr[r:r + 1, :])
        out_scr[:, heads[r]] += gate[R + r:R + r + 1, :] * o

    span = WINDOW + tq
    k0 = pl.multiple_of(jnp.maximum(i * tq - WINDOW, 0), tq)
    dist = (i * tq + lax.broadcasted_iota(jnp.int32, (span, tq), 1)
            - (k0 + lax.broadcasted_iota(jnp.int32, (span, tq), 0)))
    bias = jnp.where((dist >= 0) & (dist < WINDOW), 0.0, NEG_INF)
    k_win = kw_ref[pl.ds(k0, span), :]
    vt_win = vwt[:, pl.ds(k0, span)]
    for r in range(R):
        s = jnp.dot(k_win, qt[:, heads[r]], preferred_element_type=F32) + bias
        p = jnp.exp2(s - jnp.max(s, axis=0, keepdims=True))
        o = jnp.dot(vt_win, p.astype(BF16), preferred_element_type=F32) * (1.0 / jnp.sum(p, axis=0, keepdims=True))
        out_scr[:, heads[r]] += gate[2 * R + r:2 * R + r + 1, :] * o

    for r in range(R):
        o_ref[:, r * d:(r + 1) * d] = out_scr[:, heads[r]].T.astype(o_ref.dtype)


def _nsa(qkv, kc, vct, small_t, B, S, tq=256, tks=512):
    assert WINDOW % tq == 0 and S >= WINDOW + tq
    T = B * S
    G, d, R = ATTN_KV_GROUPS, ATTN_HEAD_DIM, ATTN_REP
    nq = S // tq
    nk = kc.shape[2]
    n_cmp = S // CMP_STRIDE - CMP_BLOCK // CMP_STRIDE + 1
    n_blk = S // SEL_BLOCK
    seq = lambda c0: pl.BlockSpec((S, d), lambda b, g, i: (b, c0 + g))
    cmp_spec = pl.BlockSpec((1, 1, nk, d), lambda b, g, i: (b, g, 0, 0))
    gate0 = SSM_HEADS // GATE_ROWS
    return pl.pallas_call(
        functools.partial(_nsa_kernel, tq=tq, tks=tks, n_cmp=n_cmp, n_blk=n_blk),
        grid=(B, G, nq),
        in_specs=[pl.BlockSpec((tq, KV_WIDTH), lambda b, g, i: (b * nq + i, g)),
                  cmp_spec, cmp_spec,
                  seq(QKV_KS), seq(QKV_VS), seq(QKV_KW), seq(QKV_VW),
                  pl.BlockSpec((1, GATE_ROWS, tq), lambda b, g, i: (b, gate0 + g, i))],
        out_specs=pl.BlockSpec((tq, KV_WIDTH), lambda b, g, i: (b * nq + i, g)),
        out_shape=jax.ShapeDtypeStruct((T, ATTN_WIDTH), BF16),
        scratch_shapes=[pltpu.VMEM((S // tks, d, tks), BF16),
                        pltpu.VMEM((d, S), BF16),
                        pltpu.VMEM((d, R * tq), BF16),
                        pltpu.VMEM((8, tq), F32),
                        pltpu.VMEM((8, tq), F32),
                        pltpu.VMEM((d, R * tq), F32),
                        pltpu.VMEM((d, R * tq), F32)],
        compiler_params=_cparams(("parallel", "parallel", "arbitrary")),
        name="nsa",
    )(qkv, kc, vct, qkv, qkv, qkv, qkv, small_t)


def _merge_kernel(ys_ref, ws_ref, ya_ref, wa_ref, gs_ref, ga_ref, o_ref, wsb, wab):
    _cast_per_column_tile(((ws_ref, wsb), (wa_ref, wab)))
    s = jnp.dot(ys_ref[...], wsb[...], preferred_element_type=F32)
    a = jnp.dot(ya_ref[...], wab[...], preferred_element_type=F32)
    o_ref[...] = (gs_ref[...].astype(F32) * s + ga_ref[...].astype(F32) * a).astype(o_ref.dtype)


def _merge(y_ssm, w_s, y_attn, w_a, gm, tm=512, tn=512):
    T = y_ssm.shape[0]
    return pl.pallas_call(
        _merge_kernel,
        grid=(D_MODEL // tn, T // tm),
        in_specs=[pl.BlockSpec((tm, D_INNER), lambda n, m: (m, 0)),
                  pl.BlockSpec((D_INNER, tn), lambda n, m: (0, n)),
                  pl.BlockSpec((tm, ATTN_WIDTH), lambda n, m: (m, 0)),
                  pl.BlockSpec((ATTN_WIDTH, tn), lambda n, m: (0, n)),
                  pl.BlockSpec((tm, tn), lambda n, m: (m, n)),
                  pl.BlockSpec((tm, tn), lambda n, m: (m, D_MODEL // tn + n))],
        out_specs=pl.BlockSpec((tm, tn), lambda n, m: (m, n)),
        out_shape=jax.ShapeDtypeStruct((T, D_MODEL), BF16),
        scratch_shapes=[pltpu.VMEM((D_INNER, tn), BF16), pltpu.VMEM((ATTN_WIDTH, tn), BF16)],
        compiler_params=_cparams(("parallel", "arbitrary")),
        name="merge",
    )(y_ssm, w_s, y_attn, w_a, gm, gm)


def _mix_out_kernel(mg_ref, w_ref, x_ref, nw_ref, x1_ref, h_ref):
    x1 = x_ref[...] + jnp.dot(mg_ref[...], w_ref[...], preferred_element_type=F32)
    x1_ref[...] = x1
    h_ref[...] = _rms(x1, nw_ref[...]).astype(h_ref.dtype)


def _mix_out(merged, w, x2d, norm_w, tm=512):
    T = x2d.shape[0]
    return pl.pallas_call(
        _mix_out_kernel,
        grid=(T // tm,),
        in_specs=[pl.BlockSpec((tm, D_MODEL), lambda m: (m, 0)),
                  _resident((D_MODEL, D_MODEL)),
                  pl.BlockSpec((tm, D_MODEL), lambda m: (m, 0)),
                  pl.BlockSpec((1, D_MODEL), lambda m: (0, 0))],
        out_specs=[pl.BlockSpec((tm, D_MODEL), lambda m: (m, 0)),
                   pl.BlockSpec((tm, D_MODEL), lambda m: (m, 0))],
        out_shape=[jax.ShapeDtypeStruct((T, D_MODEL), F32),
                   jax.ShapeDtypeStruct((T, D_MODEL), BF16)],
        compiler_params=_cparams(("parallel",)),
        name="mix_out",
    )(merged, w, x2d, norm_w)


def _ffn_up_kernel(h_ref, wg_ref, wu_ref, cw_ref, cb_ref, o_ref, gbuf, wgb, wub, *, tiles_per_seq):
    m = pl.program_id(1)
    _cast_per_column_tile(((wg_ref, wgb), (wu_ref, wub)))

    @pl.when(m % tiles_per_seq == 0)
    def _():
        gbuf[0:TAIL, :] = jnp.zeros((TAIL, gbuf.shape[1]), F32)

    h = h_ref[...]
    g = jnp.dot(h, wgb[...], preferred_element_type=F32)
    u = jnp.dot(h, wub[...], preferred_element_type=F32)
    y, y_top = _causal_conv(g, gbuf, cw_ref, cb_ref)
    o_ref[...] = (_silu(y) * u).astype(o_ref.dtype)
    o_ref[0:FIX, :] = (_silu(y_top) * u[0:FIX, :]).astype(o_ref.dtype)


def _ffn_up(h, wg, wu, conv_w, conv_b, S, tm=1024, tn=512):
    T = h.shape[0]
    return pl.pallas_call(
        functools.partial(_ffn_up_kernel, tiles_per_seq=S // tm),
        grid=(D_FF // tn, T // tm),
        in_specs=[pl.BlockSpec((tm, D_MODEL), lambda n, m: (m, 0)),
                  pl.BlockSpec((D_MODEL, tn), lambda n, m: (0, n)),
                  pl.BlockSpec((D_MODEL, tn), lambda n, m: (0, n)),
                  pl.BlockSpec((FFN_CONV, tn), lambda n, m: (0, n)),
                  pl.BlockSpec((1, tn), lambda n, m: (0, n))],
        out_specs=pl.BlockSpec((tm, tn), lambda n, m: (m, n)),
        out_shape=jax.ShapeDtypeStruct((T, D_FF), BF16),
        scratch_shapes=[pltpu.VMEM((TAIL + FIX, tn), F32),
                        pltpu.VMEM((D_MODEL, tn), BF16), pltpu.VMEM((D_MODEL, tn), BF16)],
        compiler_params=_cparams(("parallel", "arbitrary")),
        name="ffn_up",
    )(h, wg, wu, conv_w, conv_b.reshape(1, D_FF))


def _ffn_down_kernel(a_ref, w_ref, x_ref, nw_ref, x2_ref, h_ref):
    x2 = x_ref[...] + jnp.dot(a_ref[...], w_ref[...], preferred_element_type=F32)
    x2_ref[...] = x2
    h_ref[...] = _rms(x2, nw_ref[...]).astype(h_ref.dtype)


def _ffn_down(act, w, x1, norm_w, tm=256):
    T = x1.shape[0]
    return pl.pallas_call(
        _ffn_down_kernel,
        grid=(T // tm,),
        in_specs=[pl.BlockSpec((tm, D_FF), lambda m: (m, 0)),
                  _resident((D_FF, D_MODEL)),
                  pl.BlockSpec((tm, D_MODEL), lambda m: (m, 0)),
                  _resident((1, D_MODEL))],
        out_specs=[pl.BlockSpec((tm, D_MODEL), lambda m: (m, 0)),
                   pl.BlockSpec((tm, D_MODEL), lambda m: (m, 0))],
        out_shape=[jax.ShapeDtypeStruct((T, D_MODEL), F32),
                   jax.ShapeDtypeStruct((T, D_MODEL), BF16)],
        compiler_params=_cparams(("parallel",)),
        name="ffn_down",
    )(act, w, x1, norm_w)


def _ple_kernel(h_ref, wg_ref, p_ref, wp_ref, x_ref, nw_ref, o_ref):
    gate = jax.nn.sigmoid(jnp.dot(h_ref[...], wg_ref[...], preferred_element_type=F32))
    emb = jnp.dot(p_ref[...].astype(BF16), wp_ref[...], preferred_element_type=F32)
    o_ref[...] = _rms(x_ref[...] + gate * emb, nw_ref[...])


def _ple(h, wg, p2d, wp, x2, norm_w, tm=512):
    T = x2.shape[0]
    return pl.pallas_call(
        _ple_kernel,
        grid=(T // tm,),
        in_specs=[pl.BlockSpec((tm, D_MODEL), lambda m: (m, 0)),
                  _resident((D_MODEL, D_MODEL)),
                  pl.BlockSpec((tm, PLE_DIM), lambda m: (m, 0)),
                  _resident((PLE_DIM, D_MODEL)),
                  pl.BlockSpec((tm, D_MODEL), lambda m: (m, 0)),
                  pl.BlockSpec((1, D_MODEL), lambda m: (0, 0))],
        out_specs=pl.BlockSpec((tm, D_MODEL), lambda m: (m, 0)),
        out_shape=jax.ShapeDtypeStruct((T, D_MODEL), F32),
        compiler_params=_cparams(("parallel",)),
        name="ple",
    )(h, wg, p2d, wp, x2, norm_w)


def _mixer_inputs(x2d, norm_w, w, ssm_conv_w, ssm_conv_b, B, S):
    wt = w.T
    h, small, small_t = _norm_small(x2d, norm_w.reshape(1, D_MODEL), _small_weight(wt), B, S)
    zs = _proj("silu", h, wt, 0, D_INNER, S)
    xbc = _proj("conv", h, wt, D_INNER, SSM_CONV_DIM, S,
                extra=(ssm_conv_w, ssm_conv_b.reshape(1, SSM_CONV_DIM)))
    qkv = _proj("rope", h, wt, NAT_Q, QKV_WIDTH, S, extra=_rope_tables(S))
    gm = _proj("sigmoid", h, wt, NAT_GM, 2 * D_MODEL, S)
    return small, small_t, zs, xbc, qkv, gm


def _layer(x2d, p2d, B, S, norm_mix_w, w_in, ssm_conv_w, ssm_conv_b, ssm_dt_bias, ssm_a_log, ssm_d,
           ssm_norm_w, cmp_pe_k, cmp_pe_v, cmp_wk1, cmp_wk2, cmp_wv1, cmp_wv2,
           w_ssm_branch, w_attn_branch, w_mix_out, norm_ffn_w, ffn_w_gate, ffn_w_up,
           ffn_conv_w, ffn_conv_b, ffn_w_down, ple_norm_w, ple_w_gate, ple_w_proj, out_norm_w):
    small, small_t, zs, xbc, qkv, gm = _mixer_inputs(x2d, norm_mix_w, w_in, ssm_conv_w, ssm_conv_b, B, S)
    y_ssm = _ssd(xbc, zs, small, small_t, ssm_dt_bias, ssm_a_log, ssm_d, ssm_norm_w, B, S)
    kc, vct = _compress(qkv, cmp_pe_k, cmp_pe_v, cmp_wk1, cmp_wv1, cmp_wk2, cmp_wv2, B, S)
    y_attn = _nsa(qkv, kc, vct, small_t, B, S)

    merged = _merge(y_ssm, w_ssm_branch, y_attn, w_attn_branch, gm)
    x1, h2 = _mix_out(merged, w_mix_out.astype(BF16), x2d, norm_ffn_w.reshape(1, D_MODEL))

    act = _ffn_up(h2, ffn_w_gate, ffn_w_up, ffn_conv_w, ffn_conv_b, S)
    x2, h3 = _ffn_down(act, ffn_w_down.astype(BF16), x1, ple_norm_w.reshape(1, D_MODEL))
    return _ple(h3, ple_w_gate.astype(BF16), p2d, ple_w_proj.astype(BF16), x2, out_norm_w.reshape(1, D_MODEL))


def kernel(x, p, norm_mix_w, w_in, ssm_conv_w, ssm_conv_b, ssm_dt_bias, ssm_a_log, ssm_d, ssm_norm_w, cmp_pe_k, cmp_pe_v, cmp_wk1, cmp_wk2, cmp_wv1, cmp_wv2, w_ssm_branch, w_attn_branch, w_mix_out, norm_ffn_w, ffn_w_gate, ffn_w_up, ffn_conv_w, ffn_conv_b, ffn_w_down, ple_norm_w, ple_w_gate, ple_w_proj, final_norm_w):
    B, S, D = x.shape
    depth = w_in.shape[0]
    assert depth == 1, "the final norm is fused into the (single) layer's last kernel"
    x2d = x.reshape(B * S, D)
    out = _layer(x2d, p[0].reshape(B * S, PLE_DIM), B, S, norm_mix_w[0], w_in[0], ssm_conv_w[0], ssm_conv_b[0],
                 ssm_dt_bias[0], ssm_a_log[0], ssm_d[0], ssm_norm_w[0], cmp_pe_k[0], cmp_pe_v[0],
                 cmp_wk1[0], cmp_wk2[0], cmp_wv1[0], cmp_wv2[0], w_ssm_branch[0], w_attn_branch[0],
                 w_mix_out[0], norm_ffn_w[0], ffn_w_gate[0], ffn_w_up[0], ffn_conv_w[0], ffn_conv_b[0],
                 ffn_w_down[0], ple_norm_w[0], ple_w_gate[0], ple_w_proj[0], final_norm_w)
    return out.reshape(B, S, D)
```

```python
import functools
import math

import jax
import jax.numpy as jnp
from jax import lax
from jax.experimental import pallas as pl
from jax.experimental.pallas import tpu as pltpu

F32 = jnp.float32
BF16 = jnp.bfloat16

D_MODEL = 2048
PLE_DIM = 256
EPS = 1e-6
D_INNER = 4096
SSM_HEAD_DIM = 64
SSM_HEADS = 64
SSM_GROUPS = 8
SSM_STATE = 128
SSM_CONV = 4
SSM_CHUNK = 128
SSM_BC = SSM_GROUPS * SSM_STATE
SSM_CONV_DIM = D_INNER + 2 * SSM_BC
ATTN_HEADS = 16
ATTN_HEAD_DIM = 128
ATTN_KV_GROUPS = 4
ATTN_REP = ATTN_HEADS // ATTN_KV_GROUPS
ATTN_WIDTH = ATTN_HEADS * ATTN_HEAD_DIM
KV_WIDTH = ATTN_KV_GROUPS * ATTN_HEAD_DIM
CMP_BLOCK = 32
CMP_STRIDE = 16
SEL_BLOCK = 64
N_SEL = 16
WINDOW = 512
ROPE_THETA = 500000.0
ROPE_DIM = ATTN_HEAD_DIM // 4
D_FF = 5632
FFN_CONV = 3
NEG_INF = -1e30
FORCE_SCORE = 1e4

LANES = 128
TAIL = 8
FIX = 16
VMEM_LIMIT = 56 * 1024 * 1024

NAT_DT = D_INNER + SSM_CONV_DIM
NAT_Q = NAT_DT + SSM_HEADS
NAT_GN = NAT_Q + ATTN_WIDTH + 6 * KV_WIDTH
NAT_GM = NAT_GN + 3 * ATTN_HEADS
QKV_WIDTH = ATTN_WIDTH + 6 * KV_WIDTH
WTILE = 1024
N_SMALL = 128
GATE_ROWS = 16
ONES_ROWS = 16
QKV_KC, QKV_VC, QKV_KS, QKV_VS, QKV_KW, QKV_VW = (
    (ATTN_WIDTH + i * KV_WIDTH) // ATTN_HEAD_DIM for i in range(6))


def _cparams(sem):
    return pltpu.CompilerParams(dimension_semantics=sem, vmem_limit_bytes=VMEM_LIMIT)


def _resident(shape):
    return pl.BlockSpec(shape, lambda *_: (0,) * len(shape), pipeline_mode=pl.Buffered(1))


def _rms(xf, w):
    return xf * lax.rsqrt(jnp.mean(xf * xf, axis=-1, keepdims=True) + EPS) * w


def _silu(v):
    return v * jax.nn.sigmoid(v)


def _causal_conv(x, top, w_ref, bias_ref):
    taps = w_ref.shape[0]
    bias = bias_ref[...]
    y = bias + w_ref[taps - 1:taps, :] * x
    for s in range(1, taps):
        y = y + w_ref[taps - 1 - s:taps - s, :] * pltpu.roll(x, s, 0)
    top[TAIL:TAIL + FIX, :] = x[0:FIX, :]
    y_top = bias
    for k in range(taps):
        off = TAIL - (taps - 1) + k
        y_top = y_top + w_ref[k:k + 1, :] * top[off:off + FIX, :]
    top[0:TAIL, :] = x[x.shape[0] - TAIL:, :]
    return y, y_top


def _cast_per_column_tile(pairs):
    @pl.when(pl.program_id(1) == 0)
    def _():
        for src, dst in pairs:
            dst[...] = src[...].astype(BF16)


def _wide_dot(h_ref, wt_ref, wb):
    _cast_per_column_tile(((wt_ref, wb),))
    return lax.dot_general(h_ref[...], wb[...], (((1,), (1,)), ((), ())), preferred_element_type=F32)


def _small_weight(wt):
    G, R = ATTN_KV_GROUPS, ATTN_REP
    K = wt.shape[1]
    w_dt = wt[NAT_DT:NAT_DT + SSM_HEADS]
    w_g = wt[NAT_GN:NAT_GN + 3 * ATTN_HEADS].reshape(3, G, R, K).transpose(1, 0, 2, 3).reshape(G, 3 * R, K)
    w_g = jnp.pad(w_g, ((0, 0), (0, GATE_ROWS - 3 * R), (0, 0))).reshape(G * GATE_ROWS, K)
    return jnp.concatenate([w_dt, w_g], axis=0)


def _norm_small_kernel(x_ref, nw_ref, ws_ref, h_ref, small_ref, smallt_ref):
    h = _rms(x_ref[...], nw_ref[...])
    h_ref[...] = h.astype(BF16)
    small = lax.dot_general(h, ws_ref[...], (((1,), (1,)), ((), ())), preferred_element_type=F32,
                            precision=lax.Precision.HIGHEST)
    small_ref[...] = small
    smallt_ref[0] = small.T


def _norm_small(x2d, norm_w, w_small, B, S, tm=512):
    T = x2d.shape[0]
    per = S // tm
    return pl.pallas_call(
        _norm_small_kernel,
        grid=(T // tm,),
        in_specs=[pl.BlockSpec((tm, D_MODEL), lambda i: (i, 0)),
                  pl.BlockSpec((1, D_MODEL), lambda i: (0, 0)),
                  pl.BlockSpec((N_SMALL, D_MODEL), lambda i: (0, 0))],
        out_specs=[pl.BlockSpec((tm, D_MODEL), lambda i: (i, 0)),
                   pl.BlockSpec((tm, N_SMALL), lambda i: (i, 0)),
                   pl.BlockSpec((1, N_SMALL, tm), lambda i: (i // per, 0, i % per))],
        out_shape=[jax.ShapeDtypeStruct((T, D_MODEL), BF16),
                   jax.ShapeDtypeStruct((T, N_SMALL), F32),
                   jax.ShapeDtypeStruct((B, N_SMALL, S), F32)],
        compiler_params=_cparams(("parallel",)),
        name="norm_small",
    )(x2d, norm_w, w_small)


def _proj_act_kernel(h_ref, wt_ref, o_ref, wb, *, act):
    o_ref[...] = act(_wide_dot(h_ref, wt_ref, wb)).astype(o_ref.dtype)


def _proj_conv_kernel(h_ref, wt_ref, cw_ref, cb_ref, o_ref, wb, buf, *, tiles_per_seq):
    m = pl.program_id(1)

    @pl.when(m % tiles_per_seq == 0)
    def _():
        buf[0:TAIL, :] = jnp.zeros((TAIL, buf.shape[1]), F32)

    y, y_top = _causal_conv(_wide_dot(h_ref, wt_ref, wb), buf, cw_ref, cb_ref)
    o_ref[...] = _silu(y).astype(o_ref.dtype)
    o_ref[0:FIX, :] = _silu(y_top).astype(o_ref.dtype)


def _proj_rope_kernel(h_ref, wt_ref, ca_ref, cb_ref, cc_ref, o_ref, wb, *, scale):
    n = pl.program_id(0)
    d = ATTN_HEAD_DIM
    half = ROPE_DIM // 2
    is_q = n < ATTN_WIDTH // WTILE
    acc = _wide_dot(h_ref, wt_ref, wb)
    ca, cb, cc = ca_ref[...], cb_ref[...], cc_ref[...]
    q_scale = jnp.where(is_q, scale, 1.0).astype(F32)
    for hd in range(WTILE // d):
        xh = acc[:, hd * d:(hd + 1) * d]
        up = pltpu.roll(xh, d - half, 1)
        dn = pltpu.roll(xh, half, 1)
        roped = (xh * ca + up * cb + dn * cc) * q_scale
        if hd >= KV_WIDTH // d:
            roped = jnp.where(is_q, roped, xh)
        o_ref[:, hd * d:(hd + 1) * d] = roped.astype(o_ref.dtype)


def _rope_tables(S):
    half = ROPE_DIM // 2
    pos = jnp.arange(S)
    inv = ROPE_THETA ** (-(jnp.arange(half, dtype=F32) * 2.0 / ROPE_DIM))
    ang = pos.astype(F32)[:, None] * inv[None, :]
    cos, sin = jnp.cos(ang), jnp.sin(ang)
    rest = ATTN_HEAD_DIM - ROPE_DIM
    ca = jnp.concatenate([cos, cos, jnp.ones((S, rest), F32)], axis=1)
    cb = jnp.concatenate([-sin, jnp.zeros((S, ATTN_HEAD_DIM - half), F32)], axis=1)
    cc = jnp.concatenate([jnp.zeros((S, half), F32), sin, jnp.zeros((S, rest), F32)], axis=1)
    return ca, cb, cc


def _proj(kind, h, wt, row0, width, S, extra=(), tm=1024):
    T, K = h.shape
    tn = WTILE
    per = S // tm
    in_specs = [pl.BlockSpec((tm, K), lambda n, m: (m, 0)),
                pl.BlockSpec((pl.Element(tn), pl.Element(K)),
                             lambda n, m: (pl.multiple_of(row0 + n * tn, TAIL), 0))]
    scratch = [pltpu.VMEM((tn, K), BF16)]
    if kind == "silu":
        body = functools.partial(_proj_act_kernel, act=_silu)
    elif kind == "sigmoid":
        body = functools.partial(_proj_act_kernel, act=jax.nn.sigmoid)
    elif kind == "conv":
        body = functools.partial(_proj_conv_kernel, tiles_per_seq=per)
        taps = extra[0].shape[0]
        in_specs += [pl.BlockSpec((taps, tn), lambda n, m: (0, n)),
                     pl.BlockSpec((1, tn), lambda n, m: (0, n))]
        scratch += [pltpu.VMEM((TAIL + FIX, tn), F32)]
    else:
        body = functools.partial(_proj_rope_kernel, scale=math.log2(math.e) / math.sqrt(ATTN_HEAD_DIM))
        tab = pl.BlockSpec((tm, ATTN_HEAD_DIM), lambda n, m: (m % per, 0))
        in_specs += [tab, tab, tab]
    return pl.pallas_call(
        body,
        grid=(width // tn, T // tm),
        in_specs=in_specs,
        out_specs=pl.BlockSpec((tm, tn), lambda n, m: (m, n)),
        out_shape=jax.ShapeDtypeStruct((T, width), BF16),
        scratch_shapes=scratch,
        compiler_params=_cparams(("parallel", "arbitrary")),
        name="proj_" + kind,
    )(h, wt, *extra)


def _ssd_scalars_kernel(raw_ref, rawt_ref, dtb_ref, dtbt_ref, alog_ref, alogt_ref, acum_ref, dtt_ref, acumt_ref):
    L = SSM_CHUNK
    H = SSM_HEADS
    ri = lax.broadcasted_iota(jnp.int32, (L, L), 0)
    ci = lax.broadcasted_iota(jnp.int32, (L, L), 1)
    hp = lax.Precision.HIGHEST
    dt = jax.nn.softplus(raw_ref[:, 0:H] + dtb_ref[...])
    dtt = jax.nn.softplus(rawt_ref[0, 0:H, :] + dtbt_ref[...])
    a = -jnp.exp(alog_ref[...])
    at = -jnp.exp(alogt_ref[...])
    acum_ref[...] = jnp.dot(jnp.where(ci <= ri, 1.0, 0.0), dt * a, preferred_element_type=F32, precision=hp)
    dtt_ref[0] = dtt
    acumt_ref[0] = jnp.dot(dtt * at, jnp.where(ri <= ci, 1.0, 0.0), preferred_element_type=F32, precision=hp)


def _ssd_scalars(small, small_t, dt_bias, a_log, B, S):
    T = B * S
    L, H = SSM_CHUNK, SSM_HEADS
    nc = S // L
    head_major = pl.BlockSpec((1, H, L), lambda b, c: (b, 0, c))
    return pl.pallas_call(
        _ssd_scalars_kernel,
        grid=(B, nc),
        in_specs=[pl.BlockSpec((L, N_SMALL), lambda b, c: (b * nc + c, 0)),
                  pl.BlockSpec((1, N_SMALL, L), lambda b, c: (b, 0, c)),
                  pl.BlockSpec((1, H), lambda b, c: (0, 0)),
                  pl.BlockSpec((H, 1), lambda b, c: (0, 0)),
                  pl.BlockSpec((1, H), lambda b, c: (0, 0)),
                  pl.BlockSpec((H, 1), lambda b, c: (0, 0))],
        out_specs=[pl.BlockSpec((L, H), lambda b, c: (b * nc + c, 0)), head_major, head_major],
        out_shape=[jax.ShapeDtypeStruct((T, H), F32),
                   jax.ShapeDtypeStruct((B, H, S), F32),
                   jax.ShapeDtypeStruct((B, H, S), F32)],
        compiler_params=_cparams(("parallel", "parallel")),
        name="ssd_scalars",
    )(small, small_t, dt_bias.reshape(1, H), dt_bias.reshape(H, 1), a_log.reshape(1, H), a_log.reshape(H, 1))


def _ssd_kernel(x_ref, b_ref, c_ref, z_ref, acum_ref, acumt_ref, dtt_ref, dsk_ref, nw_ref, y_ref, h_scr):
    L = SSM_CHUNK
    P = SSM_HEAD_DIM
    c = pl.program_id(2)

    @pl.when(c == 0)
    def _():
        h_scr[...] = jnp.zeros_like(h_scr)

    xb = x_ref[...]
    bmb = b_ref[...]
    cmb = c_ref[...]
    n_heads = SSM_HEADS // SSM_GROUPS
    acum = acum_ref[0, 0]
    acumT = acumt_ref[0]
    dtT = dtt_ref[0]
    ri = lax.broadcasted_iota(jnp.int32, (L, L), 0)
    ci = lax.broadcasted_iota(jnp.int32, (L, L), 1)
    tril = ci <= ri

    cb = lax.dot_general(cmb, bmb, (((1,), (1,)), ((), ())), preferred_element_type=F32)
    yoff = lax.dot_general(cmb, h_scr[...].astype(BF16), (((1,), (1,)), ((), ())),
                           preferred_element_type=F32)

    lane = lax.broadcasted_iota(jnp.int32, (L, 2 * P), 1)
    lo = lane < P

    def pair(v0, v1):
        return jnp.where(lo, v0, v1)

    dsk = dsk_ref[...]
    ys = []
    for pr in range(n_heads // 2):
        sl = slice(pr * 2 * P, (pr + 1) * 2 * P)
        xp = xb[:, sl]
        yd, ea = [], []
        for r in (2 * pr, 2 * pr + 1):
            acol = jnp.broadcast_to(acum[:, r:r + 1], (L, L))
            dec = jnp.exp(jnp.where(tril, acol - acumT[r:r + 1, :], -jnp.inf))
            yd.append(jnp.dot((cb * dec * dtT[r:r + 1, :]).astype(BF16), xp, preferred_element_type=F32))
            ea.append(jnp.exp(acol))
        ys.append(pair(yd[0], yd[1]) + yoff[:, sl] * pair(ea[0], ea[1]) + dsk[:, sl] * xp.astype(F32))

    a_last = acumT[:, L - 1:L]
    wT = dtT * jnp.exp(a_last - acumT)
    cdT = jnp.exp(a_last)
    xT = xb.astype(F32).T
    xds = jnp.concatenate([xT[r * P:(r + 1) * P, :] * wT[r:r + 1, :] for r in range(n_heads)], axis=0)
    cd = jnp.concatenate([jnp.broadcast_to(cdT[r:r + 1, :], (P, h_scr.shape[1])) for r in range(n_heads)], axis=0)
    st = jnp.dot(xds.astype(BF16), bmb, preferred_element_type=F32)
    h_scr[...] = h_scr[...] * cd + st

    y = jnp.concatenate(ys, axis=1) * z_ref[...].astype(F32)
    y_ref[...] = _rms(y, nw_ref[...]).astype(y_ref.dtype)


def _ssd(xbc, zs, small, small_t, dt_bias, a_log, d_skip, norm_w, B, S):
    T = B * S
    L, G, N = SSM_CHUNK, SSM_GROUPS, SSM_STATE
    R = SSM_HEADS // G
    gw = D_INNER // G
    nc = S // L
    acum, dt_t, acum_t = _ssd_scalars(small, small_t, dt_bias, a_log, B, S)
    acum_g = acum.reshape(B, S, G, R).transpose(0, 2, 1, 3)
    dsk = jnp.repeat(d_skip, SSM_HEAD_DIM).reshape(1, D_INNER)
    nw = norm_w.reshape(1, D_INNER)

    row = lambda b, g, c: b * nc + c
    in_specs = [
        pl.BlockSpec((L, gw), lambda b, g, c: (row(b, g, c), g)),
        pl.BlockSpec((L, N), lambda b, g, c: (row(b, g, c), D_INNER // N + g)),
        pl.BlockSpec((L, N), lambda b, g, c: (row(b, g, c), (D_INNER + SSM_BC) // N + g)),
        pl.BlockSpec((L, gw), lambda b, g, c: (row(b, g, c), g)),
        pl.BlockSpec((1, 1, L, R), lambda b, g, c: (b, g, c, 0)),
        pl.BlockSpec((1, R, L), lambda b, g, c: (b, g, c)),
        pl.BlockSpec((1, R, L), lambda b, g, c: (b, g, c)),
        pl.BlockSpec((1, gw), lambda b, g, c: (0, g)),
        pl.BlockSpec((1, gw), lambda b, g, c: (0, g)),
    ]
    return pl.pallas_call(
        _ssd_kernel,
        grid=(B, G, nc),
        in_specs=in_specs,
        out_specs=pl.BlockSpec((L, gw), lambda b, g, c: (row(b, g, c), g)),
        out_shape=jax.ShapeDtypeStruct((T, D_INNER), BF16),
        scratch_shapes=[pltpu.VMEM((gw, N), F32)],
        compiler_params=_cparams(("parallel", "parallel", "arbitrary")),
        name="ssd",
    )(xbc, xbc, xbc, zs, acum_g, acum_t, dt_t, dsk, nw)


def _compress_kernel(k_ref, v_ref, pek_ref, pev_ref, wk1_ref, wv1_ref, wk2_ref, wv2_ref, kc_ref, vct_ref, xf):
    d = ATTN_HEAD_DIM
    nsub = xf.shape[0] // CMP_STRIDE

    def mlp(src_ref, pe_ref, w1_ref, w2_ref):
        xf[...] = src_ref[...].astype(F32)
        u = jnp.zeros((nsub, d), F32)
        v = jnp.zeros((nsub, d), F32)
        for l in range(CMP_STRIDE):
            tok = xf[pl.ds(l, nsub, stride=CMP_STRIDE), :]
            l2 = CMP_STRIDE + l
            u = u + jnp.dot((tok + pe_ref[l:l + 1, :]).astype(BF16), w1_ref[l * d:(l + 1) * d, :],
                            preferred_element_type=F32)
            v = v + jnp.dot((tok + pe_ref[l2:l2 + 1, :]).astype(BF16), w1_ref[l2 * d:(l2 + 1) * d, :],
                            preferred_element_type=F32)
        pre = u + pltpu.roll(v, nsub - 1, 0)
        return jnp.dot(_silu(pre).astype(BF16), w2_ref[...], preferred_element_type=F32)

    kc_ref[0, 0] = mlp(k_ref, pek_ref, wk1_ref, wk2_ref).astype(kc_ref.dtype)
    vct_ref[0, 0] = mlp(v_ref, pev_ref, wv1_ref, wv2_ref).T.astype(vct_ref.dtype)


def _compress(qkv, pe_k, pe_v, wk1, wv1, wk2, wv2, B, S):
    G, d = ATTN_KV_GROUPS, ATTN_HEAD_DIM
    nsub = S // CMP_STRIDE
    full = lambda shape: pl.BlockSpec(shape, lambda b, g: (0, 0))
    out = pl.BlockSpec((1, 1, nsub, d), lambda b, g: (b, g, 0, 0))
    return pl.pallas_call(
        _compress_kernel,
        grid=(B, G),
        in_specs=[pl.BlockSpec((S, d), lambda b, g: (b, QKV_KC + g)),
                  pl.BlockSpec((S, d), lambda b, g: (b, QKV_VC + g)),
                  full((CMP_BLOCK, d)), full((CMP_BLOCK, d)),
                  full((CMP_BLOCK * d, d)), full((CMP_BLOCK * d, d)),
                  full((d, d)), full((d, d))],
        out_specs=[out, out],
        out_shape=[jax.ShapeDtypeStruct((B, G, nsub, d), BF16),
                   jax.ShapeDtypeStruct((B, G, d, nsub), BF16)],
        scratch_shapes=[pltpu.VMEM((S, d), F32)],
        compiler_params=_cparams(("parallel", "parallel")),
        name="compress",
    )(qkv, qkv, pe_k, pe_v, wk1.astype(BF16), wv1.astype(BF16), wk2.astype(BF16), wv2.astype(BF16))


def _staged(n, scores, probs, values):
    for step in range(n + 2):
        if step < n:
            scores(step)
        if 1 <= step <= n:
            probs(step - 1)
        if step >= 2:
            values(step - 2)


def _nsa_kernel(q_ref, kc_ref, vct_ref, ks_ref, vs_ref, kw_ref, vw_ref, gt_ref, o_ref,
                vst, vwt, qt, m_scr, acc_scr, out_scr, s_scr, p_scr, *, tq, tks, n_cmp, n_blk):
    i = pl.program_id(2)
    d = ATTN_HEAD_DIM
    R = ATTN_REP
    heads = [slice(r * tq, (r + 1) * tq) for r in range(R)]

    def transposed(v):
        return v.astype(F32).T.astype(BF16)

    @pl.when(i == 0)
    def _():
        for c in range(vst.shape[0]):
            vst[c, 0:d, :] = transposed(vs_ref[c * tks:(c + 1) * tks, :])
            vst[c, d:, :] = jnp.ones((ONES_ROWS, tks), BF16)
        vwt[0:d, :] = transposed(vw_ref[...])
        vwt[d:, :] = jnp.ones((ONES_ROWS, vwt.shape[1]), BF16)

    for r in range(R):
        qt[:, heads[r]] = transposed(q_ref[:, r * d:(r + 1) * d])
    gate = jax.nn.sigmoid(gt_ref[0])

    nk = kc_ref.shape[2]
    kc = kc_ref[0, 0]
    vct = vct_ref[0, 0]
    n_idx = lax.broadcasted_iota(jnp.int32, (nk, tq), 0)
    t_cmp = i * tq + lax.broadcasted_iota(jnp.int32, (nk, tq), 1)
    cmask = (n_idx * CMP_STRIDE + CMP_BLOCK - 1 <= t_cmp) & (n_idx < n_cmp)
    psum = jnp.zeros((nk, tq), F32)
    for r in range(R):
        s = jnp.where(cmask, jnp.dot(kc, qt[:, heads[r]], preferred_element_type=F32), NEG_INF)
        e = jnp.exp2(s - jnp.max(s, axis=0, keepdims=True))
        p = jnp.where(cmask, e * (1.0 / jnp.sum(e, axis=0, keepdims=True)), 0.0)
        out_scr[:, heads[r]] = gate[r:r + 1, :] * jnp.dot(vct, p.astype(BF16), preferred_element_type=F32)
        psum = psum + p
    bi = lax.broadcasted_iota(jnp.int32, (n_blk, nk), 0)
    ni = lax.broadcasted_iota(jnp.int32, (n_blk, nk), 1)
    ovl = ((ni * CMP_STRIDE < bi * SEL_BLOCK + SEL_BLOCK) & (ni * CMP_STRIDE + CMP_BLOCK > bi * SEL_BLOCK)
           & (ni < n_cmp))
    imp = jnp.dot(jnp.where(ovl, 1.0, 0.0), psum, preferred_element_type=F32, precision=lax.Precision.HIGHEST)
    blk = lax.broadcasted_iota(jnp.int32, (n_blk, tq), 0)
    cur = (i * tq + lax.broadcasted_iota(jnp.int32, (n_blk, tq), 1)) // SEL_BLOCK
    forced = (blk == 0) | (blk == cur) | (blk == cur - 1)
    score = jnp.where(forced, FORCE_SCORE, jnp.where(blk <= cur, imp, NEG_INF))
    rank = jnp.zeros((n_blk, tq), F32)
    for k in range(n_blk):
        sk = score[k:k + 1, :]
        rank = rank + jnp.where((sk > score) | ((sk == score) & (k < blk)), 1.0, 0.0)
    sel = jnp.where(rank < N_SEL, 1.0, 0.0).astype(BF16)

    m_scr[...] = jnp.full(m_scr.shape, NEG_INF, F32)
    acc_scr[...] = jnp.zeros_like(acc_scr)

    def sel_body(c, carry):
        k0 = pl.multiple_of(c * tks, tks)
        key = k0 + lax.broadcasted_iota(jnp.int32, (tks, tq), 0)
        t = i * tq + lax.broadcasted_iota(jnp.int32, (tks, tq), 1)
        in_blk = (lax.broadcasted_iota(jnp.int32, (tks, n_blk), 1)
                  == (k0 + lax.broadcasted_iota(jnp.int32, (tks, n_blk), 0)) // SEL_BLOCK)
        picked = jnp.dot(jnp.where(in_blk, 1.0, 0.0).astype(BF16), sel, preferred_element_type=F32)
        bias = jnp.where((picked > 0.5) & (key <= t), 0.0, NEG_INF)
        k_chunk = ks_ref[pl.ds(k0, tks), :]
        vt_chunk = vst[c]
        m_new, alpha = [None] * R, [None] * R

        def scores(r):
            s = jnp.dot(k_chunk, qt[:, heads[r]], preferred_element_type=F32) + bias
            s_scr[r, 0:tks, :] = s
            m_prev = m_scr[r:r + 1, :]
            m_new[r] = jnp.maximum(m_prev, jnp.max(s, axis=0, keepdims=True))
            alpha[r] = jnp.exp2(m_prev - m_new[r])
            m_scr[r:r + 1, :] = m_new[r]

        def probs(r):
            p_scr[r, 0:tks, :] = jnp.exp2(s_scr[r, 0:tks, :] - m_new[r]).astype(BF16)

        def values(r):
            acc_scr[:, heads[r]] = (alpha[r] * acc_scr[:, heads[r]]
                                    + jnp.dot(vt_chunk, p_scr[r, 0:tks, :], preferred_element_type=F32))

        _staged(R, scores, probs, values)
        return carry

    lax.fori_loop(0, (i * tq + tq - 1) // tks + 1, sel_body, 0)
    for r in range(R):
        o = acc_scr[0:d, heads[r]] * (1.0 / acc_scr[d:d + 1, heads[r]])
        out_scr[:, heads[r]] += gate[R + r:R + r + 1, :] * o

    span = WINDOW + tq
    k0 = pl.multiple_of(jnp.maximum(i * tq - WINDOW, 0), tq)
    dist = (i * tq + lax.broadcasted_iota(jnp.int32, (span, tq), 1)
            - (k0 + lax.broadcasted_iota(jnp.int32, (span, tq), 0)))
    bias = jnp.where((dist >= 0) & (dist < WINDOW), 0.0, NEG_INF)
    k_win = kw_ref[pl.ds(k0, span), :]
    vt_win = vwt[:, pl.ds(k0, span)]
    mx = [None] * R

    def scores(r):
        s = jnp.dot(k_win, qt[:, heads[r]], preferred_element_type=F32) + bias
        s_scr[r] = s
        mx[r] = jnp.max(s, axis=0, keepdims=True)

    def probs(r):
        p_scr[r] = jnp.exp2(s_scr[r] - mx[r]).astype(BF16)

    def values(r):
        o = jnp.dot(vt_win, p_scr[r], preferred_element_type=F32)
        out_scr[:, heads[r]] += gate[2 * R + r:2 * R + r + 1, :] * (o[0:d, :] * (1.0 / o[d:d + 1, :]))

    _staged(R, scores, probs, values)

    for r in range(R):
        o_ref[:, r * d:(r + 1) * d] = out_scr[:, heads[r]].T.astype(o_ref.dtype)


def _nsa(qkv, kc, vct, small_t, B, S, tq=256, tks=512):
    assert WINDOW % tq == 0 and S >= WINDOW + tq
    T = B * S
    G, d, R = ATTN_KV_GROUPS, ATTN_HEAD_DIM, ATTN_REP
    nq = S // tq
    nk = kc.shape[2]
    n_cmp = S // CMP_STRIDE - CMP_BLOCK // CMP_STRIDE + 1
    n_blk = S // SEL_BLOCK
    seq = lambda c0: pl.BlockSpec((S, d), lambda b, g, i: (b, c0 + g))
    cmp_spec = pl.BlockSpec((1, 1, nk, d), lambda b, g, i: (b, g, 0, 0))
    gate0 = SSM_HEADS // GATE_ROWS
    return pl.pallas_call(
        functools.partial(_nsa_kernel, tq=tq, tks=tks, n_cmp=n_cmp, n_blk=n_blk),
        grid=(B, G, nq),
        in_specs=[pl.BlockSpec((tq, KV_WIDTH), lambda b, g, i: (b * nq + i, g)),
                  cmp_spec, cmp_spec,
                  seq(QKV_KS), seq(QKV_VS), seq(QKV_KW), seq(QKV_VW),
                  pl.BlockSpec((1, GATE_ROWS, tq), lambda b, g, i: (b, gate0 + g, i))],
        out_specs=pl.BlockSpec((tq, KV_WIDTH), lambda b, g, i: (b * nq + i, g)),
        out_shape=jax.ShapeDtypeStruct((T, ATTN_WIDTH), BF16),
        scratch_shapes=[pltpu.VMEM((S // tks, d + ONES_ROWS, tks), BF16),
                        pltpu.VMEM((d + ONES_ROWS, S), BF16),
                        pltpu.VMEM((d, R * tq), BF16),
                        pltpu.VMEM((8, tq), F32),
                        pltpu.VMEM((d + ONES_ROWS, R * tq), F32),
                        pltpu.VMEM((d, R * tq), F32),
                        pltpu.VMEM((R, WINDOW + tq, tq), F32),
                        pltpu.VMEM((R, WINDOW + tq, tq), BF16)],
        compiler_params=_cparams(("parallel", "parallel", "arbitrary")),
        name="nsa",
    )(qkv, kc, vct, qkv, qkv, qkv, qkv, small_t)


def _merge_kernel(ys_ref, ws_ref, ya_ref, wa_ref, gs_ref, ga_ref, o_ref, wsb, wab):
    _cast_per_column_tile(((ws_ref, wsb), (wa_ref, wab)))
    s = jnp.dot(ys_ref[...], wsb[...], preferred_element_type=F32)
    a = jnp.dot(ya_ref[...], wab[...], preferred_element_type=F32)
    o_ref[...] = (gs_ref[...].astype(F32) * s + ga_ref[...].astype(F32) * a).astype(o_ref.dtype)


def _merge(y_ssm, w_s, y_attn, w_a, gm, tm=512, tn=512):
    T = y_ssm.shape[0]
    return pl.pallas_call(
        _merge_kernel,
        grid=(D_MODEL // tn, T // tm),
        in_specs=[pl.BlockSpec((tm, D_INNER), lambda n, m: (m, 0)),
                  pl.BlockSpec((D_INNER, tn), lambda n, m: (0, n)),
                  pl.BlockSpec((tm, ATTN_WIDTH), lambda n, m: (m, 0)),
                  pl.BlockSpec((ATTN_WIDTH, tn), lambda n, m: (0, n)),
                  pl.BlockSpec((tm, tn), lambda n, m: (m, n)),
                  pl.BlockSpec((tm, tn), lambda n, m: (m, D_MODEL // tn + n))],
        out_specs=pl.BlockSpec((tm, tn), lambda n, m: (m, n)),
        out_shape=jax.ShapeDtypeStruct((T, D_MODEL), BF16),
        scratch_shapes=[pltpu.VMEM((D_INNER, tn), BF16), pltpu.VMEM((ATTN_WIDTH, tn), BF16)],
        compiler_params=_cparams(("parallel", "arbitrary")),
        name="merge",
    )(y_ssm, w_s, y_attn, w_a, gm, gm)


def _mix_out_kernel(mg_ref, w_ref, x_ref, nw_ref, x1_ref, h_ref):
    x1 = x_ref[...] + jnp.dot(mg_ref[...], w_ref[...], preferred_element_type=F32)
    x1_ref[...] = x1
    h_ref[...] = _rms(x1, nw_ref[...]).astype(h_ref.dtype)


def _mix_out(merged, w, x2d, norm_w, tm=512):
    T = x2d.shape[0]
    return pl.pallas_call(
        _mix_out_kernel,
        grid=(T // tm,),
        in_specs=[pl.BlockSpec((tm, D_MODEL), lambda m: (m, 0)),
                  _resident((D_MODEL, D_MODEL)),
                  pl.BlockSpec((tm, D_MODEL), lambda m: (m, 0)),
                  pl.BlockSpec((1, D_MODEL), lambda m: (0, 0))],
        out_specs=[pl.BlockSpec((tm, D_MODEL), lambda m: (m, 0)),
                   pl.BlockSpec((tm, D_MODEL), lambda m: (m, 0))],
        out_shape=[jax.ShapeDtypeStruct((T, D_MODEL), F32),
                   jax.ShapeDtypeStruct((T, D_MODEL), BF16)],
        compiler_params=_cparams(("parallel",)),
        name="mix_out",
    )(merged, w, x2d, norm_w)


def _ffn_up_kernel(h_ref, wg_ref, wu_ref, cw_ref, cb_ref, o_ref, gbuf, wgb, wub, *, tiles_per_seq):
    m = pl.program_id(1)
    _cast_per_column_tile(((wg_ref, wgb), (wu_ref, wub)))

    @pl.when(m % tiles_per_seq == 0)
    def _():
        gbuf[0:TAIL, :] = jnp.zeros((TAIL, gbuf.shape[1]), F32)

    h = h_ref[...]
    g = jnp.dot(h, wgb[...], preferred_element_type=F32)
    u = jnp.dot(h, wub[...], preferred_element_type=F32)
    y, y_top = _causal_conv(g, gbuf, cw_ref, cb_ref)
    o_ref[...] = (_silu(y) * u).astype(o_ref.dtype)
    o_ref[0:FIX, :] = (_silu(y_top) * u[0:FIX, :]).astype(o_ref.dtype)


def _ffn_up(h, wg, wu, conv_w, conv_b, S, tm=1024, tn=512):
    T = h.shape[0]
    return pl.pallas_call(
        functools.partial(_ffn_up_kernel, tiles_per_seq=S // tm),
        grid=(D_FF // tn, T // tm),
        in_specs=[pl.BlockSpec((tm, D_MODEL), lambda n, m: (m, 0)),
                  pl.BlockSpec((D_MODEL, tn), lambda n, m: (0, n)),
                  pl.BlockSpec((D_MODEL, tn), lambda n, m: (0, n)),
                  pl.BlockSpec((FFN_CONV, tn), lambda n, m: (0, n)),
                  pl.BlockSpec((1, tn), lambda n, m: (0, n))],
        out_specs=pl.BlockSpec((tm, tn), lambda n, m: (m, n)),
        out_shape=jax.ShapeDtypeStruct((T, D_FF), BF16),
        scratch_shapes=[pltpu.VMEM((TAIL + FIX, tn), F32),
                        pltpu.VMEM((D_MODEL, tn), BF16), pltpu.VMEM((D_MODEL, tn), BF16)],
        compiler_params=_cparams(("parallel", "arbitrary")),
        name="ffn_up",
    )(h, wg, wu, conv_w, conv_b.reshape(1, D_FF))


def _ffn_down_kernel(a_ref, w_ref, x_ref, nw_ref, x2_ref, h_ref):
    x2 = x_ref[...] + jnp.dot(a_ref[...], w_ref[...], preferred_element_type=F32)
    x2_ref[...] = x2
    h_ref[...] = _rms(x2, nw_ref[...]).astype(h_ref.dtype)


def _ffn_down(act, w, x1, norm_w, tm=256):
    T = x1.shape[0]
    return pl.pallas_call(
        _ffn_down_kernel,
        grid=(T // tm,),
        in_specs=[pl.BlockSpec((tm, D_FF), lambda m: (m, 0)),
                  _resident((D_FF, D_MODEL)),
                  pl.BlockSpec((tm, D_MODEL), lambda m: (m, 0)),
                  _resident((1, D_MODEL))],
        out_specs=[pl.BlockSpec((tm, D_MODEL), lambda m: (m, 0)),
                   pl.BlockSpec((tm, D_MODEL), lambda m: (m, 0))],
        out_shape=[jax.ShapeDtypeStruct((T, D_MODEL), F32),
                   jax.ShapeDtypeStruct((T, D_MODEL), BF16)],
        compiler_params=_cparams(("parallel",)),
        name="ffn_down",
    )(act, w, x1, norm_w)


def _ple_kernel(h_ref, wg_ref, p_ref, wp_ref, x_ref, nw_ref, o_ref):
    gate = jax.nn.sigmoid(jnp.dot(h_ref[...], wg_ref[...], preferred_element_type=F32))
    emb = jnp.dot(p_ref[...].astype(BF16), wp_ref[...], preferred_element_type=F32)
    o_ref[...] = _rms(x_ref[...] + gate * emb, nw_ref[...])


def _ple(h, wg, p2d, wp, x2, norm_w, tm=512):
    T = x2.shape[0]
    return pl.pallas_call(
        _ple_kernel,
        grid=(T // tm,),
        in_specs=[pl.BlockSpec((tm, D_MODEL), lambda m: (m, 0)),
                  _resident((D_MODEL, D_MODEL)),
                  pl.BlockSpec((tm, PLE_DIM), lambda m: (m, 0)),
                  _resident((PLE_DIM, D_MODEL)),
                  pl.BlockSpec((tm, D_MODEL), lambda m: (m, 0)),
                  pl.BlockSpec((1, D_MODEL), lambda m: (0, 0))],
        out_specs=pl.BlockSpec((tm, D_MODEL), lambda m: (m, 0)),
        out_shape=jax.ShapeDtypeStruct((T, D_MODEL), F32),
        compiler_params=_cparams(("parallel",)),
        name="ple",
    )(h, wg, p2d, wp, x2, norm_w)


def _mixer_inputs(x2d, norm_w, w, ssm_conv_w, ssm_conv_b, B, S):
    wt = w.T
    h, small, small_t = _norm_small(x2d, norm_w.reshape(1, D_MODEL), _small_weight(wt), B, S)
    zs = _proj("silu", h, wt, 0, D_INNER, S)
    xbc = _proj("conv", h, wt, D_INNER, SSM_CONV_DIM, S,
                extra=(ssm_conv_w, ssm_conv_b.reshape(1, SSM_CONV_DIM)))
    qkv = _proj("rope", h, wt, NAT_Q, QKV_WIDTH, S, extra=_rope_tables(S))
    gm = _proj("sigmoid", h, wt, NAT_GM, 2 * D_MODEL, S)
    return small, small_t, zs, xbc, qkv, gm


def _layer(x2d, p2d, B, S, norm_mix_w, w_in, ssm_conv_w, ssm_conv_b, ssm_dt_bias, ssm_a_log, ssm_d,
           ssm_norm_w, cmp_pe_k, cmp_pe_v, cmp_wk1, cmp_wk2, cmp_wv1, cmp_wv2,
           w_ssm_branch, w_attn_branch, w_mix_out, norm_ffn_w, ffn_w_gate, ffn_w_up,
           ffn_conv_w, ffn_conv_b, ffn_w_down, ple_norm_w, ple_w_gate, ple_w_proj, out_norm_w):
    small, small_t, zs, xbc, qkv, gm = _mixer_inputs(x2d, norm_mix_w, w_in, ssm_conv_w, ssm_conv_b, B, S)
    y_ssm = _ssd(xbc, zs, small, small_t, ssm_dt_bias, ssm_a_log, ssm_d, ssm_norm_w, B, S)
    kc, vct = _compress(qkv, cmp_pe_k, cmp_pe_v, cmp_wk1, cmp_wv1, cmp_wk2, cmp_wv2, B, S)
    y_attn = _nsa(qkv, kc, vct, small_t, B, S)

    merged = _merge(y_ssm, w_ssm_branch, y_attn, w_attn_branch, gm)
    x1, h2 = _mix_out(merged, w_mix_out.astype(BF16), x2d, norm_ffn_w.reshape(1, D_MODEL))

    act = _ffn_up(h2, ffn_w_gate, ffn_w_up, ffn_conv_w, ffn_conv_b, S)
    x2, h3 = _ffn_down(act, ffn_w_down.astype(BF16), x1, ple_norm_w.reshape(1, D_MODEL))
    return _ple(h3, ple_w_gate.astype(BF16), p2d, ple_w_proj.astype(BF16), x2, out_norm_w.reshape(1, D_MODEL))


def kernel(x, p, norm_mix_w, w_in, ssm_conv_w, ssm_conv_b, ssm_dt_bias, ssm_a_log, ssm_d, ssm_norm_w, cmp_pe_k, cmp_pe_v, cmp_wk1, cmp_wk2, cmp_wv1, cmp_wv2, w_ssm_branch, w_attn_branch, w_mix_out, norm_ffn_w, ffn_w_gate, ffn_w_up, ffn_conv_w, ffn_conv_b, ffn_w_down, ple_norm_w, ple_w_gate, ple_w_proj, final_norm_w):
    B, S, D = x.shape
    depth = w_in.shape[0]
    assert depth == 1, "the final norm is fused into the (single) layer's last kernel"
    x2d = x.reshape(B * S, D)
    out = _layer(x2d, p[0].reshape(B * S, PLE_DIM), B, S, norm_mix_w[0], w_in[0], ssm_conv_w[0], ssm_conv_b[0],
                 ssm_dt_bias[0], ssm_a_log[0], ssm_d[0], ssm_norm_w[0], cmp_pe_k[0], cmp_pe_v[0],
                 cmp_wk1[0], cmp_wk2[0], cmp_wv1[0], cmp_wv2[0], w_ssm_branch[0], w_attn_branch[0],
                 w_mix_out[0], norm_ffn_w[0], ffn_w_gate[0], ffn_w_up[0], ffn_conv_w[0], ffn_conv_b[0],
                 ffn_w_down[0], ple_norm_w[0], ple_w_gate[0], ple_w_proj[0], final_norm_w)
    return out.reshape(B, S, D)
```

```python
import functools
import math

import jax
import jax.numpy as jnp
from jax import lax
from jax.experimental import pallas as pl
from jax.experimental.pallas import tpu as pltpu

F32 = jnp.float32
BF16 = jnp.bfloat16

D_MODEL = 2048
PLE_DIM = 256
EPS = 1e-6
D_INNER = 4096
SSM_HEAD_DIM = 64
SSM_HEADS = 64
SSM_GROUPS = 8
SSM_STATE = 128
SSM_CONV = 4
SSM_CHUNK = 128
SSM_BC = SSM_GROUPS * SSM_STATE
SSM_CONV_DIM = D_INNER + 2 * SSM_BC
ATTN_HEADS = 16
ATTN_HEAD_DIM = 128
ATTN_KV_GROUPS = 4
ATTN_REP = ATTN_HEADS // ATTN_KV_GROUPS
ATTN_WIDTH = ATTN_HEADS * ATTN_HEAD_DIM
KV_WIDTH = ATTN_KV_GROUPS * ATTN_HEAD_DIM
CMP_BLOCK = 32
CMP_STRIDE = 16
SEL_BLOCK = 64
N_SEL = 16
WINDOW = 512
ROPE_THETA = 500000.0
ROPE_DIM = ATTN_HEAD_DIM // 4
D_FF = 5632
FFN_CONV = 3
NEG_INF = -1e30
FORCE_SCORE = 1e4

LANES = 128
TAIL = 8
FIX = 16
VMEM_LIMIT = 56 * 1024 * 1024

NAT_DT = D_INNER + SSM_CONV_DIM
NAT_Q = NAT_DT + SSM_HEADS
NAT_GN = NAT_Q + ATTN_WIDTH + 6 * KV_WIDTH
NAT_GM = NAT_GN + 3 * ATTN_HEADS
QKV_WIDTH = ATTN_WIDTH + 6 * KV_WIDTH
WTILE = 1024
N_SMALL = 128
GATE_ROWS = 16
ONES_ROWS = 16
QKV_KC, QKV_VC, QKV_KS, QKV_VS, QKV_KW, QKV_VW = (
    (ATTN_WIDTH + i * KV_WIDTH) // ATTN_HEAD_DIM for i in range(6))


def _cparams(sem):
    return pltpu.CompilerParams(dimension_semantics=sem, vmem_limit_bytes=VMEM_LIMIT)


def _resident(shape):
    return pl.BlockSpec(shape, lambda *_: (0,) * len(shape), pipeline_mode=pl.Buffered(1))


def _rms(xf, w):
    return xf * lax.rsqrt(jnp.mean(xf * xf, axis=-1, keepdims=True) + EPS) * w


def _silu(v):
    return v * jax.nn.sigmoid(v)


def _causal_conv(x, top, w_ref, bias_ref):
    taps = w_ref.shape[0]
    bias = bias_ref[...]
    y = bias + w_ref[taps - 1:taps, :] * x
    for s in range(1, taps):
        y = y + w_ref[taps - 1 - s:taps - s, :] * pltpu.roll(x, s, 0)
    top[TAIL:TAIL + FIX, :] = x[0:FIX, :]
    y_top = bias
    for k in range(taps):
        off = TAIL - (taps - 1) + k
        y_top = y_top + w_ref[k:k + 1, :] * top[off:off + FIX, :]
    top[0:TAIL, :] = x[x.shape[0] - TAIL:, :]
    return y, y_top


def _cast_per_column_tile(pairs):
    @pl.when(pl.program_id(1) == 0)
    def _():
        for src, dst in pairs:
            dst[...] = src[...].astype(BF16)


def _wide_dot(h_ref, wt_ref, wb):
    _cast_per_column_tile(((wt_ref, wb),))
    return lax.dot_general(h_ref[...], wb[...], (((1,), (1,)), ((), ())), preferred_element_type=F32)


def _small_weight(wt):
    G, R = ATTN_KV_GROUPS, ATTN_REP
    K = wt.shape[1]
    w_dt = wt[NAT_DT:NAT_DT + SSM_HEADS]
    w_g = wt[NAT_GN:NAT_GN + 3 * ATTN_HEADS].reshape(3, G, R, K).transpose(1, 0, 2, 3).reshape(G, 3 * R, K)
    w_g = jnp.pad(w_g, ((0, 0), (0, GATE_ROWS - 3 * R), (0, 0))).reshape(G * GATE_ROWS, K)
    return jnp.concatenate([w_dt, w_g], axis=0)


def _norm_small_kernel(x_ref, nw_ref, ws_ref, h_ref, small_ref, smallt_ref):
    h = _rms(x_ref[...], nw_ref[...])
    h_ref[...] = h.astype(BF16)
    small = lax.dot_general(h, ws_ref[...], (((1,), (1,)), ((), ())), preferred_element_type=F32,
                            precision=lax.Precision.HIGHEST)
    small_ref[...] = small
    smallt_ref[0] = small.T


def _norm_small(x2d, norm_w, w_small, B, S, tm=512):
    T = x2d.shape[0]
    per = S // tm
    return pl.pallas_call(
        _norm_small_kernel,
        grid=(T // tm,),
        in_specs=[pl.BlockSpec((tm, D_MODEL), lambda i: (i, 0)),
                  pl.BlockSpec((1, D_MODEL), lambda i: (0, 0)),
                  pl.BlockSpec((N_SMALL, D_MODEL), lambda i: (0, 0))],
        out_specs=[pl.BlockSpec((tm, D_MODEL), lambda i: (i, 0)),
                   pl.BlockSpec((tm, N_SMALL), lambda i: (i, 0)),
                   pl.BlockSpec((1, N_SMALL, tm), lambda i: (i // per, 0, i % per))],
        out_shape=[jax.ShapeDtypeStruct((T, D_MODEL), BF16),
                   jax.ShapeDtypeStruct((T, N_SMALL), F32),
                   jax.ShapeDtypeStruct((B, N_SMALL, S), F32)],
        compiler_params=_cparams(("parallel",)),
        name="norm_small",
    )(x2d, norm_w, w_small)


def _proj_act_kernel(h_ref, wt_ref, o_ref, wb, *, act):
    o_ref[...] = act(_wide_dot(h_ref, wt_ref, wb)).astype(o_ref.dtype)


def _proj_conv_kernel(h_ref, wt_ref, cw_ref, cb_ref, o_ref, wb, buf, *, tiles_per_seq):
    m = pl.program_id(1)

    @pl.when(m % tiles_per_seq == 0)
    def _():
        buf[0:TAIL, :] = jnp.zeros((TAIL, buf.shape[1]), F32)

    y, y_top = _causal_conv(_wide_dot(h_ref, wt_ref, wb), buf, cw_ref, cb_ref)
    o_ref[...] = _silu(y).astype(o_ref.dtype)
    o_ref[0:FIX, :] = _silu(y_top).astype(o_ref.dtype)


def _proj_rope_kernel(h_ref, wt_ref, ca_ref, cb_ref, cc_ref, o_ref, wb, *, scale):
    n = pl.program_id(0)
    d = ATTN_HEAD_DIM
    half = ROPE_DIM // 2
    is_q = n < ATTN_WIDTH // WTILE
    acc = _wide_dot(h_ref, wt_ref, wb)
    ca, cb, cc = ca_ref[...], cb_ref[...], cc_ref[...]
    q_scale = jnp.where(is_q, scale, 1.0).astype(F32)
    for hd in range(WTILE // d):
        xh = acc[:, hd * d:(hd + 1) * d]
        up = pltpu.roll(xh, d - half, 1)
        dn = pltpu.roll(xh, half, 1)
        roped = (xh * ca + up * cb + dn * cc) * q_scale
        if hd >= KV_WIDTH // d:
            roped = jnp.where(is_q, roped, xh)
        o_ref[:, hd * d:(hd + 1) * d] = roped.astype(o_ref.dtype)


def _rope_tables(S):
    half = ROPE_DIM // 2
    pos = jnp.arange(S)
    inv = ROPE_THETA ** (-(jnp.arange(half, dtype=F32) * 2.0 / ROPE_DIM))
    ang = pos.astype(F32)[:, None] * inv[None, :]
    cos, sin = jnp.cos(ang), jnp.sin(ang)
    rest = ATTN_HEAD_DIM - ROPE_DIM
    ca = jnp.concatenate([cos, cos, jnp.ones((S, rest), F32)], axis=1)
    cb = jnp.concatenate([-sin, jnp.zeros((S, ATTN_HEAD_DIM - half), F32)], axis=1)
    cc = jnp.concatenate([jnp.zeros((S, half), F32), sin, jnp.zeros((S, rest), F32)], axis=1)
    return ca, cb, cc


def _proj(kind, h, wt, row0, width, S, extra=(), tm=1024):
    T, K = h.shape
    tn = WTILE
    per = S // tm
    in_specs = [pl.BlockSpec((tm, K), lambda n, m: (m, 0)),
                pl.BlockSpec((pl.Element(tn), pl.Element(K)),
                             lambda n, m: (pl.multiple_of(row0 + n * tn, TAIL), 0))]
    scratch = [pltpu.VMEM((tn, K), BF16)]
    if kind == "silu":
        body = functools.partial(_proj_act_kernel, act=_silu)
    elif kind == "sigmoid":
        body = functools.partial(_proj_act_kernel, act=jax.nn.sigmoid)
    elif kind == "conv":
        body = functools.partial(_proj_conv_kernel, tiles_per_seq=per)
        taps = extra[0].shape[0]
        in_specs += [pl.BlockSpec((taps, tn), lambda n, m: (0, n)),
                     pl.BlockSpec((1, tn), lambda n, m: (0, n))]
        scratch += [pltpu.VMEM((TAIL + FIX, tn), F32)]
    else:
        body = functools.partial(_proj_rope_kernel, scale=math.log2(math.e) / math.sqrt(ATTN_HEAD_DIM))
        tab = pl.BlockSpec((tm, ATTN_HEAD_DIM), lambda n, m: (m % per, 0))
        in_specs += [tab, tab, tab]
    return pl.pallas_call(
        body,
        grid=(width // tn, T // tm),
        in_specs=in_specs,
        out_specs=pl.BlockSpec((tm, tn), lambda n, m: (m, n)),
        out_shape=jax.ShapeDtypeStruct((T, width), BF16),
        scratch_shapes=scratch,
        compiler_params=_cparams(("parallel", "arbitrary")),
        name="proj_" + kind,
    )(h, wt, *extra)


def _ssd_scalars_kernel(raw_ref, rawt_ref, dtb_ref, dtbt_ref, alog_ref, alogt_ref, acum_ref, dtt_ref, acumt_ref):
    L = SSM_CHUNK
    H = SSM_HEADS
    ri = lax.broadcasted_iota(jnp.int32, (L, L), 0)
    ci = lax.broadcasted_iota(jnp.int32, (L, L), 1)
    hp = lax.Precision.HIGHEST
    dt = jax.nn.softplus(raw_ref[:, 0:H] + dtb_ref[...])
    dtt = jax.nn.softplus(rawt_ref[0, 0:H, :] + dtbt_ref[...])
    a = -jnp.exp(alog_ref[...])
    at = -jnp.exp(alogt_ref[...])
    acum = jnp.dot(jnp.where(ci <= ri, 1.0, 0.0), dt * a, preferred_element_type=F32, precision=hp)
    per = H // SSM_GROUPS
    for g in range(SSM_GROUPS):
        acum_ref[0, g] = acum[:, g * per:(g + 1) * per]
    dtt_ref[0] = dtt
    acumt_ref[0] = jnp.dot(dtt * at, jnp.where(ri <= ci, 1.0, 0.0), preferred_element_type=F32, precision=hp)


def _ssd_scalars(small, small_t, dt_bias, a_log, B, S):
    T = B * S
    L, H = SSM_CHUNK, SSM_HEADS
    nc = S // L
    head_major = pl.BlockSpec((1, H, L), lambda b, c: (b, 0, c))
    return pl.pallas_call(
        _ssd_scalars_kernel,
        grid=(B, nc),
        in_specs=[pl.BlockSpec((L, N_SMALL), lambda b, c: (b * nc + c, 0)),
                  pl.BlockSpec((1, N_SMALL, L), lambda b, c: (b, 0, c)),
                  pl.BlockSpec((1, H), lambda b, c: (0, 0)),
                  pl.BlockSpec((H, 1), lambda b, c: (0, 0)),
                  pl.BlockSpec((1, H), lambda b, c: (0, 0)),
                  pl.BlockSpec((H, 1), lambda b, c: (0, 0))],
        out_specs=[pl.BlockSpec((1, SSM_GROUPS, L, H // SSM_GROUPS), lambda b, c: (b, 0, c, 0)),
                   head_major, head_major],
        out_shape=[jax.ShapeDtypeStruct((B, SSM_GROUPS, S, H // SSM_GROUPS), F32),
                   jax.ShapeDtypeStruct((B, H, S), F32),
                   jax.ShapeDtypeStruct((B, H, S), F32)],
        compiler_params=_cparams(("parallel", "parallel")),
        name="ssd_scalars",
    )(small, small_t, dt_bias.reshape(1, H), dt_bias.reshape(H, 1), a_log.reshape(1, H), a_log.reshape(H, 1))


def _ssd_kernel(x_ref, b_ref, c_ref, z_ref, acum_ref, acumt_ref, dtt_ref, dsk_ref, nw_ref, y_ref, h_scr):
    @pl.when(pl.program_id(2) == 0)
    def _():
        h_scr[...] = jnp.zeros_like(h_scr)

    def chunk(j, carry):
        _ssd_chunk(pl.ds(pl.multiple_of(j * SSM_CHUNK, SSM_CHUNK), SSM_CHUNK),
                   x_ref, b_ref, c_ref, z_ref, acum_ref, acumt_ref, dtt_ref, dsk_ref, nw_ref, y_ref, h_scr)
        return carry

    lax.fori_loop(0, x_ref.shape[0] // SSM_CHUNK, chunk, 0)


def _ssd_chunk(rows, x_ref, b_ref, c_ref, z_ref, acum_ref, acumt_ref, dtt_ref, dsk_ref, nw_ref, y_ref, h_scr):
    L = SSM_CHUNK
    P = SSM_HEAD_DIM
    xb = x_ref[rows, :]
    bmb = b_ref[rows, :]
    cmb = c_ref[rows, :]
    n_heads = SSM_HEADS // SSM_GROUPS
    acum = acum_ref[0, 0, rows, :]
    acumT = acumt_ref[0, :, rows]
    dtT = dtt_ref[0, :, rows]
    ri = lax.broadcasted_iota(jnp.int32, (L, L), 0)
    ci = lax.broadcasted_iota(jnp.int32, (L, L), 1)
    tril = ci <= ri

    cb = lax.dot_general(cmb, bmb, (((1,), (1,)), ((), ())), preferred_element_type=F32)
    yoff = lax.dot_general(cmb, h_scr[...].astype(BF16), (((1,), (1,)), ((), ())),
                           preferred_element_type=F32)

    lane = lax.broadcasted_iota(jnp.int32, (L, 2 * P), 1)
    lo = lane < P

    def pair(v0, v1):
        return jnp.where(lo, v0, v1)

    dsk = dsk_ref[...]
    ys = []
    for pr in range(n_heads // 2):
        sl = slice(pr * 2 * P, (pr + 1) * 2 * P)
        xp = xb[:, sl]
        yd, ea = [], []
        for r in (2 * pr, 2 * pr + 1):
            acol = jnp.broadcast_to(acum[:, r:r + 1], (L, L))
            dec = jnp.exp(jnp.where(tril, acol - acumT[r:r + 1, :], -jnp.inf))
            yd.append(jnp.dot((cb * dec * dtT[r:r + 1, :]).astype(BF16), xp, preferred_element_type=F32))
            ea.append(jnp.exp(acol))
        ys.append(pair(yd[0], yd[1]) + yoff[:, sl] * pair(ea[0], ea[1]) + dsk[:, sl] * xp.astype(F32))

    a_last = acumT[:, L - 1:L]
    wT = dtT * jnp.exp(a_last - acumT)
    cdT = jnp.exp(a_last)
    xT = xb.astype(F32).T
    xds = jnp.concatenate([xT[r * P:(r + 1) * P, :] * wT[r:r + 1, :] for r in range(n_heads)], axis=0)
    cd = jnp.concatenate([jnp.broadcast_to(cdT[r:r + 1, :], (P, h_scr.shape[1])) for r in range(n_heads)], axis=0)
    st = jnp.dot(xds.astype(BF16), bmb, preferred_element_type=F32)
    h_scr[...] = h_scr[...] * cd + st

    y = jnp.concatenate(ys, axis=1) * z_ref[rows, :].astype(F32)
    y_ref[rows, :] = _rms(y, nw_ref[...]).astype(y_ref.dtype)


def _ssd(xbc, zs, small, small_t, dt_bias, a_log, d_skip, norm_w, B, S, rows=512):
    T = B * S
    L, G, N = rows, SSM_GROUPS, SSM_STATE
    R = SSM_HEADS // G
    gw = D_INNER // G
    nc = S // L
    acum_g, dt_t, acum_t = _ssd_scalars(small, small_t, dt_bias, a_log, B, S)
    dsk = jnp.repeat(d_skip, SSM_HEAD_DIM).reshape(1, D_INNER)
    nw = norm_w.reshape(1, D_INNER)

    row = lambda b, g, c: b * nc + c
    in_specs = [
        pl.BlockSpec((L, gw), lambda b, g, c: (row(b, g, c), g)),
        pl.BlockSpec((L, N), lambda b, g, c: (row(b, g, c), D_INNER // N + g)),
        pl.BlockSpec((L, N), lambda b, g, c: (row(b, g, c), (D_INNER + SSM_BC) // N + g)),
        pl.BlockSpec((L, gw), lambda b, g, c: (row(b, g, c), g)),
        pl.BlockSpec((1, 1, L, R), lambda b, g, c: (b, g, c, 0)),
        pl.BlockSpec((1, R, L), lambda b, g, c: (b, g, c)),
        pl.BlockSpec((1, R, L), lambda b, g, c: (b, g, c)),
        pl.BlockSpec((1, gw), lambda b, g, c: (0, g)),
        pl.BlockSpec((1, gw), lambda b, g, c: (0, g)),
    ]
    return pl.pallas_call(
        _ssd_kernel,
        grid=(B, G, nc),
        in_specs=in_specs,
        out_specs=pl.BlockSpec((L, gw), lambda b, g, c: (row(b, g, c), g)),
        out_shape=jax.ShapeDtypeStruct((T, D_INNER), BF16),
        scratch_shapes=[pltpu.VMEM((gw, N), F32)],
        compiler_params=_cparams(("parallel", "parallel", "arbitrary")),
        name="ssd",
    )(xbc, xbc, xbc, zs, acum_g, acum_t, dt_t, dsk, nw)


def _compress_kernel(k_ref, v_ref, pek_ref, pev_ref, wk1_ref, wv1_ref, wk2_ref, wv2_ref, kc_ref, vct_ref, xf):
    d = ATTN_HEAD_DIM
    nsub = xf.shape[0] // CMP_STRIDE

    def mlp(src_ref, pe_ref, w1_ref, w2_ref):
        xf[...] = src_ref[...].astype(F32)
        u = jnp.zeros((nsub, d), F32)
        v = jnp.zeros((nsub, d), F32)
        for l in range(CMP_STRIDE):
            tok = xf[pl.ds(l, nsub, stride=CMP_STRIDE), :]
            l2 = CMP_STRIDE + l
            u = u + jnp.dot((tok + pe_ref[l:l + 1, :]).astype(BF16), w1_ref[l * d:(l + 1) * d, :],
                            preferred_element_type=F32)
            v = v + jnp.dot((tok + pe_ref[l2:l2 + 1, :]).astype(BF16), w1_ref[l2 * d:(l2 + 1) * d, :],
                            preferred_element_type=F32)
        pre = u + pltpu.roll(v, nsub - 1, 0)
        return jnp.dot(_silu(pre).astype(BF16), w2_ref[...], preferred_element_type=F32)

    kc_ref[0, 0] = mlp(k_ref, pek_ref, wk1_ref, wk2_ref).astype(kc_ref.dtype)
    vct_ref[0, 0] = mlp(v_ref, pev_ref, wv1_ref, wv2_ref).T.astype(vct_ref.dtype)


def _compress(qkv, pe_k, pe_v, wk1, wv1, wk2, wv2, B, S):
    G, d = ATTN_KV_GROUPS, ATTN_HEAD_DIM
    nsub = S // CMP_STRIDE
    full = lambda shape: pl.BlockSpec(shape, lambda b, g: (0, 0))
    out = pl.BlockSpec((1, 1, nsub, d), lambda b, g: (b, g, 0, 0))
    return pl.pallas_call(
        _compress_kernel,
        grid=(B, G),
        in_specs=[pl.BlockSpec((S, d), lambda b, g: (b, QKV_KC + g)),
                  pl.BlockSpec((S, d), lambda b, g: (b, QKV_VC + g)),
                  full((CMP_BLOCK, d)), full((CMP_BLOCK, d)),
                  full((CMP_BLOCK * d, d)), full((CMP_BLOCK * d, d)),
                  full((d, d)), full((d, d))],
        out_specs=[out, out],
        out_shape=[jax.ShapeDtypeStruct((B, G, nsub, d), BF16),
                   jax.ShapeDtypeStruct((B, G, d, nsub), BF16)],
        scratch_shapes=[pltpu.VMEM((S, d), F32)],
        compiler_params=_cparams(("parallel", "parallel")),
        name="compress",
    )(qkv, qkv, pe_k, pe_v, wk1.astype(BF16), wv1.astype(BF16), wk2.astype(BF16), wv2.astype(BF16))


def _staged(n, scores, probs, values):
    for step in range(n + 2):
        if step < n:
            scores(step)
        if 1 <= step <= n:
            probs(step - 1)
        if step >= 2:
            values(step - 2)


def _nsa_kernel(q_ref, kc_ref, vct_ref, ks_ref, vs_ref, kw_ref, vw_ref, gt_ref, o_ref,
                vst, vwt, qt, m_scr, acc_scr, out_scr, s_scr, p_scr, *, tq, tks, n_cmp, n_blk):
    i = pl.program_id(2)
    d = ATTN_HEAD_DIM
    R = ATTN_REP
    heads = [slice(r * tq, (r + 1) * tq) for r in range(R)]

    def transposed(v):
        return v.astype(F32).T.astype(BF16)

    @pl.when(i == 0)
    def _():
        for c in range(vst.shape[0]):
            vst[c, 0:d, :] = transposed(vs_ref[c * tks:(c + 1) * tks, :])
            vst[c, d:, :] = jnp.ones((ONES_ROWS, tks), BF16)
        vwt[0:d, :] = transposed(vw_ref[...])
        vwt[d:, :] = jnp.ones((ONES_ROWS, vwt.shape[1]), BF16)

    for r in range(R):
        qt[:, heads[r]] = transposed(q_ref[:, r * d:(r + 1) * d])
    gate = jax.nn.sigmoid(gt_ref[0])

    nk = kc_ref.shape[2]
    kc = kc_ref[0, 0]
    vct = vct_ref[0, 0]
    n_idx = lax.broadcasted_iota(jnp.int32, (nk, tq), 0)
    t_cmp = i * tq + lax.broadcasted_iota(jnp.int32, (nk, tq), 1)
    cmask = (n_idx * CMP_STRIDE + CMP_BLOCK - 1 <= t_cmp) & (n_idx < n_cmp)
    psum = jnp.zeros((nk, tq), F32)
    for r in range(R):
        s = jnp.where(cmask, jnp.dot(kc, qt[:, heads[r]], preferred_element_type=F32), NEG_INF)
        e = jnp.exp2(s - jnp.max(s, axis=0, keepdims=True))
        p = jnp.where(cmask, e * (1.0 / jnp.sum(e, axis=0, keepdims=True)), 0.0)
        out_scr[:, heads[r]] = gate[r:r + 1, :] * jnp.dot(vct, p.astype(BF16), preferred_element_type=F32)
        psum = psum + p
    bi = lax.broadcasted_iota(jnp.int32, (n_blk, nk), 0)
    ni = lax.broadcasted_iota(jnp.int32, (n_blk, nk), 1)
    ovl = ((ni * CMP_STRIDE < bi * SEL_BLOCK + SEL_BLOCK) & (ni * CMP_STRIDE + CMP_BLOCK > bi * SEL_BLOCK)
           & (ni < n_cmp))
    imp = jnp.dot(jnp.where(ovl, 1.0, 0.0), psum, preferred_element_type=F32, precision=lax.Precision.HIGHEST)
    blk = lax.broadcasted_iota(jnp.int32, (n_blk, tq), 0)
    cur = (i * tq + lax.broadcasted_iota(jnp.int32, (n_blk, tq), 1)) // SEL_BLOCK
    forced = (blk == 0) | (blk == cur) | (blk == cur - 1)
    score = jnp.where(forced, FORCE_SCORE, jnp.where(blk <= cur, imp, NEG_INF))
    rank = jnp.zeros((n_blk, tq), F32)
    for k in range(n_blk):
        sk = score[k:k + 1, :]
        rank = rank + jnp.where((sk > score) | ((sk == score) & (k < blk)), 1.0, 0.0)
    sel = jnp.where(rank < N_SEL, 1.0, 0.0).astype(BF16)

    m_scr[...] = jnp.full(m_scr.shape, NEG_INF, F32)
    acc_scr[...] = jnp.zeros_like(acc_scr)

    def sel_body(c, carry):
        k0 = pl.multiple_of(c * tks, tks)
        key = k0 + lax.broadcasted_iota(jnp.int32, (tks, tq), 0)
        t = i * tq + lax.broadcasted_iota(jnp.int32, (tks, tq), 1)
        in_blk = (lax.broadcasted_iota(jnp.int32, (tks, n_blk), 1)
                  == (k0 + lax.broadcasted_iota(jnp.int32, (tks, n_blk), 0)) // SEL_BLOCK)
        picked = jnp.dot(jnp.where(in_blk, 1.0, 0.0).astype(BF16), sel, preferred_element_type=F32)
        bias = jnp.where((picked > 0.5) & (key <= t), 0.0, NEG_INF)
        k_chunk = ks_ref[pl.ds(k0, tks), :]
        vt_chunk = vst[c]
        m_new, alpha = [None] * R, [None] * R

        def scores(r):
            s = jnp.dot(k_chunk, qt[:, heads[r]], preferred_element_type=F32) + bias
            s_scr[r, 0:tks, :] = s
            m_prev = m_scr[r:r + 1, :]
            m_new[r] = jnp.maximum(m_prev, jnp.max(s, axis=0, keepdims=True))
            alpha[r] = jnp.exp2(m_prev - m_new[r])
            m_scr[r:r + 1, :] = m_new[r]

        def probs(r):
            p_scr[r, 0:tks, :] = jnp.exp2(s_scr[r, 0:tks, :] - m_new[r]).astype(BF16)

        def values(r):
            acc_scr[:, heads[r]] = (alpha[r] * acc_scr[:, heads[r]]
                                    + jnp.dot(vt_chunk, p_scr[r, 0:tks, :], preferred_element_type=F32))

        _staged(R, scores, probs, values)
        return carry

    lax.fori_loop(0, (i * tq + tq - 1) // tks + 1, sel_body, 0)
    for r in range(R):
        o = acc_scr[0:d, heads[r]] * (1.0 / acc_scr[d:d + 1, heads[r]])
        out_scr[:, heads[r]] += gate[R + r:R + r + 1, :] * o

    span = WINDOW + tq
    k0 = pl.multiple_of(jnp.maximum(i * tq - WINDOW, 0), tq)
    dist = (i * tq + lax.broadcasted_iota(jnp.int32, (span, tq), 1)
            - (k0 + lax.broadcasted_iota(jnp.int32, (span, tq), 0)))
    bias = jnp.where((dist >= 0) & (dist < WINDOW), 0.0, NEG_INF)
    k_win = kw_ref[pl.ds(k0, span), :]
    vt_win = vwt[:, pl.ds(k0, span)]
    mx = [None] * R

    def scores(r):
        s = jnp.dot(k_win, qt[:, heads[r]], preferred_element_type=F32) + bias
        s_scr[r] = s
        mx[r] = jnp.max(s, axis=0, keepdims=True)

    def probs(r):
        p_scr[r] = jnp.exp2(s_scr[r] - mx[r]).astype(BF16)

    def values(r):
        o = jnp.dot(vt_win, p_scr[r], preferred_element_type=F32)
        out_scr[:, heads[r]] += gate[2 * R + r:2 * R + r + 1, :] * (o[0:d, :] * (1.0 / o[d:d + 1, :]))

    _staged(R, scores, probs, values)

    for r in range(R):
        o_ref[:, r * d:(r + 1) * d] = out_scr[:, heads[r]].T.astype(o_ref.dtype)


def _nsa(qkv, kc, vct, small_t, B, S, tq=256, tks=512):
    assert WINDOW % tq == 0 and S >= WINDOW + tq
    T = B * S
    G, d, R = ATTN_KV_GROUPS, ATTN_HEAD_DIM, ATTN_REP
    nq = S // tq
    nk = kc.shape[2]
    n_cmp = S // CMP_STRIDE - CMP_BLOCK // CMP_STRIDE + 1
    n_blk = S // SEL_BLOCK
    seq = lambda c0: pl.BlockSpec((S, d), lambda b, g, i: (b, c0 + g))
    cmp_spec = pl.BlockSpec((1, 1, nk, d), lambda b, g, i: (b, g, 0, 0))
    gate0 = SSM_HEADS // GATE_ROWS
    return pl.pallas_call(
        functools.partial(_nsa_kernel, tq=tq, tks=tks, n_cmp=n_cmp, n_blk=n_blk),
        grid=(B, G, nq),
        in_specs=[pl.BlockSpec((tq, KV_WIDTH), lambda b, g, i: (b * nq + i, g)),
                  cmp_spec, cmp_spec,
                  seq(QKV_KS), seq(QKV_VS), seq(QKV_KW), seq(QKV_VW),
                  pl.BlockSpec((1, GATE_ROWS, tq), lambda b, g, i: (b, gate0 + g, i))],
        out_specs=pl.BlockSpec((tq, KV_WIDTH), lambda b, g, i: (b * nq + i, g)),
        out_shape=jax.ShapeDtypeStruct((T, ATTN_WIDTH), BF16),
        scratch_shapes=[pltpu.VMEM((S // tks, d + ONES_ROWS, tks), BF16),
                        pltpu.VMEM((d + ONES_ROWS, S), BF16),
                        pltpu.VMEM((d, R * tq), BF16),
                        pltpu.VMEM((8, tq), F32),
                        pltpu.VMEM((d + ONES_ROWS, R * tq), F32),
                        pltpu.VMEM((d, R * tq), F32),
                        pltpu.VMEM((R, WINDOW + tq, tq), F32),
                        pltpu.VMEM((R, WINDOW + tq, tq), BF16)],
        compiler_params=_cparams(("parallel", "parallel", "arbitrary")),
        name="nsa",
    )(qkv, kc, vct, qkv, qkv, qkv, qkv, small_t)


def _merge_kernel(ys_ref, ws_ref, ya_ref, wa_ref, gs_ref, ga_ref, o_ref, wsb, wab):
    _cast_per_column_tile(((ws_ref, wsb), (wa_ref, wab)))
    s = jnp.dot(ys_ref[...], wsb[...], preferred_element_type=F32)
    a = jnp.dot(ya_ref[...], wab[...], preferred_element_type=F32)
    o_ref[...] = (gs_ref[...].astype(F32) * s + ga_ref[...].astype(F32) * a).astype(o_ref.dtype)


def _merge(y_ssm, w_s, y_attn, w_a, gm, tm=512, tn=512):
    T = y_ssm.shape[0]
    return pl.pallas_call(
        _merge_kernel,
        grid=(D_MODEL // tn, T // tm),
        in_specs=[pl.BlockSpec((tm, D_INNER), lambda n, m: (m, 0)),
                  pl.BlockSpec((D_INNER, tn), lambda n, m: (0, n)),
                  pl.BlockSpec((tm, ATTN_WIDTH), lambda n, m: (m, 0)),
                  pl.BlockSpec((ATTN_WIDTH, tn), lambda n, m: (0, n)),
                  pl.BlockSpec((tm, tn), lambda n, m: (m, n)),
                  pl.BlockSpec((tm, tn), lambda n, m: (m, D_MODEL // tn + n))],
        out_specs=pl.BlockSpec((tm, tn), lambda n, m: (m, n)),
        out_shape=jax.ShapeDtypeStruct((T, D_MODEL), BF16),
        scratch_shapes=[pltpu.VMEM((D_INNER, tn), BF16), pltpu.VMEM((ATTN_WIDTH, tn), BF16)],
        compiler_params=_cparams(("parallel", "arbitrary")),
        name="merge",
    )(y_ssm, w_s, y_attn, w_a, gm, gm)


def _mix_out_kernel(mg_ref, w_ref, x_ref, nw_ref, x1_ref, h_ref):
    x1 = x_ref[...] + jnp.dot(mg_ref[...], w_ref[...], preferred_element_type=F32)
    x1_ref[...] = x1
    h_ref[...] = _rms(x1, nw_ref[...]).astype(h_ref.dtype)


def _mix_out(merged, w, x2d, norm_w, tm=512):
    T = x2d.shape[0]
    return pl.pallas_call(
        _mix_out_kernel,
        grid=(T // tm,),
        in_specs=[pl.BlockSpec((tm, D_MODEL), lambda m: (m, 0)),
                  _resident((D_MODEL, D_MODEL)),
                  pl.BlockSpec((tm, D_MODEL), lambda m: (m, 0)),
                  pl.BlockSpec((1, D_MODEL), lambda m: (0, 0))],
        out_specs=[pl.BlockSpec((tm, D_MODEL), lambda m: (m, 0)),
                   pl.BlockSpec((tm, D_MODEL), lambda m: (m, 0))],
        out_shape=[jax.ShapeDtypeStruct((T, D_MODEL), F32),
                   jax.ShapeDtypeStruct((T, D_MODEL), BF16)],
        compiler_params=_cparams(("parallel",)),
        name="mix_out",
    )(merged, w, x2d, norm_w)


def _ffn_up_kernel(h_ref, wg_ref, wu_ref, cw_ref, cb_ref, o_ref, gbuf, wgb, wub, *, tiles_per_seq):
    m = pl.program_id(1)
    _cast_per_column_tile(((wg_ref, wgb), (wu_ref, wub)))

    @pl.when(m % tiles_per_seq == 0)
    def _():
        gbuf[0:TAIL, :] = jnp.zeros((TAIL, gbuf.shape[1]), F32)

    h = h_ref[...]
    g = jnp.dot(h, wgb[...], preferred_element_type=F32)
    u = jnp.dot(h, wub[...], preferred_element_type=F32)
    y, y_top = _causal_conv(g, gbuf, cw_ref, cb_ref)
    o_ref[...] = (_silu(y) * u).astype(o_ref.dtype)
    o_ref[0:FIX, :] = (_silu(y_top) * u[0:FIX, :]).astype(o_ref.dtype)


def _ffn_up(h, wg, wu, conv_w, conv_b, S, tm=1024, tn=512):
    T = h.shape[0]
    return pl.pallas_call(
        functools.partial(_ffn_up_kernel, tiles_per_seq=S // tm),
        grid=(D_FF // tn, T // tm),
        in_specs=[pl.BlockSpec((tm, D_MODEL), lambda n, m: (m, 0)),
                  pl.BlockSpec((D_MODEL, tn), lambda n, m: (0, n)),
                  pl.BlockSpec((D_MODEL, tn), lambda n, m: (0, n)),
                  pl.BlockSpec((FFN_CONV, tn), lambda n, m: (0, n)),
                  pl.BlockSpec((1, tn), lambda n, m: (0, n))],
        out_specs=pl.BlockSpec((tm, tn), lambda n, m: (m, n)),
        out_shape=jax.ShapeDtypeStruct((T, D_FF), BF16),
        scratch_shapes=[pltpu.VMEM((TAIL + FIX, tn), F32),
                        pltpu.VMEM((D_MODEL, tn), BF16), pltpu.VMEM((D_MODEL, tn), BF16)],
        compiler_params=_cparams(("parallel", "arbitrary")),
        name="ffn_up",
    )(h, wg, wu, conv_w, conv_b.reshape(1, D_FF))


def _ffn_down_kernel(a_ref, w_ref, x_ref, nw_ref, x2_ref, h_ref):
    x2 = x_ref[...] + jnp.dot(a_ref[...], w_ref[...], preferred_element_type=F32)
    x2_ref[...] = x2
    h_ref[...] = _rms(x2, nw_ref[...]).astype(h_ref.dtype)


def _ffn_down(act, w, x1, norm_w, tm=256):
    T = x1.shape[0]
    return pl.pallas_call(
        _ffn_down_kernel,
        grid=(T // tm,),
        in_specs=[pl.BlockSpec((tm, D_FF), lambda m: (m, 0)),
                  _resident((D_FF, D_MODEL)),
                  pl.BlockSpec((tm, D_MODEL), lambda m: (m, 0)),
                  _resident((1, D_MODEL))],
        out_specs=[pl.BlockSpec((tm, D_MODEL), lambda m: (m, 0)),
                   pl.BlockSpec((tm, D_MODEL), lambda m: (m, 0))],
        out_shape=[jax.ShapeDtypeStruct((T, D_MODEL), F32),
                   jax.ShapeDtypeStruct((T, D_MODEL), BF16)],
        compiler_params=_cparams(("parallel",)),
        name="ffn_down",
    )(act, w, x1, norm_w)


def _ple_kernel(h_ref, wg_ref, p_ref, wp_ref, x_ref, nw_ref, o_ref):
    gate = jax.nn.sigmoid(jnp.dot(h_ref[...], wg_ref[...], preferred_element_type=F32))
    emb = jnp.dot(p_ref[...].astype(BF16), wp_ref[...], preferred_element_type=F32)
    o_ref[...] = _rms(x_ref[...] + gate * emb, nw_ref[...])


def _ple(h, wg, p2d, wp, x2, norm_w, tm=512):
    T = x2.shape[0]
    return pl.pallas_call(
        _ple_kernel,
        grid=(T // tm,),
        in_specs=[pl.BlockSpec((tm, D_MODEL), lambda m: (m, 0)),
                  _resident((D_MODEL, D_MODEL)),
                  pl.BlockSpec((tm, PLE_DIM), lambda m: (m, 0)),
                  _resident((PLE_DIM, D_MODEL)),
                  pl.BlockSpec((tm, D_MODEL), lambda m: (m, 0)),
                  pl.BlockSpec((1, D_MODEL), lambda m: (0, 0))],
        out_specs=pl.BlockSpec((tm, D_MODEL), lambda m: (m, 0)),
        out_shape=jax.ShapeDtypeStruct((T, D_MODEL), F32),
        compiler_params=_cparams(("parallel",)),
        name="ple",
    )(h, wg, p2d, wp, x2, norm_w)


def _mixer_inputs(x2d, norm_w, w, ssm_conv_w, ssm_conv_b, B, S):
    wt = w.T
    h, small, small_t = _norm_small(x2d, norm_w.reshape(1, D_MODEL), _small_weight(wt), B, S)
    zs = _proj("silu", h, wt, 0, D_INNER, S)
    xbc = _proj("conv", h, wt, D_INNER, SSM_CONV_DIM, S,
                extra=(ssm_conv_w, ssm_conv_b.reshape(1, SSM_CONV_DIM)))
    qkv = _proj("rope", h, wt, NAT_Q, QKV_WIDTH, S, extra=_rope_tables(S))
    gm = _proj("sigmoid", h, wt, NAT_GM, 2 * D_MODEL, S)
    return small, small_t, zs, xbc, qkv, gm


def _layer(x2d, p2d, B, S, norm_mix_w, w_in, ssm_conv_w, ssm_conv_b, ssm_dt_bias, ssm_a_log, ssm_d,
           ssm_norm_w, cmp_pe_k, cmp_pe_v, cmp_wk1, cmp_wk2, cmp_wv1, cmp_wv2,
           w_ssm_branch, w_attn_branch, w_mix_out, norm_ffn_w, ffn_w_gate, ffn_w_up,
           ffn_conv_w, ffn_conv_b, ffn_w_down, ple_norm_w, ple_w_gate, ple_w_proj, out_norm_w):
    small, small_t, zs, xbc, qkv, gm = _mixer_inputs(x2d, norm_mix_w, w_in, ssm_conv_w, ssm_conv_b, B, S)
    y_ssm = _ssd(xbc, zs, small, small_t, ssm_dt_bias, ssm_a_log, ssm_d, ssm_norm_w, B, S)
    kc, vct = _compress(qkv, cmp_pe_k, cmp_pe_v, cmp_wk1, cmp_wv1, cmp_wk2, cmp_wv2, B, S)
    y_attn = _nsa(qkv, kc, vct, small_t, B, S)

    merged = _merge(y_ssm, w_ssm_branch, y_attn, w_attn_branch, gm)
    x1, h2 = _mix_out(merged, w_mix_out.astype(BF16), x2d, norm_ffn_w.reshape(1, D_MODEL))

    act = _ffn_up(h2, ffn_w_gate, ffn_w_up, ffn_conv_w, ffn_conv_b, S)
    x2, h3 = _ffn_down(act, ffn_w_down.astype(BF16), x1, ple_norm_w.reshape(1, D_MODEL))
    return _ple(h3, ple_w_gate.astype(BF16), p2d, ple_w_proj.astype(BF16), x2, out_norm_w.reshape(1, D_MODEL))


def kernel(x, p, norm_mix_w, w_in, ssm_conv_w, ssm_conv_b, ssm_dt_bias, ssm_a_log, ssm_d, ssm_norm_w, cmp_pe_k, cmp_pe_v, cmp_wk1, cmp_wk2, cmp_wv1, cmp_wv2, w_ssm_branch, w_attn_branch, w_mix_out, norm_ffn_w, ffn_w_gate, ffn_w_up, ffn_conv_w, ffn_conv_b, ffn_w_down, ple_norm_w, ple_w_gate, ple_w_proj, final_norm_w):
    B, S, D = x.shape
    depth = w_in.shape[0]
    assert depth == 1, "the final norm is fused into the (single) layer's last kernel"
    x2d = x.reshape(B * S, D)
    out = _layer(x2d, p[0].reshape(B * S, PLE_DIM), B, S, norm_mix_w[0], w_in[0], ssm_conv_w[0], ssm_conv_b[0],
                 ssm_dt_bias[0], ssm_a_log[0], ssm_d[0], ssm_norm_w[0], cmp_pe_k[0], cmp_pe_v[0],
                 cmp_wk1[0], cmp_wk2[0], cmp_wv1[0], cmp_wv2[0], w_ssm_branch[0], w_attn_branch[0],
                 w_mix_out[0], norm_ffn_w[0], ffn_w_gate[0], ffn_w_up[0], ffn_conv_w[0], ffn_conv_b[0],
                 ffn_w_down[0], ple_norm_w[0], ple_w_gate[0], ple_w_proj[0], final_norm_w)
    return out.reshape(B, S, D)
```

```python
import functools
import math

import jax
import jax.numpy as jnp
from jax import lax
from jax.experimental import pallas as pl
from jax.experimental.pallas import tpu as pltpu

F32 = jnp.float32
BF16 = jnp.bfloat16

D_MODEL = 2048
PLE_DIM = 256
EPS = 1e-6
D_INNER = 4096
SSM_HEAD_DIM = 64
SSM_HEADS = 64
SSM_GROUPS = 8
SSM_STATE = 128
SSM_CONV = 4
SSM_CHUNK = 128
SSM_BC = SSM_GROUPS * SSM_STATE
SSM_CONV_DIM = D_INNER + 2 * SSM_BC
ATTN_HEADS = 16
ATTN_HEAD_DIM = 128
ATTN_KV_GROUPS = 4
ATTN_REP = ATTN_HEADS // ATTN_KV_GROUPS
ATTN_WIDTH = ATTN_HEADS * ATTN_HEAD_DIM
KV_WIDTH = ATTN_KV_GROUPS * ATTN_HEAD_DIM
CMP_BLOCK = 32
CMP_STRIDE = 16
SEL_BLOCK = 64
N_SEL = 16
WINDOW = 512
ROPE_THETA = 500000.0
ROPE_DIM = ATTN_HEAD_DIM // 4
D_FF = 5632
FFN_CONV = 3
NEG_INF = -1e30
FORCE_SCORE = 1e4

LANES = 128
TAIL = 8
FIX = 16
VMEM_LIMIT = 56 * 1024 * 1024

NAT_DT = D_INNER + SSM_CONV_DIM
NAT_Q = NAT_DT + SSM_HEADS
NAT_GN = NAT_Q + ATTN_WIDTH + 6 * KV_WIDTH
NAT_GM = NAT_GN + 3 * ATTN_HEADS
QKV_WIDTH = ATTN_WIDTH + 6 * KV_WIDTH
WTILE = 1024
N_SMALL = 128
GATE_ROWS = 16
ONES_ROWS = 16
QKV_KC, QKV_VC, QKV_KS, QKV_VS, QKV_KW, QKV_VW = (
    (ATTN_WIDTH + i * KV_WIDTH) // ATTN_HEAD_DIM for i in range(6))


def _cparams(sem):
    return pltpu.CompilerParams(dimension_semantics=sem, vmem_limit_bytes=VMEM_LIMIT)


def _resident(shape):
    return pl.BlockSpec(shape, lambda *_: (0,) * len(shape), pipeline_mode=pl.Buffered(1))


def _rms(xf, w):
    return xf * lax.rsqrt(jnp.mean(xf * xf, axis=-1, keepdims=True) + EPS) * w


def _silu(v):
    return v * jax.nn.sigmoid(v)


def _causal_conv(x, top, w_ref, bias_ref):
    taps = w_ref.shape[0]
    bias = bias_ref[...]
    y = bias + w_ref[taps - 1:taps, :] * x
    for s in range(1, taps):
        y = y + w_ref[taps - 1 - s:taps - s, :] * pltpu.roll(x, s, 0)
    top[TAIL:TAIL + FIX, :] = x[0:FIX, :]
    y_top = bias
    for k in range(taps):
        off = TAIL - (taps - 1) + k
        y_top = y_top + w_ref[k:k + 1, :] * top[off:off + FIX, :]
    top[0:TAIL, :] = x[x.shape[0] - TAIL:, :]
    return y, y_top


def _cast_per_column_tile(pairs):
    @pl.when(pl.program_id(1) == 0)
    def _():
        for src, dst in pairs:
            dst[...] = src[...].astype(BF16)


def _wide_dot(h_ref, wt_ref, wb):
    _cast_per_column_tile(((wt_ref, wb),))
    return lax.dot_general(h_ref[...], wb[...], (((1,), (1,)), ((), ())), preferred_element_type=F32)


def _small_weight(wt):
    G, R = ATTN_KV_GROUPS, ATTN_REP
    K = wt.shape[1]
    w_dt = wt[NAT_DT:NAT_DT + SSM_HEADS]
    w_g = wt[NAT_GN:NAT_GN + 3 * ATTN_HEADS].reshape(3, G, R, K).transpose(1, 0, 2, 3).reshape(G, 3 * R, K)
    w_g = jnp.pad(w_g, ((0, 0), (0, GATE_ROWS - 3 * R), (0, 0))).reshape(G * GATE_ROWS, K)
    return jnp.concatenate([w_dt, w_g], axis=0)


def _norm_small_kernel(x_ref, nw_ref, whi_ref, wlo_ref, dtb_ref, dtbt_ref, alog_ref, alogt_ref,
                       h_ref, smallt_ref, acum_ref, dtt_ref, acumt_ref):
    L, H = SSM_CHUNK, SSM_HEADS
    nt = (((1,), (1,)), ((), ()))
    h = _rms(x_ref[...], nw_ref[...])
    h_hi = h.astype(BF16)
    h_lo = (h - h_hi.astype(F32)).astype(BF16)
    h_ref[...] = h_hi
    small = (lax.dot_general(h_hi, whi_ref[...], nt, preferred_element_type=F32)
             + lax.dot_general(h_hi, wlo_ref[...], nt, preferred_element_type=F32)
             + lax.dot_general(h_lo, whi_ref[...], nt, preferred_element_type=F32))
    small_t = small.T
    smallt_ref[0] = small_t

    ri = lax.broadcasted_iota(jnp.int32, (L, L), 0)
    ci = lax.broadcasted_iota(jnp.int32, (L, L), 1)
    lower = jnp.where(ci <= ri, 1.0, 0.0)
    upper = jnp.where(ri <= ci, 1.0, 0.0)
    hp = lax.Precision.HIGHEST
    a = -jnp.exp(alog_ref[...])
    at = -jnp.exp(alogt_ref[...])
    per = H // SSM_GROUPS
    for j in range(small.shape[0] // L):
        rows = slice(j * L, (j + 1) * L)
        dt = jax.nn.softplus(small[rows, 0:H] + dtb_ref[...])
        dtt = jax.nn.softplus(small_t[0:H, rows] + dtbt_ref[...])
        acum = jnp.dot(lower, dt * a, preferred_element_type=F32, precision=hp)
        for g in range(SSM_GROUPS):
            acum_ref[0, g, rows, :] = acum[:, g * per:(g + 1) * per]
        dtt_ref[0, :, rows] = dtt
        acumt_ref[0, :, rows] = jnp.dot(dtt * at, upper, preferred_element_type=F32, precision=hp)


def _norm_small(x2d, norm_w, w_small, dt_bias, a_log, B, S, tm=512):
    T = x2d.shape[0]
    H, G = SSM_HEADS, SSM_GROUPS
    per = S // tm
    w_hi = w_small.astype(BF16)
    w_lo = (w_small - w_hi.astype(F32)).astype(BF16)
    const = lambda shape: pl.BlockSpec(shape, lambda i: (0,) * len(shape))
    head_major = pl.BlockSpec((1, H, tm), lambda i: (i // per, 0, i % per))
    return pl.pallas_call(
        _norm_small_kernel,
        grid=(T // tm,),
        in_specs=[pl.BlockSpec((tm, D_MODEL), lambda i: (i, 0)),
                  const((1, D_MODEL)), const((N_SMALL, D_MODEL)), const((N_SMALL, D_MODEL)),
                  const((1, H)), const((H, 1)), const((1, H)), const((H, 1))],
        out_specs=[pl.BlockSpec((tm, D_MODEL), lambda i: (i, 0)),
                   pl.BlockSpec((1, N_SMALL, tm), lambda i: (i // per, 0, i % per)),
                   pl.BlockSpec((1, G, tm, H // G), lambda i: (i // per, 0, i % per, 0)),
                   head_major, head_major],
        out_shape=[jax.ShapeDtypeStruct((T, D_MODEL), BF16),
                   jax.ShapeDtypeStruct((B, N_SMALL, S), F32),
                   jax.ShapeDtypeStruct((B, G, S, H // G), F32),
                   jax.ShapeDtypeStruct((B, H, S), F32),
                   jax.ShapeDtypeStruct((B, H, S), F32)],
        compiler_params=_cparams(("parallel",)),
        name="norm_small",
    )(x2d, norm_w, w_hi, w_lo, dt_bias.reshape(1, H), dt_bias.reshape(H, 1), a_log.reshape(1, H), a_log.reshape(H, 1))


def _proj_act_kernel(h_ref, wt_ref, o_ref, wb, *, act):
    o_ref[...] = act(_wide_dot(h_ref, wt_ref, wb)).astype(o_ref.dtype)


def _proj_conv_kernel(h_ref, wt_ref, cw_ref, cb_ref, o_ref, wb, buf, *, tiles_per_seq):
    m = pl.program_id(1)

    @pl.when(m % tiles_per_seq == 0)
    def _():
        buf[0:TAIL, :] = jnp.zeros((TAIL, buf.shape[1]), F32)

    y, y_top = _causal_conv(_wide_dot(h_ref, wt_ref, wb), buf, cw_ref, cb_ref)
    o_ref[...] = _silu(y).astype(o_ref.dtype)
    o_ref[0:FIX, :] = _silu(y_top).astype(o_ref.dtype)


def _proj_rope_kernel(h_ref, wt_ref, ca_ref, cb_ref, cc_ref, o_ref, wb, *, scale):
    n = pl.program_id(0)
    d = ATTN_HEAD_DIM
    half = ROPE_DIM // 2
    is_q = n < ATTN_WIDTH // WTILE
    acc = _wide_dot(h_ref, wt_ref, wb)
    ca, cb, cc = ca_ref[...], cb_ref[...], cc_ref[...]
    q_scale = jnp.where(is_q, scale, 1.0).astype(F32)
    for hd in range(WTILE // d):
        xh = acc[:, hd * d:(hd + 1) * d]
        up = pltpu.roll(xh, d - half, 1)
        dn = pltpu.roll(xh, half, 1)
        roped = (xh * ca + up * cb + dn * cc) * q_scale
        if hd >= KV_WIDTH // d:
            roped = jnp.where(is_q, roped, xh)
        o_ref[:, hd * d:(hd + 1) * d] = roped.astype(o_ref.dtype)


def _rope_tables(S):
    half = ROPE_DIM // 2
    pos = jnp.arange(S)
    inv = ROPE_THETA ** (-(jnp.arange(half, dtype=F32) * 2.0 / ROPE_DIM))
    ang = pos.astype(F32)[:, None] * inv[None, :]
    cos, sin = jnp.cos(ang), jnp.sin(ang)
    rest = ATTN_HEAD_DIM - ROPE_DIM
    ca = jnp.concatenate([cos, cos, jnp.ones((S, rest), F32)], axis=1)
    cb = jnp.concatenate([-sin, jnp.zeros((S, ATTN_HEAD_DIM - half), F32)], axis=1)
    cc = jnp.concatenate([jnp.zeros((S, half), F32), sin, jnp.zeros((S, rest), F32)], axis=1)
    return ca, cb, cc


def _proj(kind, h, wt, row0, width, S, extra=(), tm=1024):
    T, K = h.shape
    tn = WTILE
    per = S // tm
    in_specs = [pl.BlockSpec((tm, K), lambda n, m: (m, 0)),
                pl.BlockSpec((pl.Element(tn), pl.Element(K)),
                             lambda n, m: (pl.multiple_of(row0 + n * tn, TAIL), 0))]
    scratch = [pltpu.VMEM((tn, K), BF16)]
    if kind == "silu":
        body = functools.partial(_proj_act_kernel, act=_silu)
    elif kind == "sigmoid":
        body = functools.partial(_proj_act_kernel, act=jax.nn.sigmoid)
    elif kind == "conv":
        body = functools.partial(_proj_conv_kernel, tiles_per_seq=per)
        taps = extra[0].shape[0]
        in_specs += [pl.BlockSpec((taps, tn), lambda n, m: (0, n)),
                     pl.BlockSpec((1, tn), lambda n, m: (0, n))]
        scratch += [pltpu.VMEM((TAIL + FIX, tn), F32)]
    else:
        body = functools.partial(_proj_rope_kernel, scale=math.log2(math.e) / math.sqrt(ATTN_HEAD_DIM))
        tab = pl.BlockSpec((tm, ATTN_HEAD_DIM), lambda n, m: (m % per, 0))
        in_specs += [tab, tab, tab]
    return pl.pallas_call(
        body,
        grid=(width // tn, T // tm),
        in_specs=in_specs,
        out_specs=pl.BlockSpec((tm, tn), lambda n, m: (m, n)),
        out_shape=jax.ShapeDtypeStruct((T, width), BF16),
        scratch_shapes=scratch,
        compiler_params=_cparams(("parallel", "arbitrary")),
        name="proj_" + kind,
    )(h, wt, *extra)


def _ssd_kernel(x_ref, b_ref, c_ref, z_ref, acum_ref, acumt_ref, dtt_ref, dsk_ref, nw_ref, y_ref, h_scr):
    @pl.when(pl.program_id(2) == 0)
    def _():
        h_scr[...] = jnp.zeros_like(h_scr)

    def chunk(j, carry):
        _ssd_chunk(pl.ds(pl.multiple_of(j * SSM_CHUNK, SSM_CHUNK), SSM_CHUNK),
                   x_ref, b_ref, c_ref, z_ref, acum_ref, acumt_ref, dtt_ref, dsk_ref, nw_ref, y_ref, h_scr)
        return carry

    lax.fori_loop(0, x_ref.shape[0] // SSM_CHUNK, chunk, 0)


def _ssd_chunk(rows, x_ref, b_ref, c_ref, z_ref, acum_ref, acumt_ref, dtt_ref, dsk_ref, nw_ref, y_ref, h_scr):
    L = SSM_CHUNK
    P = SSM_HEAD_DIM
    xb = x_ref[rows, :]
    bmb = b_ref[rows, :]
    cmb = c_ref[rows, :]
    n_heads = SSM_HEADS // SSM_GROUPS
    acum = acum_ref[0, 0, rows, :]
    acumT = acumt_ref[0, :, rows]
    dtT = dtt_ref[0, :, rows]
    ri = lax.broadcasted_iota(jnp.int32, (L, L), 0)
    ci = lax.broadcasted_iota(jnp.int32, (L, L), 1)
    tril = ci <= ri

    cb = lax.dot_general(cmb, bmb, (((1,), (1,)), ((), ())), preferred_element_type=F32)
    yoff = lax.dot_general(cmb, h_scr[...].astype(BF16), (((1,), (1,)), ((), ())),
                           preferred_element_type=F32)

    lane = lax.broadcasted_iota(jnp.int32, (L, 2 * P), 1)
    lo = lane < P

    def pair(v0, v1):
        return jnp.where(lo, v0, v1)

    dsk = dsk_ref[...]
    ys = []
    for pr in range(n_heads // 2):
        sl = slice(pr * 2 * P, (pr + 1) * 2 * P)
        xp = xb[:, sl]
        yd, ea = [], []
        for r in (2 * pr, 2 * pr + 1):
            acol = jnp.broadcast_to(acum[:, r:r + 1], (L, L))
            dec = jnp.exp(jnp.where(tril, acol - acumT[r:r + 1, :], -jnp.inf))
            yd.append(jnp.dot((cb * dec * dtT[r:r + 1, :]).astype(BF16), xp, preferred_element_type=F32))
            ea.append(jnp.exp(acol))
        ys.append(pair(yd[0], yd[1]) + yoff[:, sl] * pair(ea[0], ea[1]) + dsk[:, sl] * xp.astype(F32))

    a_last = acumT[:, L - 1:L]
    wT = dtT * jnp.exp(a_last - acumT)
    cdT = jnp.exp(a_last)
    xT = xb.astype(F32).T
    xds = jnp.concatenate([xT[r * P:(r + 1) * P, :] * wT[r:r + 1, :] for r in range(n_heads)], axis=0)
    cd = jnp.concatenate([jnp.broadcast_to(cdT[r:r + 1, :], (P, h_scr.shape[1])) for r in range(n_heads)], axis=0)
    st = jnp.dot(xds.astype(BF16), bmb, preferred_element_type=F32)
    h_scr[...] = h_scr[...] * cd + st

    y = jnp.concatenate(ys, axis=1) * z_ref[rows, :].astype(F32)
    y_ref[rows, :] = _rms(y, nw_ref[...]).astype(y_ref.dtype)


def _ssd(xbc, zs, acum_g, dt_t, acum_t, d_skip, norm_w, B, S, rows=1024):
    T = B * S
    L, G, N = rows, SSM_GROUPS, SSM_STATE
    R = SSM_HEADS // G
    gw = D_INNER // G
    nc = S // L
    dsk = jnp.repeat(d_skip, SSM_HEAD_DIM).reshape(1, D_INNER)
    nw = norm_w.reshape(1, D_INNER)

    row = lambda b, g, c: b * nc + c
    in_specs = [
        pl.BlockSpec((L, gw), lambda b, g, c: (row(b, g, c), g)),
        pl.BlockSpec((L, N), lambda b, g, c: (row(b, g, c), D_INNER // N + g)),
        pl.BlockSpec((L, N), lambda b, g, c: (row(b, g, c), (D_INNER + SSM_BC) // N + g)),
        pl.BlockSpec((L, gw), lambda b, g, c: (row(b, g, c), g)),
        pl.BlockSpec((1, 1, L, R), lambda b, g, c: (b, g, c, 0)),
        pl.BlockSpec((1, R, L), lambda b, g, c: (b, g, c)),
        pl.BlockSpec((1, R, L), lambda b, g, c: (b, g, c)),
        pl.BlockSpec((1, gw), lambda b, g, c: (0, g)),
        pl.BlockSpec((1, gw), lambda b, g, c: (0, g)),
    ]
    return pl.pallas_call(
        _ssd_kernel,
        grid=(B, G, nc),
        in_specs=in_specs,
        out_specs=pl.BlockSpec((L, gw), lambda b, g, c: (row(b, g, c), g)),
        out_shape=jax.ShapeDtypeStruct((T, D_INNER), BF16),
        scratch_shapes=[pltpu.VMEM((gw, N), F32)],
        compiler_params=_cparams(("parallel", "parallel", "arbitrary")),
        name="ssd",
    )(xbc, xbc, xbc, zs, acum_g, acum_t, dt_t, dsk, nw)


def _compress_kernel(k_ref, v_ref, pek_ref, pev_ref, wk1_ref, wv1_ref, wk2_ref, wv2_ref, kc_ref, vct_ref, xf):
    d = ATTN_HEAD_DIM
    nsub = xf.shape[0] // CMP_STRIDE

    def mlp(src_ref, pe_ref, w1_ref, w2_ref):
        xf[...] = src_ref[...].astype(F32)
        u = jnp.zeros((nsub, d), F32)
        v = jnp.zeros((nsub, d), F32)
        for l in range(CMP_STRIDE):
            tok = xf[pl.ds(l, nsub, stride=CMP_STRIDE), :]
            l2 = CMP_STRIDE + l
            u = u + jnp.dot((tok + pe_ref[l:l + 1, :]).astype(BF16), w1_ref[l * d:(l + 1) * d, :],
                            preferred_element_type=F32)
            v = v + jnp.dot((tok + pe_ref[l2:l2 + 1, :]).astype(BF16), w1_ref[l2 * d:(l2 + 1) * d, :],
                            preferred_element_type=F32)
        pre = u + pltpu.roll(v, nsub - 1, 0)
        return jnp.dot(_silu(pre).astype(BF16), w2_ref[...], preferred_element_type=F32)

    kc_ref[0, 0] = mlp(k_ref, pek_ref, wk1_ref, wk2_ref).astype(kc_ref.dtype)
    vct_ref[0, 0] = mlp(v_ref, pev_ref, wv1_ref, wv2_ref).T.astype(vct_ref.dtype)


def _compress(qkv, pe_k, pe_v, wk1, wv1, wk2, wv2, B, S):
    G, d = ATTN_KV_GROUPS, ATTN_HEAD_DIM
    nsub = S // CMP_STRIDE
    full = lambda shape: pl.BlockSpec(shape, lambda b, g: (0, 0))
    out = pl.BlockSpec((1, 1, nsub, d), lambda b, g: (b, g, 0, 0))
    return pl.pallas_call(
        _compress_kernel,
        grid=(B, G),
        in_specs=[pl.BlockSpec((S, d), lambda b, g: (b, QKV_KC + g)),
                  pl.BlockSpec((S, d), lambda b, g: (b, QKV_VC + g)),
                  full((CMP_BLOCK, d)), full((CMP_BLOCK, d)),
                  full((CMP_BLOCK * d, d)), full((CMP_BLOCK * d, d)),
                  full((d, d)), full((d, d))],
        out_specs=[out, out],
        out_shape=[jax.ShapeDtypeStruct((B, G, nsub, d), BF16),
                   jax.ShapeDtypeStruct((B, G, d, nsub), BF16)],
        scratch_shapes=[pltpu.VMEM((S, d), F32)],
        compiler_params=_cparams(("parallel", "parallel")),
        name="compress",
    )(qkv, qkv, pe_k, pe_v, wk1.astype(BF16), wv1.astype(BF16), wk2.astype(BF16), wv2.astype(BF16))


def _staged(n, scores, probs, values):
    for step in range(n + 2):
        if step < n:
            scores(step)
        if 1 <= step <= n:
            probs(step - 1)
        if step >= 2:
            values(step - 2)


def _nsa_kernel(q_ref, kc_ref, vct_ref, ks_ref, vs_ref, kw_ref, vw_ref, gt_ref, o_ref,
                vst, vwt, qt, m_scr, acc_scr, out_scr, s_scr, p_scr, *, tq, tks, n_cmp, n_blk):
    i = pl.program_id(2)
    d = ATTN_HEAD_DIM
    R = ATTN_REP
    heads = [slice(r * tq, (r + 1) * tq) for r in range(R)]

    def transposed(v):
        return v.astype(F32).T.astype(BF16)

    @pl.when(i == 0)
    def _():
        for c in range(vst.shape[0]):
            vst[c, 0:d, :] = transposed(vs_ref[c * tks:(c + 1) * tks, :])
            vst[c, d:, :] = jnp.ones((ONES_ROWS, tks), BF16)
        vwt[0:d, :] = transposed(vw_ref[...])
        vwt[d:, :] = jnp.ones((ONES_ROWS, vwt.shape[1]), BF16)

    for r in range(R):
        qt[:, heads[r]] = transposed(q_ref[:, r * d:(r + 1) * d])
    gate = jax.nn.sigmoid(gt_ref[0])

    nk = kc_ref.shape[2]
    kc = kc_ref[0, 0]
    vct = vct_ref[0, 0]
    n_idx = lax.broadcasted_iota(jnp.int32, (nk, tq), 0)
    t_cmp = i * tq + lax.broadcasted_iota(jnp.int32, (nk, tq), 1)
    cmask = (n_idx * CMP_STRIDE + CMP_BLOCK - 1 <= t_cmp) & (n_idx < n_cmp)
    psum = jnp.zeros((nk, tq), F32)
    for r in range(R):
        s = jnp.where(cmask, jnp.dot(kc, qt[:, heads[r]], preferred_element_type=F32), NEG_INF)
        e = jnp.exp2(s - jnp.max(s, axis=0, keepdims=True))
        p = jnp.where(cmask, e * (1.0 / jnp.sum(e, axis=0, keepdims=True)), 0.0)
        out_scr[:, heads[r]] = gate[r:r + 1, :] * jnp.dot(vct, p.astype(BF16), preferred_element_type=F32)
        psum = psum + p
    bi = lax.broadcasted_iota(jnp.int32, (n_blk, nk), 0)
    ni = lax.broadcasted_iota(jnp.int32, (n_blk, nk), 1)
    ovl = ((ni * CMP_STRIDE < bi * SEL_BLOCK + SEL_BLOCK) & (ni * CMP_STRIDE + CMP_BLOCK > bi * SEL_BLOCK)
           & (ni < n_cmp))
    imp = jnp.dot(jnp.where(ovl, 1.0, 0.0), psum, preferred_element_type=F32, precision=lax.Precision.HIGHEST)
    blk = lax.broadcasted_iota(jnp.int32, (n_blk, tq), 0)
    cur = (i * tq + lax.broadcasted_iota(jnp.int32, (n_blk, tq), 1)) // SEL_BLOCK
    forced = (blk == 0) | (blk == cur) | (blk == cur - 1)
    score = jnp.where(forced, FORCE_SCORE, jnp.where(blk <= cur, imp, NEG_INF))
    rank = jnp.zeros((n_blk, tq), F32)
    for k in range(n_blk):
        sk = score[k:k + 1, :]
        rank = rank + jnp.where((sk > score) | ((sk == score) & (k < blk)), 1.0, 0.0)
    sel = jnp.where(rank < N_SEL, 1.0, 0.0).astype(BF16)

    m_scr[...] = jnp.full(m_scr.shape, NEG_INF, F32)
    acc_scr[...] = jnp.zeros_like(acc_scr)

    def sel_body(c, carry):
        k0 = pl.multiple_of(c * tks, tks)
        key = k0 + lax.broadcasted_iota(jnp.int32, (tks, tq), 0)
        t = i * tq + lax.broadcasted_iota(jnp.int32, (tks, tq), 1)
        in_blk = (lax.broadcasted_iota(jnp.int32, (tks, n_blk), 1)
                  == (k0 + lax.broadcasted_iota(jnp.int32, (tks, n_blk), 0)) // SEL_BLOCK)
        picked = jnp.dot(jnp.where(in_blk, 1.0, 0.0).astype(BF16), sel, preferred_element_type=F32)
        bias = jnp.where((picked > 0.5) & (key <= t), 0.0, NEG_INF)
        k_chunk = ks_ref[pl.ds(k0, tks), :]
        vt_chunk = vst[c]
        m_new, alpha = [None] * R, [None] * R

        def scores(r):
            s = jnp.dot(k_chunk, qt[:, heads[r]], preferred_element_type=F32) + bias
            s_scr[r, 0:tks, :] = s
            m_prev = m_scr[r:r + 1, :]
            m_new[r] = jnp.maximum(m_prev, jnp.max(s, axis=0, keepdims=True))
            alpha[r] = jnp.exp2(m_prev - m_new[r])
            m_scr[r:r + 1, :] = m_new[r]

        def probs(r):
            p_scr[r, 0:tks, :] = jnp.exp2(s_scr[r, 0:tks, :] - m_new[r]).astype(BF16)

        def values(r):
            acc_scr[:, heads[r]] = (alpha[r] * acc_scr[:, heads[r]]
                                    + jnp.dot(vt_chunk, p_scr[r, 0:tks, :], preferred_element_type=F32))

        _staged(R, scores, probs, values)
        return carry

    lax.fori_loop(0, (i * tq + tq - 1) // tks + 1, sel_body, 0)
    for r in range(R):
        o = acc_scr[0:d, heads[r]] * (1.0 / acc_scr[d:d + 1, heads[r]])
        out_scr[:, heads[r]] += gate[R + r:R + r + 1, :] * o

    span = WINDOW + tq
    k0 = pl.multiple_of(jnp.maximum(i * tq - WINDOW, 0), tq)
    dist = (i * tq + lax.broadcasted_iota(jnp.int32, (span, tq), 1)
            - (k0 + lax.broadcasted_iota(jnp.int32, (span, tq), 0)))
    bias = jnp.where((dist >= 0) & (dist < WINDOW), 0.0, NEG_INF)
    k_win = kw_ref[pl.ds(k0, span), :]
    vt_win = vwt[:, pl.ds(k0, span)]
    mx = [None] * R

    def scores(r):
        s = jnp.dot(k_win, qt[:, heads[r]], preferred_element_type=F32) + bias
        s_scr[r] = s
        mx[r] = jnp.max(s, axis=0, keepdims=True)

    def probs(r):
        p_scr[r] = jnp.exp2(s_scr[r] - mx[r]).astype(BF16)

    def values(r):
        o = jnp.dot(vt_win, p_scr[r], preferred_element_type=F32)
        out_scr[:, heads[r]] += gate[2 * R + r:2 * R + r + 1, :] * (o[0:d, :] * (1.0 / o[d:d + 1, :]))

    _staged(R, scores, probs, values)

    for r in range(R):
        o_ref[:, r * d:(r + 1) * d] = out_scr[:, heads[r]].T.astype(o_ref.dtype)


def _nsa(qkv, kc, vct, small_t, B, S, tq=256, tks=512):
    assert WINDOW % tq == 0 and S >= WINDOW + tq
    T = B * S
    G, d, R = ATTN_KV_GROUPS, ATTN_HEAD_DIM, ATTN_REP
    nq = S // tq
    nk = kc.shape[2]
    n_cmp = S // CMP_STRIDE - CMP_BLOCK // CMP_STRIDE + 1
    n_blk = S // SEL_BLOCK
    seq = lambda c0: pl.BlockSpec((S, d), lambda b, g, i: (b, c0 + g))
    cmp_spec = pl.BlockSpec((1, 1, nk, d), lambda b, g, i: (b, g, 0, 0))
    gate0 = SSM_HEADS // GATE_ROWS
    return pl.pallas_call(
        functools.partial(_nsa_kernel, tq=tq, tks=tks, n_cmp=n_cmp, n_blk=n_blk),
        grid=(B, G, nq),
        in_specs=[pl.BlockSpec((tq, KV_WIDTH), lambda b, g, i: (b * nq + i, g)),
                  cmp_spec, cmp_spec,
                  seq(QKV_KS), seq(QKV_VS), seq(QKV_KW), seq(QKV_VW),
                  pl.BlockSpec((1, GATE_ROWS, tq), lambda b, g, i: (b, gate0 + g, i))],
        out_specs=pl.BlockSpec((tq, KV_WIDTH), lambda b, g, i: (b * nq + i, g)),
        out_shape=jax.ShapeDtypeStruct((T, ATTN_WIDTH), BF16),
        scratch_shapes=[pltpu.VMEM((S // tks, d + ONES_ROWS, tks), BF16),
                        pltpu.VMEM((d + ONES_ROWS, S), BF16),
                        pltpu.VMEM((d, R * tq), BF16),
                        pltpu.VMEM((8, tq), F32),
                        pltpu.VMEM((d + ONES_ROWS, R * tq), F32),
                        pltpu.VMEM((d, R * tq), F32),
                        pltpu.VMEM((R, WINDOW + tq, tq), F32),
                        pltpu.VMEM((R, WINDOW + tq, tq), BF16)],
        compiler_params=_cparams(("parallel", "parallel", "arbitrary")),
        name="nsa",
    )(qkv, kc, vct, qkv, qkv, qkv, qkv, small_t)


def _merge_kernel(ys_ref, ws_ref, ya_ref, wa_ref, gs_ref, ga_ref, o_ref, wsb, wab):
    _cast_per_column_tile(((ws_ref, wsb), (wa_ref, wab)))
    s = jnp.dot(ys_ref[...], wsb[...], preferred_element_type=F32)
    a = jnp.dot(ya_ref[...], wab[...], preferred_element_type=F32)
    o_ref[...] = (gs_ref[...].astype(F32) * s + ga_ref[...].astype(F32) * a).astype(o_ref.dtype)


def _merge(y_ssm, w_s, y_attn, w_a, gm, tm=512, tn=512):
    T = y_ssm.shape[0]
    return pl.pallas_call(
        _merge_kernel,
        grid=(D_MODEL // tn, T // tm),
        in_specs=[pl.BlockSpec((tm, D_INNER), lambda n, m: (m, 0)),
                  pl.BlockSpec((D_INNER, tn), lambda n, m: (0, n)),
                  pl.BlockSpec((tm, ATTN_WIDTH), lambda n, m: (m, 0)),
                  pl.BlockSpec((ATTN_WIDTH, tn), lambda n, m: (0, n)),
                  pl.BlockSpec((tm, tn), lambda n, m: (m, n)),
                  pl.BlockSpec((tm, tn), lambda n, m: (m, D_MODEL // tn + n))],
        out_specs=pl.BlockSpec((tm, tn), lambda n, m: (m, n)),
        out_shape=jax.ShapeDtypeStruct((T, D_MODEL), BF16),
        scratch_shapes=[pltpu.VMEM((D_INNER, tn), BF16), pltpu.VMEM((ATTN_WIDTH, tn), BF16)],
        compiler_params=_cparams(("parallel", "arbitrary")),
        name="merge",
    )(y_ssm, w_s, y_attn, w_a, gm, gm)


def _mix_out_kernel(mg_ref, w_ref, x_ref, nw_ref, x1_ref, h_ref):
    x1 = x_ref[...] + jnp.dot(mg_ref[...], w_ref[...], preferred_element_type=F32)
    x1_ref[...] = x1
    h_ref[...] = _rms(x1, nw_ref[...]).astype(h_ref.dtype)


def _mix_out(merged, w, x2d, norm_w, tm=512):
    T = x2d.shape[0]
    return pl.pallas_call(
        _mix_out_kernel,
        grid=(T // tm,),
        in_specs=[pl.BlockSpec((tm, D_MODEL), lambda m: (m, 0)),
                  _resident((D_MODEL, D_MODEL)),
                  pl.BlockSpec((tm, D_MODEL), lambda m: (m, 0)),
                  pl.BlockSpec((1, D_MODEL), lambda m: (0, 0))],
        out_specs=[pl.BlockSpec((tm, D_MODEL), lambda m: (m, 0)),
                   pl.BlockSpec((tm, D_MODEL), lambda m: (m, 0))],
        out_shape=[jax.ShapeDtypeStruct((T, D_MODEL), F32),
                   jax.ShapeDtypeStruct((T, D_MODEL), BF16)],
        compiler_params=_cparams(("parallel",)),
        name="mix_out",
    )(merged, w, x2d, norm_w)


def _ffn_up_kernel(h_ref, wg_ref, wu_ref, cw_ref, cb_ref, o_ref, gbuf, wgb, wub, *, tiles_per_seq):
    m = pl.program_id(1)
    _cast_per_column_tile(((wg_ref, wgb), (wu_ref, wub)))

    @pl.when(m % tiles_per_seq == 0)
    def _():
        gbuf[0:TAIL, :] = jnp.zeros((TAIL, gbuf.shape[1]), F32)

    h = h_ref[...]
    g = jnp.dot(h, wgb[...], preferred_element_type=F32)
    u = jnp.dot(h, wub[...], preferred_element_type=F32)
    y, y_top = _causal_conv(g, gbuf, cw_ref, cb_ref)
    o_ref[...] = (_silu(y) * u).astype(o_ref.dtype)
    o_ref[0:FIX, :] = (_silu(y_top) * u[0:FIX, :]).astype(o_ref.dtype)


def _ffn_up(h, wg, wu, conv_w, conv_b, S, tm=1024, tn=512):
    T = h.shape[0]
    return pl.pallas_call(
        functools.partial(_ffn_up_kernel, tiles_per_seq=S // tm),
        grid=(D_FF // tn, T // tm),
        in_specs=[pl.BlockSpec((tm, D_MODEL), lambda n, m: (m, 0)),
                  pl.BlockSpec((D_MODEL, tn), lambda n, m: (0, n)),
                  pl.BlockSpec((D_MODEL, tn), lambda n, m: (0, n)),
                  pl.BlockSpec((FFN_CONV, tn), lambda n, m: (0, n)),
                  pl.BlockSpec((1, tn), lambda n, m: (0, n))],
        out_specs=pl.BlockSpec((tm, tn), lambda n, m: (m, n)),
        out_shape=jax.ShapeDtypeStruct((T, D_FF), BF16),
        scratch_shapes=[pltpu.VMEM((TAIL + FIX, tn), F32),
                        pltpu.VMEM((D_MODEL, tn), BF16), pltpu.VMEM((D_MODEL, tn), BF16)],
        compiler_params=_cparams(("parallel", "arbitrary")),
        name="ffn_up",
    )(h, wg, wu, conv_w, conv_b.reshape(1, D_FF))


def _ffn_down_kernel(a_ref, w_ref, x_ref, nw_ref, x2_ref, h_ref):
    x2 = x_ref[...] + jnp.dot(a_ref[...], w_ref[...], preferred_element_type=F32)
    x2_ref[...] = x2
    h_ref[...] = _rms(x2, nw_ref[...]).astype(h_ref.dtype)


def _ffn_down(act, w, x1, norm_w, tm=256):
    T = x1.shape[0]
    return pl.pallas_call(
        _ffn_down_kernel,
        grid=(T // tm,),
        in_specs=[pl.BlockSpec((tm, D_FF), lambda m: (m, 0)),
                  _resident((D_FF, D_MODEL)),
                  pl.BlockSpec((tm, D_MODEL), lambda m: (m, 0)),
                  _resident((1, D_MODEL))],
        out_specs=[pl.BlockSpec((tm, D_MODEL), lambda m: (m, 0)),
                   pl.BlockSpec((tm, D_MODEL), lambda m: (m, 0))],
        out_shape=[jax.ShapeDtypeStruct((T, D_MODEL), F32),
                   jax.ShapeDtypeStruct((T, D_MODEL), BF16)],
        compiler_params=_cparams(("parallel",)),
        name="ffn_down",
    )(act, w, x1, norm_w)


def _ple_kernel(h_ref, wg_ref, p_ref, wp_ref, x_ref, nw_ref, o_ref):
    gate = jax.nn.sigmoid(jnp.dot(h_ref[...], wg_ref[...], preferred_element_type=F32))
    emb = jnp.dot(p_ref[...].astype(BF16), wp_ref[...], preferred_element_type=F32)
    o_ref[...] = _rms(x_ref[...] + gate * emb, nw_ref[...])


def _ple(h, wg, p2d, wp, x2, norm_w, tm=512):
    T = x2.shape[0]
    return pl.pallas_call(
        _ple_kernel,
        grid=(T // tm,),
        in_specs=[pl.BlockSpec((tm, D_MODEL), lambda m: (m, 0)),
                  _resident((D_MODEL, D_MODEL)),
                  pl.BlockSpec((tm, PLE_DIM), lambda m: (m, 0)),
                  _resident((PLE_DIM, D_MODEL)),
                  pl.BlockSpec((tm, D_MODEL), lambda m: (m, 0)),
                  pl.BlockSpec((1, D_MODEL), lambda m: (0, 0))],
        out_specs=pl.BlockSpec((tm, D_MODEL), lambda m: (m, 0)),
        out_shape=jax.ShapeDtypeStruct((T, D_MODEL), F32),
        compiler_params=_cparams(("parallel",)),
        name="ple",
    )(h, wg, p2d, wp, x2, norm_w)


def _mixer_inputs(x2d, norm_w, w, ssm_conv_w, ssm_conv_b, dt_bias, a_log, B, S):
    wt = w.T
    h, small_t, acum_g, dt_t, acum_t = _norm_small(x2d, norm_w.reshape(1, D_MODEL), _small_weight(wt),
                                                   dt_bias, a_log, B, S)
    zs = _proj("silu", h, wt, 0, D_INNER, S)
    xbc = _proj("conv", h, wt, D_INNER, SSM_CONV_DIM, S,
                extra=(ssm_conv_w, ssm_conv_b.reshape(1, SSM_CONV_DIM)))
    qkv = _proj("rope", h, wt, NAT_Q, QKV_WIDTH, S, extra=_rope_tables(S))
    gm = _proj("sigmoid", h, wt, NAT_GM, 2 * D_MODEL, S)
    return small_t, (acum_g, dt_t, acum_t), zs, xbc, qkv, gm


def _layer(x2d, p2d, B, S, norm_mix_w, w_in, ssm_conv_w, ssm_conv_b, ssm_dt_bias, ssm_a_log, ssm_d,
           ssm_norm_w, cmp_pe_k, cmp_pe_v, cmp_wk1, cmp_wk2, cmp_wv1, cmp_wv2,
           w_ssm_branch, w_attn_branch, w_mix_out, norm_ffn_w, ffn_w_gate, ffn_w_up,
           ffn_conv_w, ffn_conv_b, ffn_w_down, ple_norm_w, ple_w_gate, ple_w_proj, out_norm_w):
    small_t, scalars, zs, xbc, qkv, gm = _mixer_inputs(x2d, norm_mix_w, w_in, ssm_conv_w, ssm_conv_b,
                                                       ssm_dt_bias, ssm_a_log, B, S)
    y_ssm = _ssd(xbc, zs, *scalars, ssm_d, ssm_norm_w, B, S)
    kc, vct = _compress(qkv, cmp_pe_k, cmp_pe_v, cmp_wk1, cmp_wv1, cmp_wk2, cmp_wv2, B, S)
    y_attn = _nsa(qkv, kc, vct, small_t, B, S)

    merged = _merge(y_ssm, w_ssm_branch, y_attn, w_attn_branch, gm)
    x1, h2 = _mix_out(merged, w_mix_out.astype(BF16), x2d, norm_ffn_w.reshape(1, D_MODEL))

    act = _ffn_up(h2, ffn_w_gate, ffn_w_up, ffn_conv_w, ffn_conv_b, S)
    x2, h3 = _ffn_down(act, ffn_w_down.astype(BF16), x1, ple_norm_w.reshape(1, D_MODEL))
    return _ple(h3, ple_w_gate.astype(BF16), p2d, ple_w_proj.astype(BF16), x2, out_norm_w.reshape(1, D_MODEL))


def kernel(x, p, norm_mix_w, w_in, ssm_conv_w, ssm_conv_b, ssm_dt_bias, ssm_a_log, ssm_d, ssm_norm_w, cmp_pe_k, cmp_pe_v, cmp_wk1, cmp_wk2, cmp_wv1, cmp_wv2, w_ssm_branch, w_attn_branch, w_mix_out, norm_ffn_w, ffn_w_gate, ffn_w_up, ffn_conv_w, ffn_conv_b, ffn_w_down, ple_norm_w, ple_w_gate, ple_w_proj, final_norm_w):
    B, S, D = x.shape
    depth = w_in.shape[0]
    assert depth == 1, "the final norm is fused into the (single) layer's last kernel"
    x2d = x.reshape(B * S, D)
    out = _layer(x2d, p[0].reshape(B * S, PLE_DIM), B, S, norm_mix_w[0], w_in[0], ssm_conv_w[0], ssm_conv_b[0],
                 ssm_dt_bias[0], ssm_a_log[0], ssm_d[0], ssm_norm_w[0], cmp_pe_k[0], cmp_pe_v[0],
                 cmp_wk1[0], cmp_wk2[0], cmp_wv1[0], cmp_wv2[0], w_ssm_branch[0], w_attn_branch[0],
                 w_mix_out[0], norm_ffn_w[0], ffn_w_gate[0], ffn_w_up[0], ffn_conv_w[0], ffn_conv_b[0],
                 ffn_w_down[0], ple_norm_w[0], ple_w_gate[0], ple_w_proj[0], final_norm_w)
    return out.reshape(B, S, D)
```

```python
import functools
import math

import jax
import jax.numpy as jnp
from jax import lax
from jax.experimental import pallas as pl
from jax.experimental.pallas import tpu as pltpu

F32 = jnp.float32
BF16 = jnp.bfloat16

D_MODEL = 2048
PLE_DIM = 256
EPS = 1e-6
D_INNER = 4096
SSM_HEAD_DIM = 64
SSM_HEADS = 64
SSM_GROUPS = 8
SSM_STATE = 128
SSM_CONV = 4
SSM_CHUNK = 128
SSM_BC = SSM_GROUPS * SSM_STATE
SSM_CONV_DIM = D_INNER + 2 * SSM_BC
ATTN_HEADS = 16
ATTN_HEAD_DIM = 128
ATTN_KV_GROUPS = 4
ATTN_REP = ATTN_HEADS // ATTN_KV_GROUPS
ATTN_WIDTH = ATTN_HEADS * ATTN_HEAD_DIM
KV_WIDTH = ATTN_KV_GROUPS * ATTN_HEAD_DIM
CMP_BLOCK = 32
CMP_STRIDE = 16
SEL_BLOCK = 64
N_SEL = 16
WINDOW = 512
ROPE_THETA = 500000.0
ROPE_DIM = ATTN_HEAD_DIM // 4
D_FF = 5632
FFN_CONV = 3
NEG_INF = -1e30
FORCE_SCORE = 1e4

LANES = 128
TAIL = 8
VMEM_LIMIT = 56 * 1024 * 1024

NAT_DT = D_INNER + SSM_CONV_DIM
NAT_Q = NAT_DT + SSM_HEADS
NAT_GN = NAT_Q + ATTN_WIDTH + 6 * KV_WIDTH
NAT_GM = NAT_GN + 3 * ATTN_HEADS
QKV_WIDTH = ATTN_WIDTH + 6 * KV_WIDTH
WTILE = 1024
N_SMALL = 128
GATE_ROWS = 16
ONES_ROWS = 16
QKV_KC, QKV_VC, QKV_KS, QKV_VS, QKV_KW, QKV_VW = (
    (ATTN_WIDTH + i * KV_WIDTH) // ATTN_HEAD_DIM for i in range(6))


def _cparams(sem):
    return pltpu.CompilerParams(dimension_semantics=sem, vmem_limit_bytes=VMEM_LIMIT)


def _resident(shape):
    return pl.BlockSpec(shape, lambda *_: (0,) * len(shape), pipeline_mode=pl.Buffered(1))


def _rms(xf, w):
    return xf * lax.rsqrt(jnp.mean(xf * xf, axis=-1, keepdims=True) + EPS) * w


def _silu(v):
    return v * jax.nn.sigmoid(v)


def _causal_conv(x, tail, w_ref, bias_ref):
    taps = w_ref.shape[0]
    rows, n = x.shape
    x3 = x.reshape(rows // TAIL, TAIL, n)
    t3 = tail[...].reshape(1, TAIL, n)
    sub = lax.broadcasted_iota(jnp.int32, (1, TAIL, n), 1)
    y = bias_ref[...].reshape(1, 1, n) + w_ref[taps - 1:taps, :].reshape(1, 1, n) * x3
    for s in range(1, taps):
        cur = pltpu.roll(x3, s, 1)
        above = jnp.concatenate([pltpu.roll(t3, s, 1), cur[:-1]], axis=0)
        y = y + w_ref[taps - 1 - s:taps - s, :].reshape(1, 1, n) * jnp.where(sub < s, above, cur)
    tail[...] = x[rows - TAIL:, :]
    return y.reshape(rows, n)


def _cast_per_column_tile(pairs):
    @pl.when(pl.program_id(1) == 0)
    def _():
        for src, dst in pairs:
            dst[...] = src[...].astype(BF16)


def _wide_dot(h_ref, wt_ref, wb):
    _cast_per_column_tile(((wt_ref, wb),))
    return lax.dot_general(h_ref[...], wb[...], (((1,), (1,)), ((), ())), preferred_element_type=F32)


def _small_weight(wt):
    G, R = ATTN_KV_GROUPS, ATTN_REP
    K = wt.shape[1]
    w_dt = wt[NAT_DT:NAT_DT + SSM_HEADS]
    w_g = wt[NAT_GN:NAT_GN + 3 * ATTN_HEADS].reshape(3, G, R, K).transpose(1, 0, 2, 3).reshape(G, 3 * R, K)
    w_g = jnp.pad(w_g, ((0, 0), (0, GATE_ROWS - 3 * R), (0, 0))).reshape(G * GATE_ROWS, K)
    return jnp.concatenate([w_dt, w_g], axis=0)


def _norm_small_kernel(x_ref, nw_ref, whi_ref, wlo_ref, dtb_ref, dtbt_ref, alog_ref, alogt_ref,
                       h_ref, smallt_ref, acum_ref, dtt_ref, acumt_ref):
    L, H = SSM_CHUNK, SSM_HEADS
    nt = (((1,), (1,)), ((), ()))
    h = _rms(x_ref[...], nw_ref[...])
    h_hi = h.astype(BF16)
    h_lo = (h - h_hi.astype(F32)).astype(BF16)
    h_ref[...] = h_hi
    small = (lax.dot_general(h_hi, whi_ref[...], nt, preferred_element_type=F32)
             + lax.dot_general(h_hi, wlo_ref[...], nt, preferred_element_type=F32)
             + lax.dot_general(h_lo, whi_ref[...], nt, preferred_element_type=F32))
    small_t = small.T
    smallt_ref[0] = small_t

    ri = lax.broadcasted_iota(jnp.int32, (L, L), 0)
    ci = lax.broadcasted_iota(jnp.int32, (L, L), 1)
    lower = jnp.where(ci <= ri, 1.0, 0.0)
    upper = jnp.where(ri <= ci, 1.0, 0.0)
    hp = lax.Precision.HIGHEST
    a = -jnp.exp(alog_ref[...])
    at = -jnp.exp(alogt_ref[...])
    per = H // SSM_GROUPS
    for j in range(small.shape[0] // L):
        rows = slice(j * L, (j + 1) * L)
        dt = jax.nn.softplus(small[rows, 0:H] + dtb_ref[...])
        dtt = jax.nn.softplus(small_t[0:H, rows] + dtbt_ref[...])
        acum = jnp.dot(lower, dt * a, preferred_element_type=F32, precision=hp)
        for g in range(SSM_GROUPS):
            acum_ref[0, g, rows, :] = acum[:, g * per:(g + 1) * per]
        dtt_ref[0, :, rows] = dtt
        acumt_ref[0, :, rows] = jnp.dot(dtt * at, upper, preferred_element_type=F32, precision=hp)


def _norm_small(x2d, norm_w, w_small, dt_bias, a_log, B, S, tm=512):
    T = x2d.shape[0]
    H, G = SSM_HEADS, SSM_GROUPS
    per = S // tm
    w_hi = w_small.astype(BF16)
    w_lo = (w_small - w_hi.astype(F32)).astype(BF16)
    const = lambda shape: pl.BlockSpec(shape, lambda i: (0,) * len(shape))
    head_major = pl.BlockSpec((1, H, tm), lambda i: (i // per, 0, i % per))
    return pl.pallas_call(
        _norm_small_kernel,
        grid=(T // tm,),
        in_specs=[pl.BlockSpec((tm, D_MODEL), lambda i: (i, 0)),
                  const((1, D_MODEL)), const((N_SMALL, D_MODEL)), const((N_SMALL, D_MODEL)),
                  const((1, H)), const((H, 1)), const((1, H)), const((H, 1))],
        out_specs=[pl.BlockSpec((tm, D_MODEL), lambda i: (i, 0)),
                   pl.BlockSpec((1, N_SMALL, tm), lambda i: (i // per, 0, i % per)),
                   pl.BlockSpec((1, G, tm, H // G), lambda i: (i // per, 0, i % per, 0)),
                   head_major, head_major],
        out_shape=[jax.ShapeDtypeStruct((T, D_MODEL), BF16),
                   jax.ShapeDtypeStruct((B, N_SMALL, S), F32),
                   jax.ShapeDtypeStruct((B, G, S, H // G), F32),
                   jax.ShapeDtypeStruct((B, H, S), F32),
                   jax.ShapeDtypeStruct((B, H, S), F32)],
        compiler_params=_cparams(("parallel",)),
        name="norm_small",
    )(x2d, norm_w, w_hi, w_lo, dt_bias.reshape(1, H), dt_bias.reshape(H, 1), a_log.reshape(1, H), a_log.reshape(H, 1))


def _proj_act_kernel(h_ref, wt_ref, o_ref, wb, *, act):
    o_ref[...] = act(_wide_dot(h_ref, wt_ref, wb)).astype(o_ref.dtype)


def _proj_conv_kernel(h_ref, wt_ref, cw_ref, cb_ref, o_ref, wb, buf, *, tiles_per_seq):
    m = pl.program_id(1)

    @pl.when(m % tiles_per_seq == 0)
    def _():
        buf[...] = jnp.zeros_like(buf)

    y = _causal_conv(_wide_dot(h_ref, wt_ref, wb), buf, cw_ref, cb_ref)
    o_ref[...] = _silu(y).astype(o_ref.dtype)


def _proj_rope_kernel(h_ref, wt_ref, ca_ref, cb_ref, cc_ref, o_ref, wb, *, scale):
    n = pl.program_id(0)
    d = ATTN_HEAD_DIM
    half = ROPE_DIM // 2
    is_q = n < ATTN_WIDTH // WTILE
    acc = _wide_dot(h_ref, wt_ref, wb)
    q_scale = jnp.where(is_q, scale, 1.0).astype(F32)
    ca, cb, cc = ca_ref[...] * q_scale, cb_ref[...] * q_scale, cc_ref[...] * q_scale

    def rope(hd):
        xh = acc[:, hd * d:(hd + 1) * d]
        up = pltpu.roll(xh, d - half, 1)
        dn = pltpu.roll(xh, half, 1)
        o_ref[:, hd * d:(hd + 1) * d] = (xh * ca + up * cb + dn * cc).astype(o_ref.dtype)

    kv_heads = KV_WIDTH // d
    for hd in range(kv_heads):
        rope(hd)

    @pl.when(is_q)
    def _():
        for hd in range(kv_heads, WTILE // d):
            rope(hd)

    @pl.when(jnp.logical_not(is_q))
    def _():
        o_ref[:, KV_WIDTH:] = acc[:, KV_WIDTH:].astype(o_ref.dtype)


def _rope_tables(S):
    half = ROPE_DIM // 2
    pos = jnp.arange(S)
    inv = ROPE_THETA ** (-(jnp.arange(half, dtype=F32) * 2.0 / ROPE_DIM))
    ang = pos.astype(F32)[:, None] * inv[None, :]
    cos, sin = jnp.cos(ang), jnp.sin(ang)
    rest = ATTN_HEAD_DIM - ROPE_DIM
    ca = jnp.concatenate([cos, cos, jnp.ones((S, rest), F32)], axis=1)
    cb = jnp.concatenate([-sin, jnp.zeros((S, ATTN_HEAD_DIM - half), F32)], axis=1)
    cc = jnp.concatenate([jnp.zeros((S, half), F32), sin, jnp.zeros((S, rest), F32)], axis=1)
    return ca, cb, cc


def _proj(kind, h, wt, row0, width, S, extra=(), tm=1024):
    T, K = h.shape
    tn = WTILE
    per = S // tm
    in_specs = [pl.BlockSpec((tm, K), lambda n, m: (m, 0)),
                pl.BlockSpec((pl.Element(tn), pl.Element(K)),
                             lambda n, m: (pl.multiple_of(row0 + n * tn, TAIL), 0))]
    scratch = [pltpu.VMEM((tn, K), BF16)]
    if kind == "silu":
        body = functools.partial(_proj_act_kernel, act=_silu)
    elif kind == "sigmoid":
        body = functools.partial(_proj_act_kernel, act=jax.nn.sigmoid)
    elif kind == "conv":
        body = functools.partial(_proj_conv_kernel, tiles_per_seq=per)
        taps = extra[0].shape[0]
        in_specs += [pl.BlockSpec((taps, tn), lambda n, m: (0, n)),
                     pl.BlockSpec((1, tn), lambda n, m: (0, n))]
        scratch += [pltpu.VMEM((TAIL, tn), F32)]
    else:
        body = functools.partial(_proj_rope_kernel, scale=math.log2(math.e) / math.sqrt(ATTN_HEAD_DIM))
        tab = pl.BlockSpec((tm, ATTN_HEAD_DIM), lambda n, m: (m % per, 0))
        in_specs += [tab, tab, tab]
    return pl.pallas_call(
        body,
        grid=(width // tn, T // tm),
        in_specs=in_specs,
        out_specs=pl.BlockSpec((tm, tn), lambda n, m: (m, n)),
        out_shape=jax.ShapeDtypeStruct((T, width), BF16),
        scratch_shapes=scratch,
        compiler_params=_cparams(("parallel", "arbitrary")),
        name="proj_" + kind,
    )(h, wt, *extra)


def _ssd_kernel(x_ref, b_ref, c_ref, z_ref, acum_ref, acumt_ref, dtt_ref, dsk_ref, nw_ref, y_ref, h_scr):
    @pl.when(pl.program_id(2) == 0)
    def _():
        h_scr[...] = jnp.zeros_like(h_scr)

    def chunk(j, carry):
        _ssd_chunk(pl.ds(pl.multiple_of(j * SSM_CHUNK, SSM_CHUNK), SSM_CHUNK),
                   x_ref, b_ref, c_ref, z_ref, acum_ref, acumt_ref, dtt_ref, dsk_ref, nw_ref, y_ref, h_scr)
        return carry

    lax.fori_loop(0, x_ref.shape[0] // SSM_CHUNK, chunk, 0)


def _ssd_chunk(rows, x_ref, b_ref, c_ref, z_ref, acum_ref, acumt_ref, dtt_ref, dsk_ref, nw_ref, y_ref, h_scr):
    L = SSM_CHUNK
    P = SSM_HEAD_DIM
    xb = x_ref[rows, :]
    bmb = b_ref[rows, :]
    cmb = c_ref[rows, :]
    n_heads = SSM_HEADS // SSM_GROUPS
    acum = acum_ref[0, 0, rows, :]
    acumT = acumt_ref[0, :, rows]
    dtT = dtt_ref[0, :, rows]
    ri = lax.broadcasted_iota(jnp.int32, (L, L), 0)
    ci = lax.broadcasted_iota(jnp.int32, (L, L), 1)
    tril = ci <= ri

    cb = lax.dot_general(cmb, bmb, (((1,), (1,)), ((), ())), preferred_element_type=F32)
    yoff = lax.dot_general(cmb, h_scr[...].astype(BF16), (((1,), (1,)), ((), ())),
                           preferred_element_type=F32)

    lane = lax.broadcasted_iota(jnp.int32, (L, 2 * P), 1)
    lo = lane < P

    def pair(v0, v1):
        return jnp.where(lo, v0, v1)

    dsk = dsk_ref[...]
    ys = []
    for pr in range(n_heads // 2):
        sl = slice(pr * 2 * P, (pr + 1) * 2 * P)
        xp = xb[:, sl]
        yd, ea = [], []
        for r in (2 * pr, 2 * pr + 1):
            acol = jnp.broadcast_to(acum[:, r:r + 1], (L, L))
            dec = jnp.exp(jnp.where(tril, acol - acumT[r:r + 1, :], -jnp.inf))
            yd.append(jnp.dot((cb * dec * dtT[r:r + 1, :]).astype(BF16), xp, preferred_element_type=F32))
            ea.append(jnp.exp(acol))
        ys.append(pair(yd[0], yd[1]) + yoff[:, sl] * pair(ea[0], ea[1]) + dsk[:, sl] * xp.astype(F32))

    a_last = acumT[:, L - 1:L]
    wT = dtT * jnp.exp(a_last - acumT)
    cdT = jnp.exp(a_last)
    xT = xb.astype(F32).T
    xds = jnp.concatenate([xT[r * P:(r + 1) * P, :] * wT[r:r + 1, :] for r in range(n_heads)], axis=0)
    cd = jnp.concatenate([jnp.broadcast_to(cdT[r:r + 1, :], (P, h_scr.shape[1])) for r in range(n_heads)], axis=0)
    st = jnp.dot(xds.astype(BF16), bmb, preferred_element_type=F32)
    h_scr[...] = h_scr[...] * cd + st

    y = jnp.concatenate(ys, axis=1) * z_ref[rows, :].astype(F32)
    y_ref[rows, :] = _rms(y, nw_ref[...]).astype(y_ref.dtype)


def _ssd(xbc, zs, acum_g, dt_t, acum_t, d_skip, norm_w, B, S, rows=1024):
    T = B * S
    L, G, N = rows, SSM_GROUPS, SSM_STATE
    R = SSM_HEADS // G
    gw = D_INNER // G
    nc = S // L
    dsk = jnp.repeat(d_skip, SSM_HEAD_DIM).reshape(1, D_INNER)
    nw = norm_w.reshape(1, D_INNER)

    row = lambda b, g, c: b * nc + c
    in_specs = [
        pl.BlockSpec((L, gw), lambda b, g, c: (row(b, g, c), g)),
        pl.BlockSpec((L, N), lambda b, g, c: (row(b, g, c), D_INNER // N + g)),
        pl.BlockSpec((L, N), lambda b, g, c: (row(b, g, c), (D_INNER + SSM_BC) // N + g)),
        pl.BlockSpec((L, gw), lambda b, g, c: (row(b, g, c), g)),
        pl.BlockSpec((1, 1, L, R), lambda b, g, c: (b, g, c, 0)),
        pl.BlockSpec((1, R, L), lambda b, g, c: (b, g, c)),
        pl.BlockSpec((1, R, L), lambda b, g, c: (b, g, c)),
        pl.BlockSpec((1, gw), lambda b, g, c: (0, g)),
        pl.BlockSpec((1, gw), lambda b, g, c: (0, g)),
    ]
    return pl.pallas_call(
        _ssd_kernel,
        grid=(B, G, nc),
        in_specs=in_specs,
        out_specs=pl.BlockSpec((L, gw), lambda b, g, c: (row(b, g, c), g)),
        out_shape=jax.ShapeDtypeStruct((T, D_INNER), BF16),
        scratch_shapes=[pltpu.VMEM((gw, N), F32)],
        compiler_params=_cparams(("parallel", "parallel", "arbitrary")),
        name="ssd",
    )(xbc, xbc, xbc, zs, acum_g, acum_t, dt_t, dsk, nw)


def _compress_kernel(k_ref, v_ref, pek_ref, pev_ref, wk1_ref, wv1_ref, wk2_ref, wv2_ref, kc_ref, vct_ref, xf):
    d = ATTN_HEAD_DIM
    nsub = xf.shape[0] // CMP_STRIDE

    def mlp(src_ref, pe_ref, w1_ref, w2_ref):
        xf[...] = src_ref[...].astype(F32)
        u = jnp.zeros((nsub, d), F32)
        v = jnp.zeros((nsub, d), F32)
        for l in range(CMP_STRIDE):
            tok = xf[pl.ds(l, nsub, stride=CMP_STRIDE), :]
            l2 = CMP_STRIDE + l
            u = u + jnp.dot((tok + pe_ref[l:l + 1, :]).astype(BF16), w1_ref[l * d:(l + 1) * d, :],
                            preferred_element_type=F32)
            v = v + jnp.dot((tok + pe_ref[l2:l2 + 1, :]).astype(BF16), w1_ref[l2 * d:(l2 + 1) * d, :],
                            preferred_element_type=F32)
        pre = u + pltpu.roll(v, nsub - 1, 0)
        return jnp.dot(_silu(pre).astype(BF16), w2_ref[...], preferred_element_type=F32)

    kc_ref[0, 0] = mlp(k_ref, pek_ref, wk1_ref, wk2_ref).astype(kc_ref.dtype)
    vct_ref[0, 0] = mlp(v_ref, pev_ref, wv1_ref, wv2_ref).T.astype(vct_ref.dtype)


def _compress(qkv, pe_k, pe_v, wk1, wv1, wk2, wv2, B, S):
    G, d = ATTN_KV_GROUPS, ATTN_HEAD_DIM
    nsub = S // CMP_STRIDE
    full = lambda shape: pl.BlockSpec(shape, lambda b, g: (0, 0))
    out = pl.BlockSpec((1, 1, nsub, d), lambda b, g: (b, g, 0, 0))
    return pl.pallas_call(
        _compress_kernel,
        grid=(B, G),
        in_specs=[pl.BlockSpec((S, d), lambda b, g: (b, QKV_KC + g)),
                  pl.BlockSpec((S, d), lambda b, g: (b, QKV_VC + g)),
                  full((CMP_BLOCK, d)), full((CMP_BLOCK, d)),
                  full((CMP_BLOCK * d, d)), full((CMP_BLOCK * d, d)),
                  full((d, d)), full((d, d))],
        out_specs=[out, out],
        out_shape=[jax.ShapeDtypeStruct((B, G, nsub, d), BF16),
                   jax.ShapeDtypeStruct((B, G, d, nsub), BF16)],
        scratch_shapes=[pltpu.VMEM((S, d), F32)],
        compiler_params=_cparams(("parallel", "parallel")),
        name="compress",
    )(qkv, qkv, pe_k, pe_v, wk1.astype(BF16), wv1.astype(BF16), wk2.astype(BF16), wv2.astype(BF16))


def _staged(n, scores, probs, values):
    for step in range(n + 2):
        if step < n:
            scores(step)
        if 1 <= step <= n:
            probs(step - 1)
        if step >= 2:
            values(step - 2)


def _nsa_kernel(q_ref, kc_ref, vct_ref, ks_ref, vs_ref, kw_ref, vw_ref, gt_ref, o_ref,
                vst, vwt, qt, m_scr, acc_scr, out_scr, s_scr, p_scr, *, tq, tks, n_cmp, n_blk):
    i = pl.program_id(2)
    d = ATTN_HEAD_DIM
    R = ATTN_REP
    heads = [slice(r * tq, (r + 1) * tq) for r in range(R)]

    def transposed(v):
        return v.astype(F32).T.astype(BF16)

    @pl.when(i == 0)
    def _():
        for c in range(vst.shape[0]):
            vst[c, 0:d, :] = transposed(vs_ref[c * tks:(c + 1) * tks, :])
            vst[c, d:, :] = jnp.ones((ONES_ROWS, tks), BF16)
        vwt[0:d, :] = transposed(vw_ref[...])
        vwt[d:, :] = jnp.ones((ONES_ROWS, vwt.shape[1]), BF16)

    for r in range(R):
        qt[:, heads[r]] = transposed(q_ref[:, r * d:(r + 1) * d])
    gate = jax.nn.sigmoid(gt_ref[0])

    nk = kc_ref.shape[2]
    kc = kc_ref[0, 0]
    vct = vct_ref[0, 0]
    n_idx = lax.broadcasted_iota(jnp.int32, (nk, tq), 0)
    t_cmp = i * tq + lax.broadcasted_iota(jnp.int32, (nk, tq), 1)
    cmask = (n_idx * CMP_STRIDE + CMP_BLOCK - 1 <= t_cmp) & (n_idx < n_cmp)
    psum = jnp.zeros((nk, tq), F32)
    for r in range(R):
        s = jnp.where(cmask, jnp.dot(kc, qt[:, heads[r]], preferred_element_type=F32), NEG_INF)
        e = jnp.exp2(s - jnp.max(s, axis=0, keepdims=True))
        p = jnp.where(cmask, e * (1.0 / jnp.sum(e, axis=0, keepdims=True)), 0.0)
        out_scr[:, heads[r]] = gate[r:r + 1, :] * jnp.dot(vct, p.astype(BF16), preferred_element_type=F32)
        psum = psum + p
    bi = lax.broadcasted_iota(jnp.int32, (n_blk, nk), 0)
    ni = lax.broadcasted_iota(jnp.int32, (n_blk, nk), 1)
    ovl = ((ni * CMP_STRIDE < bi * SEL_BLOCK + SEL_BLOCK) & (ni * CMP_STRIDE + CMP_BLOCK > bi * SEL_BLOCK)
           & (ni < n_cmp))
    imp = jnp.dot(jnp.where(ovl, 1.0, 0.0), psum, preferred_element_type=F32, precision=lax.Precision.HIGHEST)
    blk = lax.broadcasted_iota(jnp.int32, (n_blk, tq), 0)
    cur = (i * tq + lax.broadcasted_iota(jnp.int32, (n_blk, tq), 1)) // SEL_BLOCK
    forced = (blk == 0) | (blk == cur) | (blk == cur - 1)
    score = jnp.where(forced, FORCE_SCORE, jnp.where(blk <= cur, imp, NEG_INF))
    rank = jnp.zeros((n_blk, tq), F32)
    for k in range(n_blk):
        sk = score[k:k + 1, :]
        rank = rank + jnp.where((sk > score) | ((sk == score) & (k < blk)), 1.0, 0.0)
    sel = jnp.where(rank < N_SEL, 1.0, 0.0).astype(BF16)

    m_scr[...] = jnp.full(m_scr.shape, NEG_INF, F32)
    acc_scr[...] = jnp.zeros_like(acc_scr)

    def sel_body(c, carry):
        k0 = pl.multiple_of(c * tks, tks)
        key = k0 + lax.broadcasted_iota(jnp.int32, (tks, tq), 0)
        t = i * tq + lax.broadcasted_iota(jnp.int32, (tks, tq), 1)
        in_blk = (lax.broadcasted_iota(jnp.int32, (tks, n_blk), 1)
                  == (k0 + lax.broadcasted_iota(jnp.int32, (tks, n_blk), 0)) // SEL_BLOCK)
        picked = jnp.dot(jnp.where(in_blk, 1.0, 0.0).astype(BF16), sel, preferred_element_type=F32)
        bias = jnp.where((picked > 0.5) & (key <= t), 0.0, NEG_INF)
        k_chunk = ks_ref[pl.ds(k0, tks), :]
        vt_chunk = vst[c]
        m_new, alpha = [None] * R, [None] * R

        def scores(r):
            s = jnp.dot(k_chunk, qt[:, heads[r]], preferred_element_type=F32) + bias
            s_scr[r, 0:tks, :] = s
            m_prev = m_scr[r:r + 1, :]
            m_new[r] = jnp.maximum(m_prev, jnp.max(s, axis=0, keepdims=True))
            alpha[r] = jnp.exp2(m_prev - m_new[r])
            m_scr[r:r + 1, :] = m_new[r]

        def probs(r):
            p_scr[r, 0:tks, :] = jnp.exp2(s_scr[r, 0:tks, :] - m_new[r]).astype(BF16)

        def values(r):
            acc_scr[:, heads[r]] = (alpha[r] * acc_scr[:, heads[r]]
                                    + jnp.dot(vt_chunk, p_scr[r, 0:tks, :], preferred_element_type=F32))

        _staged(R, scores, probs, values)
        return carry

    lax.fori_loop(0, (i * tq + tq - 1) // tks + 1, sel_body, 0)
    for r in range(R):
        o = acc_scr[0:d, heads[r]] * (1.0 / acc_scr[d:d + 1, heads[r]])
        out_scr[:, heads[r]] += gate[R + r:R + r + 1, :] * o

    span = WINDOW + tq
    k0 = pl.multiple_of(jnp.maximum(i * tq - WINDOW, 0), tq)
    dist = (i * tq + lax.broadcasted_iota(jnp.int32, (span, tq), 1)
            - (k0 + lax.broadcasted_iota(jnp.int32, (span, tq), 0)))
    bias = jnp.where((dist >= 0) & (dist < WINDOW), 0.0, NEG_INF)
    k_win = kw_ref[pl.ds(k0, span), :]
    vt_win = vwt[:, pl.ds(k0, span)]
    mx = [None] * R

    def scores(r):
        s = jnp.dot(k_win, qt[:, heads[r]], preferred_element_type=F32) + bias
        s_scr[r] = s
        mx[r] = jnp.max(s, axis=0, keepdims=True)

    def probs(r):
        p_scr[r] = jnp.exp2(s_scr[r] - mx[r]).astype(BF16)

    def values(r):
        o = jnp.dot(vt_win, p_scr[r], preferred_element_type=F32)
        out_scr[:, heads[r]] += gate[2 * R + r:2 * R + r + 1, :] * (o[0:d, :] * (1.0 / o[d:d + 1, :]))

    _staged(R, scores, probs, values)

    for r in range(R):
        o_ref[:, r * d:(r + 1) * d] = out_scr[:, heads[r]].T.astype(o_ref.dtype)


def _nsa(qkv, kc, vct, small_t, B, S, tq=256, tks=512):
    assert WINDOW % tq == 0 and S >= WINDOW + tq
    T = B * S
    G, d, R = ATTN_KV_GROUPS, ATTN_HEAD_DIM, ATTN_REP
    nq = S // tq
    nk = kc.shape[2]
    n_cmp = S // CMP_STRIDE - CMP_BLOCK // CMP_STRIDE + 1
    n_blk = S // SEL_BLOCK
    seq = lambda c0: pl.BlockSpec((S, d), lambda b, g, i: (b, c0 + g))
    cmp_spec = pl.BlockSpec((1, 1, nk, d), lambda b, g, i: (b, g, 0, 0))
    gate0 = SSM_HEADS // GATE_ROWS
    return pl.pallas_call(
        functools.partial(_nsa_kernel, tq=tq, tks=tks, n_cmp=n_cmp, n_blk=n_blk),
        grid=(B, G, nq),
        in_specs=[pl.BlockSpec((tq, KV_WIDTH), lambda b, g, i: (b * nq + i, g)),
                  cmp_spec, cmp_spec,
                  seq(QKV_KS), seq(QKV_VS), seq(QKV_KW), seq(QKV_VW),
                  pl.BlockSpec((1, GATE_ROWS, tq), lambda b, g, i: (b, gate0 + g, i))],
        out_specs=pl.BlockSpec((tq, KV_WIDTH), lambda b, g, i: (b * nq + i, g)),
        out_shape=jax.ShapeDtypeStruct((T, ATTN_WIDTH), BF16),
        scratch_shapes=[pltpu.VMEM((S // tks, d + ONES_ROWS, tks), BF16),
                        pltpu.VMEM((d + ONES_ROWS, S), BF16),
                        pltpu.VMEM((d, R * tq), BF16),
                        pltpu.VMEM((8, tq), F32),
                        pltpu.VMEM((d + ONES_ROWS, R * tq), F32),
                        pltpu.VMEM((d, R * tq), F32),
                        pltpu.VMEM((R, WINDOW + tq, tq), F32),
                        pltpu.VMEM((R, WINDOW + tq, tq), BF16)],
        compiler_params=_cparams(("parallel", "parallel", "arbitrary")),
        name="nsa",
    )(qkv, kc, vct, qkv, qkv, qkv, qkv, small_t)


def _merge_kernel(ys_ref, ws_ref, ya_ref, wa_ref, gs_ref, ga_ref, o_ref, wsb, wab):
    _cast_per_column_tile(((ws_ref, wsb), (wa_ref, wab)))
    s = jnp.dot(ys_ref[...], wsb[...], preferred_element_type=F32)
    a = jnp.dot(ya_ref[...], wab[...], preferred_element_type=F32)
    o_ref[...] = (gs_ref[...].astype(F32) * s + ga_ref[...].astype(F32) * a).astype(o_ref.dtype)


def _merge(y_ssm, w_s, y_attn, w_a, gm, tm=512, tn=512):
    T = y_ssm.shape[0]
    return pl.pallas_call(
        _merge_kernel,
        grid=(D_MODEL // tn, T // tm),
        in_specs=[pl.BlockSpec((tm, D_INNER), lambda n, m: (m, 0)),
                  pl.BlockSpec((D_INNER, tn), lambda n, m: (0, n)),
                  pl.BlockSpec((tm, ATTN_WIDTH), lambda n, m: (m, 0)),
                  pl.BlockSpec((ATTN_WIDTH, tn), lambda n, m: (0, n)),
                  pl.BlockSpec((tm, tn), lambda n, m: (m, n)),
                  pl.BlockSpec((tm, tn), lambda n, m: (m, D_MODEL // tn + n))],
        out_specs=pl.BlockSpec((tm, tn), lambda n, m: (m, n)),
        out_shape=jax.ShapeDtypeStruct((T, D_MODEL), BF16),
        scratch_shapes=[pltpu.VMEM((D_INNER, tn), BF16), pltpu.VMEM((ATTN_WIDTH, tn), BF16)],
        compiler_params=_cparams(("parallel", "arbitrary")),
        name="merge",
    )(y_ssm, w_s, y_attn, w_a, gm, gm)


def _cast_once(src, dst):
    @pl.when(pl.program_id(0) == 0)
    def _():
        dst[...] = src[...].astype(BF16)


def _mix_out_kernel(mg_ref, w_ref, x_ref, nw_ref, x1_ref, h_ref, wb):
    _cast_once(w_ref, wb)
    x1 = x_ref[...] + jnp.dot(mg_ref[...], wb[...], preferred_element_type=F32)
    x1_ref[...] = x1
    h_ref[...] = _rms(x1, nw_ref[...]).astype(h_ref.dtype)


def _mix_out(merged, w, x2d, norm_w, tm=512):
    T = x2d.shape[0]
    return pl.pallas_call(
        _mix_out_kernel,
        grid=(T // tm,),
        in_specs=[pl.BlockSpec((tm, D_MODEL), lambda m: (m, 0)),
                  _resident((D_MODEL, D_MODEL)),
                  pl.BlockSpec((tm, D_MODEL), lambda m: (m, 0)),
                  pl.BlockSpec((1, D_MODEL), lambda m: (0, 0))],
        out_specs=[pl.BlockSpec((tm, D_MODEL), lambda m: (m, 0)),
                   pl.BlockSpec((tm, D_MODEL), lambda m: (m, 0))],
        out_shape=[jax.ShapeDtypeStruct((T, D_MODEL), F32),
                   jax.ShapeDtypeStruct((T, D_MODEL), BF16)],
        scratch_shapes=[pltpu.VMEM((D_MODEL, D_MODEL), BF16)],
        compiler_params=_cparams(("arbitrary",)),
        name="mix_out",
    )(merged, w, x2d, norm_w)


def _ffn_up_kernel(h_ref, wg_ref, wu_ref, cw_ref, cb_ref, o_ref, gbuf, wgb, wub, *, tiles_per_seq):
    m = pl.program_id(1)
    _cast_per_column_tile(((wg_ref, wgb), (wu_ref, wub)))

    @pl.when(m % tiles_per_seq == 0)
    def _():
        gbuf[...] = jnp.zeros_like(gbuf)

    h = h_ref[...]
    g = jnp.dot(h, wgb[...], preferred_element_type=F32)
    u = jnp.dot(h, wub[...], preferred_element_type=F32)
    o_ref[...] = (_silu(_causal_conv(g, gbuf, cw_ref, cb_ref)) * u).astype(o_ref.dtype)


def _ffn_up(h, wg, wu, conv_w, conv_b, S, tm=1024, tn=512):
    T = h.shape[0]
    return pl.pallas_call(
        functools.partial(_ffn_up_kernel, tiles_per_seq=S // tm),
        grid=(D_FF // tn, T // tm),
        in_specs=[pl.BlockSpec((tm, D_MODEL), lambda n, m: (m, 0)),
                  pl.BlockSpec((D_MODEL, tn), lambda n, m: (0, n)),
                  pl.BlockSpec((D_MODEL, tn), lambda n, m: (0, n)),
                  pl.BlockSpec((FFN_CONV, tn), lambda n, m: (0, n)),
                  pl.BlockSpec((1, tn), lambda n, m: (0, n))],
        out_specs=pl.BlockSpec((tm, tn), lambda n, m: (m, n)),
        out_shape=jax.ShapeDtypeStruct((T, D_FF), BF16),
        scratch_shapes=[pltpu.VMEM((TAIL, tn), F32),
                        pltpu.VMEM((D_MODEL, tn), BF16), pltpu.VMEM((D_MODEL, tn), BF16)],
        compiler_params=_cparams(("parallel", "arbitrary")),
        name="ffn_up",
    )(h, wg, wu, conv_w, conv_b.reshape(1, D_FF))


def _ffn_down_kernel(a_ref, w_ref, x_ref, nw_ref, x2_ref, h_ref):
    x2 = x_ref[...] + jnp.dot(a_ref[...], w_ref[...], preferred_element_type=F32)
    x2_ref[...] = x2
    h_ref[...] = _rms(x2, nw_ref[...]).astype(h_ref.dtype)


def _ffn_down(act, w, x1, norm_w, tm=256):
    T = x1.shape[0]
    return pl.pallas_call(
        _ffn_down_kernel,
        grid=(T // tm,),
        in_specs=[pl.BlockSpec((tm, D_FF), lambda m: (m, 0)),
                  _resident((D_FF, D_MODEL)),
                  pl.BlockSpec((tm, D_MODEL), lambda m: (m, 0)),
                  _resident((1, D_MODEL))],
        out_specs=[pl.BlockSpec((tm, D_MODEL), lambda m: (m, 0)),
                   pl.BlockSpec((tm, D_MODEL), lambda m: (m, 0))],
        out_shape=[jax.ShapeDtypeStruct((T, D_MODEL), F32),
                   jax.ShapeDtypeStruct((T, D_MODEL), BF16)],
        compiler_params=_cparams(("parallel",)),
        name="ffn_down",
    )(act, w, x1, norm_w)


def _ple_kernel(h_ref, wg_ref, p_ref, wp_ref, x_ref, nw_ref, o_ref, wgb):
    _cast_once(wg_ref, wgb)
    gate = jax.nn.sigmoid(jnp.dot(h_ref[...], wgb[...], preferred_element_type=F32))
    emb = jnp.dot(p_ref[...].astype(BF16), wp_ref[...], preferred_element_type=F32)
    o_ref[...] = _rms(x_ref[...] + gate * emb, nw_ref[...])


def _ple(h, wg, p2d, wp, x2, norm_w, tm=512):
    T = x2.shape[0]
    return pl.pallas_call(
        _ple_kernel,
        grid=(T // tm,),
        in_specs=[pl.BlockSpec((tm, D_MODEL), lambda m: (m, 0)),
                  _resident((D_MODEL, D_MODEL)),
                  pl.BlockSpec((tm, PLE_DIM), lambda m: (m, 0)),
                  _resident((PLE_DIM, D_MODEL)),
                  pl.BlockSpec((tm, D_MODEL), lambda m: (m, 0)),
                  pl.BlockSpec((1, D_MODEL), lambda m: (0, 0))],
        out_specs=pl.BlockSpec((tm, D_MODEL), lambda m: (m, 0)),
        out_shape=jax.ShapeDtypeStruct((T, D_MODEL), F32),
        scratch_shapes=[pltpu.VMEM((D_MODEL, D_MODEL), BF16)],
        compiler_params=_cparams(("arbitrary",)),
        name="ple",
    )(h, wg, p2d, wp, x2, norm_w)


def _mixer_inputs(x2d, norm_w, w, ssm_conv_w, ssm_conv_b, dt_bias, a_log, B, S):
    wt = w.T
    h, small_t, acum_g, dt_t, acum_t = _norm_small(x2d, norm_w.reshape(1, D_MODEL), _small_weight(wt),
                                                   dt_bias, a_log, B, S)
    zs = _proj("silu", h, wt, 0, D_INNER, S)
    xbc = _proj("conv", h, wt, D_INNER, SSM_CONV_DIM, S,
                extra=(ssm_conv_w, ssm_conv_b.reshape(1, SSM_CONV_DIM)))
    qkv = _proj("rope", h, wt, NAT_Q, QKV_WIDTH, S, extra=_rope_tables(S))
    gm = _proj("sigmoid", h, wt, NAT_GM, 2 * D_MODEL, S)
    return small_t, (acum_g, dt_t, acum_t), zs, xbc, qkv, gm


def _layer(x2d, p2d, B, S, norm_mix_w, w_in, ssm_conv_w, ssm_conv_b, ssm_dt_bias, ssm_a_log, ssm_d,
           ssm_norm_w, cmp_pe_k, cmp_pe_v, cmp_wk1, cmp_wk2, cmp_wv1, cmp_wv2,
           w_ssm_branch, w_attn_branch, w_mix_out, norm_ffn_w, ffn_w_gate, ffn_w_up,
           ffn_conv_w, ffn_conv_b, ffn_w_down, ple_norm_w, ple_w_gate, ple_w_proj, out_norm_w):
    small_t, scalars, zs, xbc, qkv, gm = _mixer_inputs(x2d, norm_mix_w, w_in, ssm_conv_w, ssm_conv_b,
                                                       ssm_dt_bias, ssm_a_log, B, S)
    y_ssm = _ssd(xbc, zs, *scalars, ssm_d, ssm_norm_w, B, S)
    kc, vct = _compress(qkv, cmp_pe_k, cmp_pe_v, cmp_wk1, cmp_wv1, cmp_wk2, cmp_wv2, B, S)
    y_attn = _nsa(qkv, kc, vct, small_t, B, S)

    merged = _merge(y_ssm, w_ssm_branch, y_attn, w_attn_branch, gm)
    x1, h2 = _mix_out(merged, w_mix_out, x2d, norm_ffn_w.reshape(1, D_MODEL))

    act = _ffn_up(h2, ffn_w_gate, ffn_w_up, ffn_conv_w, ffn_conv_b, S)
    x2, h3 = _ffn_down(act, ffn_w_down.astype(BF16), x1, ple_norm_w.reshape(1, D_MODEL))
    return _ple(h3, ple_w_gate, p2d, ple_w_proj.astype(BF16), x2, out_norm_w.reshape(1, D_MODEL))


def kernel(x, p, norm_mix_w, w_in, ssm_conv_w, ssm_conv_b, ssm_dt_bias, ssm_a_log, ssm_d, ssm_norm_w, cmp_pe_k, cmp_pe_v, cmp_wk1, cmp_wk2, cmp_wv1, cmp_wv2, w_ssm_branch, w_attn_branch, w_mix_out, norm_ffn_w, ffn_w_gate, ffn_w_up, ffn_conv_w, ffn_conv_b, ffn_w_down, ple_norm_w, ple_w_gate, ple_w_proj, final_norm_w):
    B, S, D = x.shape
    depth = w_in.shape[0]
    assert depth == 1, "the final norm is fused into the (single) layer's last kernel"
    x2d = x.reshape(B * S, D)
    out = _layer(x2d, p[0].reshape(B * S, PLE_DIM), B, S, norm_mix_w[0], w_in[0], ssm_conv_w[0], ssm_conv_b[0],
                 ssm_dt_bias[0], ssm_a_log[0], ssm_d[0], ssm_norm_w[0], cmp_pe_k[0], cmp_pe_v[0],
                 cmp_wk1[0], cmp_wk2[0], cmp_wv1[0], cmp_wv2[0], w_ssm_branch[0], w_attn_branch[0],
                 w_mix_out[0], norm_ffn_w[0], ffn_w_gate[0], ffn_w_up[0], ffn_conv_w[0], ffn_conv_b[0],
                 ffn_w_down[0], ple_norm_w[0], ple_w_gate[0], ple_w_proj[0], final_norm_w)
    return out.reshape(B, S, D)
```

```python
import functools
import math

import jax
import jax.numpy as jnp
from jax import lax
from jax.experimental import pallas as pl
from jax.experimental.pallas import tpu as pltpu

F32 = jnp.float32
BF16 = jnp.bfloat16

D_MODEL = 2048
PLE_DIM = 256
EPS = 1e-6
D_INNER = 4096
SSM_HEAD_DIM = 64
SSM_HEADS = 64
SSM_GROUPS = 8
SSM_STATE = 128
SSM_CONV = 4
SSM_CHUNK = 128
SSM_BC = SSM_GROUPS * SSM_STATE
SSM_CONV_DIM = D_INNER + 2 * SSM_BC
ATTN_HEADS = 16
ATTN_HEAD_DIM = 128
ATTN_KV_GROUPS = 4
ATTN_REP = ATTN_HEADS // ATTN_KV_GROUPS
ATTN_WIDTH = ATTN_HEADS * ATTN_HEAD_DIM
KV_WIDTH = ATTN_KV_GROUPS * ATTN_HEAD_DIM
CMP_BLOCK = 32
CMP_STRIDE = 16
SEL_BLOCK = 64
N_SEL = 16
WINDOW = 512
ROPE_THETA = 500000.0
ROPE_DIM = ATTN_HEAD_DIM // 4
D_FF = 5632
FFN_CONV = 3
NEG_INF = -1e30
FORCE_SCORE = 1e4

LANES = 128
TAIL = 8
VMEM_LIMIT = 56 * 1024 * 1024

NAT_DT = D_INNER + SSM_CONV_DIM
NAT_Q = NAT_DT + SSM_HEADS
NAT_GN = NAT_Q + ATTN_WIDTH + 6 * KV_WIDTH
NAT_GM = NAT_GN + 3 * ATTN_HEADS
QKV_WIDTH = ATTN_WIDTH + 6 * KV_WIDTH
WTILE = 1024
N_SMALL = 128
GATE_ROWS = 16
ONES_ROWS = 16
QKV_KC, QKV_VC, QKV_KS, QKV_VS, QKV_KW, QKV_VW = (
    (ATTN_WIDTH + i * KV_WIDTH) // ATTN_HEAD_DIM for i in range(6))


def _cparams(sem):
    return pltpu.CompilerParams(dimension_semantics=sem, vmem_limit_bytes=VMEM_LIMIT)


def _resident(shape):
    return pl.BlockSpec(shape, lambda *_: (0,) * len(shape), pipeline_mode=pl.Buffered(1))


def _rms(xf, w):
    return xf * lax.rsqrt(jnp.mean(xf * xf, axis=-1, keepdims=True) + EPS) * w


def _silu(v):
    return v * jax.nn.sigmoid(v)


def _causal_conv(x, tail, w_ref, bias_ref):
    taps = w_ref.shape[0]
    rows, n = x.shape
    x3 = x.reshape(rows // TAIL, TAIL, n)
    t3 = tail[...].reshape(1, TAIL, n)
    sub = lax.broadcasted_iota(jnp.int32, (1, TAIL, n), 1)
    y = bias_ref[...].reshape(1, 1, n) + w_ref[taps - 1:taps, :].reshape(1, 1, n) * x3
    for s in range(1, taps):
        cur = pltpu.roll(x3, s, 1)
        above = jnp.concatenate([pltpu.roll(t3, s, 1), cur[:-1]], axis=0)
        y = y + w_ref[taps - 1 - s:taps - s, :].reshape(1, 1, n) * jnp.where(sub < s, above, cur)
    tail[...] = x[rows - TAIL:, :]
    return y.reshape(rows, n)


def _cast_per_column_tile(pairs):
    @pl.when(pl.program_id(1) == 0)
    def _():
        for src, dst in pairs:
            dst[...] = src[...].astype(BF16)


def _wide_dot(h_ref, wt_ref, wb):
    _cast_per_column_tile(((wt_ref, wb),))
    return lax.dot_general(h_ref[...], wb[...], (((1,), (1,)), ((), ())), preferred_element_type=F32)


def _small_weight(wt):
    G, R = ATTN_KV_GROUPS, ATTN_REP
    K = wt.shape[1]
    w_dt = wt[NAT_DT:NAT_DT + SSM_HEADS]
    w_g = wt[NAT_GN:NAT_GN + 3 * ATTN_HEADS].reshape(3, G, R, K).transpose(1, 0, 2, 3).reshape(G, 3 * R, K)
    w_g = jnp.pad(w_g, ((0, 0), (0, GATE_ROWS - 3 * R), (0, 0))).reshape(G * GATE_ROWS, K)
    return jnp.concatenate([w_dt, w_g], axis=0)


def _norm_small_kernel(x_ref, nw_ref, whi_ref, wlo_ref, dtb_ref, dtbt_ref, alog_ref, alogt_ref,
                       h_ref, smallt_ref, acum_ref, dtt_ref, acumt_ref):
    L, H = SSM_CHUNK, SSM_HEADS
    nt = (((1,), (1,)), ((), ()))
    h = _rms(x_ref[...], nw_ref[...])
    h_hi = h.astype(BF16)
    h_lo = (h - h_hi.astype(F32)).astype(BF16)
    h_ref[...] = h_hi
    small = (lax.dot_general(h_hi, whi_ref[...], nt, preferred_element_type=F32)
             + lax.dot_general(h_hi, wlo_ref[...], nt, preferred_element_type=F32)
             + lax.dot_general(h_lo, whi_ref[...], nt, preferred_element_type=F32))
    small_t = small.T
    smallt_ref[0] = small_t

    ri = lax.broadcasted_iota(jnp.int32, (L, L), 0)
    ci = lax.broadcasted_iota(jnp.int32, (L, L), 1)
    lower = jnp.where(ci <= ri, 1.0, 0.0)
    upper = jnp.where(ri <= ci, 1.0, 0.0)
    hp = lax.Precision.HIGHEST
    a = -jnp.exp(alog_ref[...])
    at = -jnp.exp(alogt_ref[...])
    per = H // SSM_GROUPS
    for j in range(small.shape[0] // L):
        rows = slice(j * L, (j + 1) * L)
        dt = jax.nn.softplus(small[rows, 0:H] + dtb_ref[...])
        dtt = jax.nn.softplus(small_t[0:H, rows] + dtbt_ref[...])
        acum = jnp.dot(lower, dt * a, preferred_element_type=F32, precision=hp)
        for g in range(SSM_GROUPS):
            acum_ref[0, g, rows, :] = acum[:, g * per:(g + 1) * per]
        dtt_ref[0, :, rows] = dtt
        acumt_ref[0, :, rows] = jnp.dot(dtt * at, upper, preferred_element_type=F32, precision=hp)


def _norm_small(x2d, norm_w, w_small, dt_bias, a_log, B, S, tm=512):
    T = x2d.shape[0]
    H, G = SSM_HEADS, SSM_GROUPS
    per = S // tm
    w_hi = w_small.astype(BF16)
    w_lo = (w_small - w_hi.astype(F32)).astype(BF16)
    const = lambda shape: pl.BlockSpec(shape, lambda i: (0,) * len(shape))
    head_major = pl.BlockSpec((1, H, tm), lambda i: (i // per, 0, i % per))
    return pl.pallas_call(
        _norm_small_kernel,
        grid=(T // tm,),
        in_specs=[pl.BlockSpec((tm, D_MODEL), lambda i: (i, 0)),
                  const((1, D_MODEL)), const((N_SMALL, D_MODEL)), const((N_SMALL, D_MODEL)),
                  const((1, H)), const((H, 1)), const((1, H)), const((H, 1))],
        out_specs=[pl.BlockSpec((tm, D_MODEL), lambda i: (i, 0)),
                   pl.BlockSpec((1, N_SMALL, tm), lambda i: (i // per, 0, i % per)),
                   pl.BlockSpec((1, G, tm, H // G), lambda i: (i // per, 0, i % per, 0)),
                   head_major, head_major],
        out_shape=[jax.ShapeDtypeStruct((T, D_MODEL), BF16),
                   jax.ShapeDtypeStruct((B, N_SMALL, S), F32),
                   jax.ShapeDtypeStruct((B, G, S, H // G), F32),
                   jax.ShapeDtypeStruct((B, H, S), F32),
                   jax.ShapeDtypeStruct((B, H, S), F32)],
        compiler_params=_cparams(("parallel",)),
        name="norm_small",
    )(x2d, norm_w, w_hi, w_lo, dt_bias.reshape(1, H), dt_bias.reshape(H, 1), a_log.reshape(1, H), a_log.reshape(H, 1))


def _proj_act_kernel(h_ref, wt_ref, o_ref, wb, *, act):
    o_ref[...] = act(_wide_dot(h_ref, wt_ref, wb)).astype(o_ref.dtype)


def _proj_conv_kernel(h_ref, wt_ref, cw_ref, cb_ref, o_ref, wb, buf, *, tiles_per_seq):
    m = pl.program_id(1)

    @pl.when(m % tiles_per_seq == 0)
    def _():
        buf[...] = jnp.zeros_like(buf)

    y = _causal_conv(_wide_dot(h_ref, wt_ref, wb), buf, cw_ref, cb_ref)
    o_ref[...] = _silu(y).astype(o_ref.dtype)


def _proj_rope_kernel(h_ref, wt_ref, ca_ref, cb_ref, cc_ref, o_ref, wb, *, scale):
    n = pl.program_id(0)
    d = ATTN_HEAD_DIM
    half = ROPE_DIM // 2
    is_q = n < ATTN_WIDTH // WTILE
    acc = _wide_dot(h_ref, wt_ref, wb)
    q_scale = jnp.where(is_q, scale, 1.0).astype(F32)
    ca, cb, cc = ca_ref[...] * q_scale, cb_ref[...] * q_scale, cc_ref[...] * q_scale

    def rope(hd):
        xh = acc[:, hd * d:(hd + 1) * d]
        up = pltpu.roll(xh, d - half, 1)
        dn = pltpu.roll(xh, half, 1)
        o_ref[:, hd * d:(hd + 1) * d] = (xh * ca + up * cb + dn * cc).astype(o_ref.dtype)

    kv_heads = KV_WIDTH // d
    for hd in range(kv_heads):
        rope(hd)

    @pl.when(is_q)
    def _():
        for hd in range(kv_heads, WTILE // d):
            rope(hd)

    @pl.when(jnp.logical_not(is_q))
    def _():
        o_ref[:, KV_WIDTH:] = acc[:, KV_WIDTH:].astype(o_ref.dtype)


def _rope_tables(S):
    half = ROPE_DIM // 2
    pos = jnp.arange(S)
    inv = ROPE_THETA ** (-(jnp.arange(half, dtype=F32) * 2.0 / ROPE_DIM))
    ang = pos.astype(F32)[:, None] * inv[None, :]
    cos, sin = jnp.cos(ang), jnp.sin(ang)
    rest = ATTN_HEAD_DIM - ROPE_DIM
    ca = jnp.concatenate([cos, cos, jnp.ones((S, rest), F32)], axis=1)
    cb = jnp.concatenate([-sin, jnp.zeros((S, ATTN_HEAD_DIM - half), F32)], axis=1)
    cc = jnp.concatenate([jnp.zeros((S, half), F32), sin, jnp.zeros((S, rest), F32)], axis=1)
    return ca, cb, cc


def _proj(kind, h, wt, row0, width, S, extra=(), tm=1024):
    T, K = h.shape
    tn = WTILE
    per = S // tm
    in_specs = [pl.BlockSpec((tm, K), lambda n, m: (m, 0)),
                pl.BlockSpec((pl.Element(tn), pl.Element(K)),
                             lambda n, m: (pl.multiple_of(row0 + n * tn, TAIL), 0))]
    scratch = [pltpu.VMEM((tn, K), BF16)]
    if kind == "silu":
        body = functools.partial(_proj_act_kernel, act=_silu)
    elif kind == "sigmoid":
        body = functools.partial(_proj_act_kernel, act=jax.nn.sigmoid)
    elif kind == "conv":
        body = functools.partial(_proj_conv_kernel, tiles_per_seq=per)
        taps = extra[0].shape[0]
        in_specs += [pl.BlockSpec((taps, tn), lambda n, m: (0, n)),
                     pl.BlockSpec((1, tn), lambda n, m: (0, n))]
        scratch += [pltpu.VMEM((TAIL, tn), F32)]
    else:
        body = functools.partial(_proj_rope_kernel, scale=math.log2(math.e) / math.sqrt(ATTN_HEAD_DIM))
        tab = pl.BlockSpec((tm, ATTN_HEAD_DIM), lambda n, m: (m % per, 0))
        in_specs += [tab, tab, tab]
    return pl.pallas_call(
        body,
        grid=(width // tn, T // tm),
        in_specs=in_specs,
        out_specs=pl.BlockSpec((tm, tn), lambda n, m: (m, n)),
        out_shape=jax.ShapeDtypeStruct((T, width), BF16),
        scratch_shapes=scratch,
        compiler_params=_cparams(("parallel", "arbitrary")),
        name="proj_" + kind,
    )(h, wt, *extra)


def _ssd_kernel(x_ref, b_ref, c_ref, z_ref, acum_ref, acumt_ref, dtt_ref, dsk_ref, nw_ref, y_ref, h_scr):
    @pl.when(pl.program_id(2) == 0)
    def _():
        h_scr[...] = jnp.zeros_like(h_scr)

    def chunk(j, carry):
        _ssd_chunk(pl.ds(pl.multiple_of(j * SSM_CHUNK, SSM_CHUNK), SSM_CHUNK),
                   x_ref, b_ref, c_ref, z_ref, acum_ref, acumt_ref, dtt_ref, dsk_ref, nw_ref, y_ref, h_scr)
        return carry

    lax.fori_loop(0, x_ref.shape[0] // SSM_CHUNK, chunk, 0)


def _ssd_chunk(rows, x_ref, b_ref, c_ref, z_ref, acum_ref, acumt_ref, dtt_ref, dsk_ref, nw_ref, y_ref, h_scr):
    L = SSM_CHUNK
    P = SSM_HEAD_DIM
    xb = x_ref[rows, :]
    bmb = b_ref[rows, :]
    cmb = c_ref[rows, :]
    n_heads = SSM_HEADS // SSM_GROUPS
    acum = acum_ref[0, 0, rows, :]
    acumT = acumt_ref[0, :, rows]
    dtT = dtt_ref[0, :, rows]
    ri = lax.broadcasted_iota(jnp.int32, (L, L), 0)
    ci = lax.broadcasted_iota(jnp.int32, (L, L), 1)
    tril = ci <= ri

    cb = lax.dot_general(cmb, bmb, (((1,), (1,)), ((), ())), preferred_element_type=F32)
    yoff = lax.dot_general(cmb, h_scr[...].astype(BF16), (((1,), (1,)), ((), ())),
                           preferred_element_type=F32)

    lane = lax.broadcasted_iota(jnp.int32, (L, 2 * P), 1)
    lo = lane < P

    def pair(v0, v1):
        return jnp.where(lo, v0, v1)

    dsk = dsk_ref[...]
    ys = []
    for pr in range(n_heads // 2):
        sl = slice(pr * 2 * P, (pr + 1) * 2 * P)
        xp = xb[:, sl]
        yd, ea = [], []
        for r in (2 * pr, 2 * pr + 1):
            acol = jnp.broadcast_to(acum[:, r:r + 1], (L, L))
            dec = jnp.exp(jnp.where(tril, acol - acumT[r:r + 1, :], -jnp.inf))
            yd.append(jnp.dot((cb * dec * dtT[r:r + 1, :]).astype(BF16), xp, preferred_element_type=F32))
            ea.append(jnp.exp(acol))
        ys.append(pair(yd[0], yd[1]) + yoff[:, sl] * pair(ea[0], ea[1]) + dsk[:, sl] * xp.astype(F32))

    a_last = acumT[:, L - 1:L]
    wT = dtT * jnp.exp(a_last - acumT)
    cdT = jnp.exp(a_last)
    xT = xb.astype(F32).T
    xds = jnp.concatenate([xT[r * P:(r + 1) * P, :] * wT[r:r + 1, :] for r in range(n_heads)], axis=0)
    cd = jnp.concatenate([jnp.broadcast_to(cdT[r:r + 1, :], (P, h_scr.shape[1])) for r in range(n_heads)], axis=0)
    st = jnp.dot(xds.astype(BF16), bmb, preferred_element_type=F32)
    h_scr[...] = h_scr[...] * cd + st

    y = jnp.concatenate(ys, axis=1) * z_ref[rows, :].astype(F32)
    y_ref[rows, :] = _rms(y, nw_ref[...]).astype(y_ref.dtype)


def _ssd(xbc, zs, acum_g, dt_t, acum_t, d_skip, norm_w, B, S, rows=1024):
    T = B * S
    L, G, N = rows, SSM_GROUPS, SSM_STATE
    R = SSM_HEADS // G
    gw = D_INNER // G
    nc = S // L
    dsk = jnp.repeat(d_skip, SSM_HEAD_DIM).reshape(1, D_INNER)
    nw = norm_w.reshape(1, D_INNER)

    row = lambda b, g, c: b * nc + c
    in_specs = [
        pl.BlockSpec((L, gw), lambda b, g, c: (row(b, g, c), g)),
        pl.BlockSpec((L, N), lambda b, g, c: (row(b, g, c), D_INNER // N + g)),
        pl.BlockSpec((L, N), lambda b, g, c: (row(b, g, c), (D_INNER + SSM_BC) // N + g)),
        pl.BlockSpec((L, gw), lambda b, g, c: (row(b, g, c), g)),
        pl.BlockSpec((1, 1, L, R), lambda b, g, c: (b, g, c, 0)),
        pl.BlockSpec((1, R, L), lambda b, g, c: (b, g, c)),
        pl.BlockSpec((1, R, L), lambda b, g, c: (b, g, c)),
        pl.BlockSpec((1, gw), lambda b, g, c: (0, g)),
        pl.BlockSpec((1, gw), lambda b, g, c: (0, g)),
    ]
    return pl.pallas_call(
        _ssd_kernel,
        grid=(B, G, nc),
        in_specs=in_specs,
        out_specs=pl.BlockSpec((L, gw), lambda b, g, c: (row(b, g, c), g)),
        out_shape=jax.ShapeDtypeStruct((T, D_INNER), BF16),
        scratch_shapes=[pltpu.VMEM((gw, N), F32)],
        compiler_params=_cparams(("parallel", "parallel", "arbitrary")),
        name="ssd",
    )(xbc, xbc, xbc, zs, acum_g, acum_t, dt_t, dsk, nw)


def _compress_kernel(k_ref, v_ref, pek_ref, pev_ref, wk1_ref, wv1_ref, wk2_ref, wv2_ref, kc_ref, vct_ref, xf):
    d = ATTN_HEAD_DIM
    nsub = xf.shape[0] // CMP_STRIDE

    def mlp(src_ref, pe_ref, w1_ref, w2_ref):
        xf[...] = src_ref[...].astype(F32)
        u = jnp.zeros((nsub, d), F32)
        v = jnp.zeros((nsub, d), F32)
        for l in range(CMP_STRIDE):
            tok = xf[pl.ds(l, nsub, stride=CMP_STRIDE), :]
            l2 = CMP_STRIDE + l
            u = u + jnp.dot((tok + pe_ref[l:l + 1, :]).astype(BF16), w1_ref[l * d:(l + 1) * d, :],
                            preferred_element_type=F32)
            v = v + jnp.dot((tok + pe_ref[l2:l2 + 1, :]).astype(BF16), w1_ref[l2 * d:(l2 + 1) * d, :],
                            preferred_element_type=F32)
        pre = u + pltpu.roll(v, nsub - 1, 0)
        return jnp.dot(_silu(pre).astype(BF16), w2_ref[...], preferred_element_type=F32)

    kc_ref[0, 0] = mlp(k_ref, pek_ref, wk1_ref, wk2_ref).astype(kc_ref.dtype)
    vct_ref[0, 0] = mlp(v_ref, pev_ref, wv1_ref, wv2_ref).T.astype(vct_ref.dtype)


def _compress(qkv, pe_k, pe_v, wk1, wv1, wk2, wv2, B, S):
    G, d = ATTN_KV_GROUPS, ATTN_HEAD_DIM
    nsub = S // CMP_STRIDE
    full = lambda shape: pl.BlockSpec(shape, lambda b, g: (0, 0))
    out = pl.BlockSpec((1, 1, nsub, d), lambda b, g: (b, g, 0, 0))
    return pl.pallas_call(
        _compress_kernel,
        grid=(B, G),
        in_specs=[pl.BlockSpec((S, d), lambda b, g: (b, QKV_KC + g)),
                  pl.BlockSpec((S, d), lambda b, g: (b, QKV_VC + g)),
                  full((CMP_BLOCK, d)), full((CMP_BLOCK, d)),
                  full((CMP_BLOCK * d, d)), full((CMP_BLOCK * d, d)),
                  full((d, d)), full((d, d))],
        out_specs=[out, out],
        out_shape=[jax.ShapeDtypeStruct((B, G, nsub, d), BF16),
                   jax.ShapeDtypeStruct((B, G, d, nsub), BF16)],
        scratch_shapes=[pltpu.VMEM((S, d), F32)],
        compiler_params=_cparams(("parallel", "parallel")),
        name="compress",
    )(qkv, qkv, pe_k, pe_v, wk1.astype(BF16), wv1.astype(BF16), wk2.astype(BF16), wv2.astype(BF16))


def _staged(n, scores, probs, values):
    for step in range(n + 2):
        if step < n:
            scores(step)
        if 1 <= step <= n:
            probs(step - 1)
        if step >= 2:
            values(step - 2)


def _nsa_kernel(q_ref, kc_ref, vct_ref, ks_ref, vs_ref, kw_ref, vw_ref, gt_ref, o_ref,
                vst, vwt, qt, m_scr, acc_scr, out_scr, s_scr, p_scr, a_scr, *, tq, tks, n_cmp, n_blk):
    i = pl.program_id(2)
    d = ATTN_HEAD_DIM
    R = ATTN_REP
    heads = [slice(r * tq, (r + 1) * tq) for r in range(R)]

    def transposed(v):
        return v.astype(F32).T.astype(BF16)

    @pl.when(i == 0)
    def _():
        for c in range(vst.shape[0]):
            vst[c, 0:d, :] = transposed(vs_ref[c * tks:(c + 1) * tks, :])
            vst[c, d:, :] = jnp.ones((ONES_ROWS, tks), BF16)
        vwt[0:d, :] = transposed(vw_ref[...])
        vwt[d:, :] = jnp.ones((ONES_ROWS, vwt.shape[1]), BF16)

    for r in range(R):
        qt[:, heads[r]] = transposed(q_ref[:, r * d:(r + 1) * d])
    gate = jax.nn.sigmoid(gt_ref[0])

    nk = kc_ref.shape[2]
    kc = kc_ref[0, 0]
    vct = vct_ref[0, 0]
    n_idx = lax.broadcasted_iota(jnp.int32, (nk, tq), 0)
    t_cmp = i * tq + lax.broadcasted_iota(jnp.int32, (nk, tq), 1)
    cmask = (n_idx * CMP_STRIDE + CMP_BLOCK - 1 <= t_cmp) & (n_idx < n_cmp)
    psum = jnp.zeros((nk, tq), F32)
    for r in range(R):
        s = jnp.where(cmask, jnp.dot(kc, qt[:, heads[r]], preferred_element_type=F32), NEG_INF)
        e = jnp.exp2(s - jnp.max(s, axis=0, keepdims=True))
        p = jnp.where(cmask, e * (1.0 / jnp.sum(e, axis=0, keepdims=True)), 0.0)
        out_scr[:, heads[r]] = gate[r:r + 1, :] * jnp.dot(vct, p.astype(BF16), preferred_element_type=F32)
        psum = psum + p
    bi = lax.broadcasted_iota(jnp.int32, (n_blk, nk), 0)
    ni = lax.broadcasted_iota(jnp.int32, (n_blk, nk), 1)
    ovl = ((ni * CMP_STRIDE < bi * SEL_BLOCK + SEL_BLOCK) & (ni * CMP_STRIDE + CMP_BLOCK > bi * SEL_BLOCK)
           & (ni < n_cmp))
    imp = jnp.dot(jnp.where(ovl, 1.0, 0.0), psum, preferred_element_type=F32, precision=lax.Precision.HIGHEST)
    blk = lax.broadcasted_iota(jnp.int32, (n_blk, tq), 0)
    cur = (i * tq + lax.broadcasted_iota(jnp.int32, (n_blk, tq), 1)) // SEL_BLOCK
    forced = (blk == 0) | (blk == cur) | (blk == cur - 1)
    score = jnp.where(forced, FORCE_SCORE, jnp.where(blk <= cur, imp, NEG_INF))
    rank = jnp.zeros((n_blk, tq), F32)
    for k in range(n_blk):
        sk = score[k:k + 1, :]
        rank = rank + jnp.where((sk > score) | ((sk == score) & (k < blk)), 1.0, 0.0)
    sel = jnp.where(rank < N_SEL, 1.0, 0.0).astype(BF16)

    m_scr[...] = jnp.full(m_scr.shape, NEG_INF, F32)
    acc_scr[...] = jnp.zeros_like(acc_scr)
    n_chunks = (i * tq + tq - 1) // tks + 1

    def sel_inputs(c):
        k0 = pl.multiple_of(c * tks, tks)
        key = k0 + lax.broadcasted_iota(jnp.int32, (tks, tq), 0)
        t = i * tq + lax.broadcasted_iota(jnp.int32, (tks, tq), 1)
        in_blk = (lax.broadcasted_iota(jnp.int32, (tks, n_blk), 1)
                  == (k0 + lax.broadcasted_iota(jnp.int32, (tks, n_blk), 0)) // SEL_BLOCK)
        picked = jnp.dot(jnp.where(in_blk, 1.0, 0.0).astype(BF16), sel, preferred_element_type=F32)
        return ks_ref[pl.ds(k0, tks), :], jnp.where((picked > 0.5) & (key <= t), 0.0, NEG_INF)

    def sel_scores(r, k_chunk, bias):
        s = jnp.dot(k_chunk, qt[:, heads[r]], preferred_element_type=F32) + bias
        s_scr[r, 0:tks, :] = s
        m_prev = m_scr[r:r + 1, :]
        m_new = jnp.maximum(m_prev, jnp.max(s, axis=0, keepdims=True))
        m_scr[r:r + 1, :] = m_new
        return jnp.exp2(m_prev - m_new)

    def sel_probs(r):
        p_scr[r, 0:tks, :] = jnp.exp2(s_scr[r, 0:tks, :] - m_scr[r:r + 1, :]).astype(BF16)

    def sel_values(r, vt_chunk):
        acc_scr[:, heads[r]] = (a_scr[r:r + 1, :] * acc_scr[:, heads[r]]
                                + jnp.dot(vt_chunk, p_scr[r, 0:tks, :], preferred_element_type=F32))

    k_chunk, bias = sel_inputs(0)
    for r in range(R):
        a_scr[r:r + 1, :] = sel_scores(r, k_chunk, bias)

    def sel_body(c, carry):
        k_chunk, bias = sel_inputs(c)
        vt_prev = vst[c - 1]
        sel_probs(0)
        for r in range(R):
            alpha = sel_scores(r, k_chunk, bias)
            if r + 1 < R:
                sel_probs(r + 1)
            sel_values(r, vt_prev)
            a_scr[r:r + 1, :] = alpha
        return carry

    lax.fori_loop(1, n_chunks, sel_body, 0)
    vt_last = vst[n_chunks - 1]
    sel_probs(0)
    for r in range(R):
        if r + 1 < R:
            sel_probs(r + 1)
        sel_values(r, vt_last)
    for r in range(R):
        o = acc_scr[0:d, heads[r]] * (1.0 / acc_scr[d:d + 1, heads[r]])
        out_scr[:, heads[r]] += gate[R + r:R + r + 1, :] * o

    span = WINDOW + tq
    k0 = pl.multiple_of(jnp.maximum(i * tq - WINDOW, 0), tq)
    dist = (i * tq + lax.broadcasted_iota(jnp.int32, (span, tq), 1)
            - (k0 + lax.broadcasted_iota(jnp.int32, (span, tq), 0)))
    bias = jnp.where((dist >= 0) & (dist < WINDOW), 0.0, NEG_INF)
    k_win = kw_ref[pl.ds(k0, span), :]
    vt_win = vwt[:, pl.ds(k0, span)]
    mx = [None] * R

    def scores(r):
        s = jnp.dot(k_win, qt[:, heads[r]], preferred_element_type=F32) + bias
        s_scr[r] = s
        mx[r] = jnp.max(s, axis=0, keepdims=True)

    def probs(r):
        p_scr[r] = jnp.exp2(s_scr[r] - mx[r]).astype(BF16)

    def values(r):
        o = jnp.dot(vt_win, p_scr[r], preferred_element_type=F32)
        out_scr[:, heads[r]] += gate[2 * R + r:2 * R + r + 1, :] * (o[0:d, :] * (1.0 / o[d:d + 1, :]))

    _staged(R, scores, probs, values)

    for r in range(R):
        o_ref[:, r * d:(r + 1) * d] = out_scr[:, heads[r]].T.astype(o_ref.dtype)


def _nsa(qkv, kc, vct, small_t, B, S, tq=256, tks=512):
    assert WINDOW % tq == 0 and S >= WINDOW + tq
    T = B * S
    G, d, R = ATTN_KV_GROUPS, ATTN_HEAD_DIM, ATTN_REP
    nq = S // tq
    nk = kc.shape[2]
    n_cmp = S // CMP_STRIDE - CMP_BLOCK // CMP_STRIDE + 1
    n_blk = S // SEL_BLOCK
    seq = lambda c0: pl.BlockSpec((S, d), lambda b, g, i: (b, c0 + g))
    cmp_spec = pl.BlockSpec((1, 1, nk, d), lambda b, g, i: (b, g, 0, 0))
    gate0 = SSM_HEADS // GATE_ROWS
    return pl.pallas_call(
        functools.partial(_nsa_kernel, tq=tq, tks=tks, n_cmp=n_cmp, n_blk=n_blk),
        grid=(B, G, nq),
        in_specs=[pl.BlockSpec((tq, KV_WIDTH), lambda b, g, i: (b * nq + i, g)),
                  cmp_spec, cmp_spec,
                  seq(QKV_KS), seq(QKV_VS), seq(QKV_KW), seq(QKV_VW),
                  pl.BlockSpec((1, GATE_ROWS, tq), lambda b, g, i: (b, gate0 + g, i))],
        out_specs=pl.BlockSpec((tq, KV_WIDTH), lambda b, g, i: (b * nq + i, g)),
        out_shape=jax.ShapeDtypeStruct((T, ATTN_WIDTH), BF16),
        scratch_shapes=[pltpu.VMEM((S // tks, d + ONES_ROWS, tks), BF16),
                        pltpu.VMEM((d + ONES_ROWS, S), BF16),
                        pltpu.VMEM((d, R * tq), BF16),
                        pltpu.VMEM((8, tq), F32),
                        pltpu.VMEM((d + ONES_ROWS, R * tq), F32),
                        pltpu.VMEM((d, R * tq), F32),
                        pltpu.VMEM((R, WINDOW + tq, tq), F32),
                        pltpu.VMEM((R, WINDOW + tq, tq), BF16),
                        pltpu.VMEM((8, tq), F32)],
        compiler_params=_cparams(("parallel", "parallel", "arbitrary")),
        name="nsa",
    )(qkv, kc, vct, qkv, qkv, qkv, qkv, small_t)


def _merge_kernel(ys_ref, ws_ref, ya_ref, wa_ref, gs_ref, ga_ref, o_ref, wsb, wab):
    _cast_per_column_tile(((ws_ref, wsb), (wa_ref, wab)))
    s = jnp.dot(ys_ref[...], wsb[...], preferred_element_type=F32)
    a = jnp.dot(ya_ref[...], wab[...], preferred_element_type=F32)
    o_ref[...] = (gs_ref[...].astype(F32) * s + ga_ref[...].astype(F32) * a).astype(o_ref.dtype)


def _merge(y_ssm, w_s, y_attn, w_a, gm, tm=512, tn=512):
    T = y_ssm.shape[0]
    return pl.pallas_call(
        _merge_kernel,
        grid=(D_MODEL // tn, T // tm),
        in_specs=[pl.BlockSpec((tm, D_INNER), lambda n, m: (m, 0)),
                  pl.BlockSpec((D_INNER, tn), lambda n, m: (0, n)),
                  pl.BlockSpec((tm, ATTN_WIDTH), lambda n, m: (m, 0)),
                  pl.BlockSpec((ATTN_WIDTH, tn), lambda n, m: (0, n)),
                  pl.BlockSpec((tm, tn), lambda n, m: (m, n)),
                  pl.BlockSpec((tm, tn), lambda n, m: (m, D_MODEL // tn + n))],
        out_specs=pl.BlockSpec((tm, tn), lambda n, m: (m, n)),
        out_shape=jax.ShapeDtypeStruct((T, D_MODEL), BF16),
        scratch_shapes=[pltpu.VMEM((D_INNER, tn), BF16), pltpu.VMEM((ATTN_WIDTH, tn), BF16)],
        compiler_params=_cparams(("parallel", "arbitrary")),
        name="merge",
    )(y_ssm, w_s, y_attn, w_a, gm, gm)


def _cast_once(src, dst):
    @pl.when(pl.program_id(0) == 0)
    def _():
        dst[...] = src[...].astype(BF16)


def _mix_out_kernel(mg_ref, w_ref, x_ref, nw_ref, x1_ref, h_ref, wb):
    _cast_once(w_ref, wb)
    x1 = x_ref[...] + jnp.dot(mg_ref[...], wb[...], preferred_element_type=F32)
    x1_ref[...] = x1
    h_ref[...] = _rms(x1, nw_ref[...]).astype(h_ref.dtype)


def _mix_out(merged, w, x2d, norm_w, tm=512):
    T = x2d.shape[0]
    return pl.pallas_call(
        _mix_out_kernel,
        grid=(T // tm,),
        in_specs=[pl.BlockSpec((tm, D_MODEL), lambda m: (m, 0)),
                  _resident((D_MODEL, D_MODEL)),
                  pl.BlockSpec((tm, D_MODEL), lambda m: (m, 0)),
                  pl.BlockSpec((1, D_MODEL), lambda m: (0, 0))],
        out_specs=[pl.BlockSpec((tm, D_MODEL), lambda m: (m, 0)),
                   pl.BlockSpec((tm, D_MODEL), lambda m: (m, 0))],
        out_shape=[jax.ShapeDtypeStruct((T, D_MODEL), F32),
                   jax.ShapeDtypeStruct((T, D_MODEL), BF16)],
        scratch_shapes=[pltpu.VMEM((D_MODEL, D_MODEL), BF16)],
        compiler_params=_cparams(("arbitrary",)),
        name="mix_out",
    )(merged, w, x2d, norm_w)


def _ffn_up_kernel(h_ref, wg_ref, wu_ref, cw_ref, cb_ref, o_ref, gbuf, wgb, wub, *, tiles_per_seq):
    m = pl.program_id(1)
    _cast_per_column_tile(((wg_ref, wgb), (wu_ref, wub)))

    @pl.when(m % tiles_per_seq == 0)
    def _():
        gbuf[...] = jnp.zeros_like(gbuf)

    h = h_ref[...]
    g = jnp.dot(h, wgb[...], preferred_element_type=F32)
    u = jnp.dot(h, wub[...], preferred_element_type=F32)
    o_ref[...] = (_silu(_causal_conv(g, gbuf, cw_ref, cb_ref)) * u).astype(o_ref.dtype)


def _ffn_up(h, wg, wu, conv_w, conv_b, S, tm=1024, tn=512):
    T = h.shape[0]
    return pl.pallas_call(
        functools.partial(_ffn_up_kernel, tiles_per_seq=S // tm),
        grid=(D_FF // tn, T // tm),
        in_specs=[pl.BlockSpec((tm, D_MODEL), lambda n, m: (m, 0)),
                  pl.BlockSpec((D_MODEL, tn), lambda n, m: (0, n)),
                  pl.BlockSpec((D_MODEL, tn), lambda n, m: (0, n)),
                  pl.BlockSpec((FFN_CONV, tn), lambda n, m: (0, n)),
                  pl.BlockSpec((1, tn), lambda n, m: (0, n))],
        out_specs=pl.BlockSpec((tm, tn), lambda n, m: (m, n)),
        out_shape=jax.ShapeDtypeStruct((T, D_FF), BF16),
        scratch_shapes=[pltpu.VMEM((TAIL, tn), F32),
                        pltpu.VMEM((D_MODEL, tn), BF16), pltpu.VMEM((D_MODEL, tn), BF16)],
        compiler_params=_cparams(("parallel", "arbitrary")),
        name="ffn_up",
    )(h, wg, wu, conv_w, conv_b.reshape(1, D_FF))


def _ffn_down_kernel(a_ref, w_ref, x_ref, nw_ref, x2_ref, h_ref):
    x2 = x_ref[...] + jnp.dot(a_ref[...], w_ref[...], preferred_element_type=F32)
    x2_ref[...] = x2
    h_ref[...] = _rms(x2, nw_ref[...]).astype(h_ref.dtype)


def _ffn_down(act, w, x1, norm_w, tm=256):
    T = x1.shape[0]
    return pl.pallas_call(
        _ffn_down_kernel,
        grid=(T // tm,),
        in_specs=[pl.BlockSpec((tm, D_FF), lambda m: (m, 0)),
                  _resident((D_FF, D_MODEL)),
                  pl.BlockSpec((tm, D_MODEL), lambda m: (m, 0)),
                  _resident((1, D_MODEL))],
        out_specs=[pl.BlockSpec((tm, D_MODEL), lambda m: (m, 0)),
                   pl.BlockSpec((tm, D_MODEL), lambda m: (m, 0))],
        out_shape=[jax.ShapeDtypeStruct((T, D_MODEL), F32),
                   jax.ShapeDtypeStruct((T, D_MODEL), BF16)],
        compiler_params=_cparams(("parallel",)),
        name="ffn_down",
    )(act, w, x1, norm_w)


def _ple_kernel(h_ref, wg_ref, p_ref, wp_ref, x_ref, nw_ref, o_ref, wgb):
    _cast_once(wg_ref, wgb)
    gate = jax.nn.sigmoid(jnp.dot(h_ref[...], wgb[...], preferred_element_type=F32))
    emb = jnp.dot(p_ref[...].astype(BF16), wp_ref[...], preferred_element_type=F32)
    o_ref[...] = _rms(x_ref[...] + gate * emb, nw_ref[...])


def _ple(h, wg, p2d, wp, x2, norm_w, tm=512):
    T = x2.shape[0]
    return pl.pallas_call(
        _ple_kernel,
        grid=(T // tm,),
        in_specs=[pl.BlockSpec((tm, D_MODEL), lambda m: (m, 0)),
                  _resident((D_MODEL, D_MODEL)),
                  pl.BlockSpec((tm, PLE_DIM), lambda m: (m, 0)),
                  _resident((PLE_DIM, D_MODEL)),
                  pl.BlockSpec((tm, D_MODEL), lambda m: (m, 0)),
                  pl.BlockSpec((1, D_MODEL), lambda m: (0, 0))],
        out_specs=pl.BlockSpec((tm, D_MODEL), lambda m: (m, 0)),
        out_shape=jax.ShapeDtypeStruct((T, D_MODEL), F32),
        scratch_shapes=[pltpu.VMEM((D_MODEL, D_MODEL), BF16)],
        compiler_params=_cparams(("arbitrary",)),
        name="ple",
    )(h, wg, p2d, wp, x2, norm_w)


def _mixer_inputs(x2d, norm_w, w, ssm_conv_w, ssm_conv_b, dt_bias, a_log, B, S):
    wt = w.T
    h, small_t, acum_g, dt_t, acum_t = _norm_small(x2d, norm_w.reshape(1, D_MODEL), _small_weight(wt),
                                                   dt_bias, a_log, B, S)
    zs = _proj("silu", h, wt, 0, D_INNER, S)
    xbc = _proj("conv", h, wt, D_INNER, SSM_CONV_DIM, S,
                extra=(ssm_conv_w, ssm_conv_b.reshape(1, SSM_CONV_DIM)))
    qkv = _proj("rope", h, wt, NAT_Q, QKV_WIDTH, S, extra=_rope_tables(S))
    gm = _proj("sigmoid", h, wt, NAT_GM, 2 * D_MODEL, S)
    return small_t, (acum_g, dt_t, acum_t), zs, xbc, qkv, gm


def _layer(x2d, p2d, B, S, norm_mix_w, w_in, ssm_conv_w, ssm_conv_b, ssm_dt_bias, ssm_a_log, ssm_d,
           ssm_norm_w, cmp_pe_k, cmp_pe_v, cmp_wk1, cmp_wk2, cmp_wv1, cmp_wv2,
           w_ssm_branch, w_attn_branch, w_mix_out, norm_ffn_w, ffn_w_gate, ffn_w_up,
           ffn_conv_w, ffn_conv_b, ffn_w_down, ple_norm_w, ple_w_gate, ple_w_proj, out_norm_w):
    small_t, scalars, zs, xbc, qkv, gm = _mixer_inputs(x2d, norm_mix_w, w_in, ssm_conv_w, ssm_conv_b,
                                                       ssm_dt_bias, ssm_a_log, B, S)
    y_ssm = _ssd(xbc, zs, *scalars, ssm_d, ssm_norm_w, B, S)
    kc, vct = _compress(qkv, cmp_pe_k, cmp_pe_v, cmp_wk1, cmp_wv1, cmp_wk2, cmp_wv2, B, S)
    y_attn = _nsa(qkv, kc, vct, small_t, B, S)

    merged = _merge(y_ssm, w_ssm_branch, y_attn, w_attn_branch, gm)
    x1, h2 = _mix_out(merged, w_mix_out, x2d, norm_ffn_w.reshape(1, D_MODEL))

    act = _ffn_up(h2, ffn_w_gate, ffn_w_up, ffn_conv_w, ffn_conv_b, S)
    x2, h3 = _ffn_down(act, ffn_w_down.astype(BF16), x1, ple_norm_w.reshape(1, D_MODEL))
    return _ple(h3, ple_w_gate, p2d, ple_w_proj.astype(BF16), x2, out_norm_w.reshape(1, D_MODEL))


def kernel(x, p, norm_mix_w, w_in, ssm_conv_w, ssm_conv_b, ssm_dt_bias, ssm_a_log, ssm_d, ssm_norm_w, cmp_pe_k, cmp_pe_v, cmp_wk1, cmp_wk2, cmp_wv1, cmp_wv2, w_ssm_branch, w_attn_branch, w_mix_out, norm_ffn_w, ffn_w_gate, ffn_w_up, ffn_conv_w, ffn_conv_b, ffn_w_down, ple_norm_w, ple_w_gate, ple_w_proj, final_norm_w):
    B, S, D = x.shape
    depth = w_in.shape[0]
    assert depth == 1, "the final norm is fused into the (single) layer's last kernel"
    x2d = x.reshape(B * S, D)
    out = _layer(x2d, p[0].reshape(B * S, PLE_DIM), B, S, norm_mix_w[0], w_in[0], ssm_conv_w[0], ssm_conv_b[0],
                 ssm_dt_bias[0], ssm_a_log[0], ssm_d[0], ssm_norm_w[0], cmp_pe_k[0], cmp_pe_v[0],
                 cmp_wk1[0], cmp_wk2[0], cmp_wv1[0], cmp_wv2[0], w_ssm_branch[0], w_attn_branch[0],
                 w_mix_out[0], norm_ffn_w[0], ffn_w_gate[0], ffn_w_up[0], ffn_conv_w[0], ffn_conv_b[0],
                 ffn_w_down[0], ple_norm_w[0], ple_w_gate[0], ple_w_proj[0], final_norm_w)
    return out.reshape(B, S, D)
```

```python
import functools
import math

import jax
import jax.numpy as jnp
from jax import lax
from jax.experimental import pallas as pl
from jax.experimental.pallas import tpu as pltpu

F32 = jnp.float32
BF16 = jnp.bfloat16

D_MODEL = 2048
PLE_DIM = 256
EPS = 1e-6
D_INNER = 4096
SSM_HEAD_DIM = 64
SSM_HEADS = 64
SSM_GROUPS = 8
SSM_STATE = 128
SSM_CONV = 4
SSM_CHUNK = 128
SSM_BC = SSM_GROUPS * SSM_STATE
SSM_CONV_DIM = D_INNER + 2 * SSM_BC
ATTN_HEADS = 16
ATTN_HEAD_DIM = 128
ATTN_KV_GROUPS = 4
ATTN_REP = ATTN_HEADS // ATTN_KV_GROUPS
ATTN_WIDTH = ATTN_HEADS * ATTN_HEAD_DIM
KV_WIDTH = ATTN_KV_GROUPS * ATTN_HEAD_DIM
CMP_BLOCK = 32
CMP_STRIDE = 16
SEL_BLOCK = 64
N_SEL = 16
WINDOW = 512
ROPE_THETA = 500000.0
ROPE_DIM = ATTN_HEAD_DIM // 4
D_FF = 5632
FFN_CONV = 3
NEG_INF = -1e30
FORCE_SCORE = 1e4

LANES = 128
TAIL = 8
VMEM_LIMIT = 56 * 1024 * 1024

NAT_DT = D_INNER + SSM_CONV_DIM
NAT_Q = NAT_DT + SSM_HEADS
NAT_GN = NAT_Q + ATTN_WIDTH + 6 * KV_WIDTH
NAT_GM = NAT_GN + 3 * ATTN_HEADS
QKV_WIDTH = ATTN_WIDTH + 6 * KV_WIDTH
WTILE = 1024
N_SMALL = 128
GATE_ROWS = 16
ONES_ROWS = 16
QKV_KC, QKV_VC, QKV_KS, QKV_VS, QKV_KW, QKV_VW = (
    (ATTN_WIDTH + i * KV_WIDTH) // ATTN_HEAD_DIM for i in range(6))


def _cparams(sem):
    return pltpu.CompilerParams(dimension_semantics=sem, vmem_limit_bytes=VMEM_LIMIT)


def _resident(shape):
    return pl.BlockSpec(shape, lambda *_: (0,) * len(shape), pipeline_mode=pl.Buffered(1))


def _rms(xf, w):
    return xf * lax.rsqrt(jnp.mean(xf * xf, axis=-1, keepdims=True) + EPS) * w


def _silu(v):
    return v * jax.nn.sigmoid(v)


def _causal_conv(x, tail, w_ref, bias_ref):
    taps = w_ref.shape[0]
    rows, n = x.shape
    x3 = x.reshape(rows // TAIL, TAIL, n)
    t3 = tail[...].reshape(1, TAIL, n)
    sub = lax.broadcasted_iota(jnp.int32, (1, TAIL, n), 1)
    y = bias_ref[...].reshape(1, 1, n) + w_ref[taps - 1:taps, :].reshape(1, 1, n) * x3
    for s in range(1, taps):
        cur = pltpu.roll(x3, s, 1)
        above = jnp.concatenate([pltpu.roll(t3, s, 1), cur[:-1]], axis=0)
        y = y + w_ref[taps - 1 - s:taps - s, :].reshape(1, 1, n) * jnp.where(sub < s, above, cur)
    tail[...] = x[rows - TAIL:, :]
    return y.reshape(rows, n)


def _cast_per_column_tile(pairs):
    @pl.when(pl.program_id(1) == 0)
    def _():
        for src, dst in pairs:
            dst[...] = src[...].astype(BF16)


def _wide_dot(h_ref, wt_ref, wb):
    _cast_per_column_tile(((wt_ref, wb),))
    return lax.dot_general(h_ref[...], wb[...], (((1,), (1,)), ((), ())), preferred_element_type=F32)


def _small_weight(wt):
    G, R = ATTN_KV_GROUPS, ATTN_REP
    K = wt.shape[1]
    w_dt = wt[NAT_DT:NAT_DT + SSM_HEADS]
    w_g = wt[NAT_GN:NAT_GN + 3 * ATTN_HEADS].reshape(3, G, R, K).transpose(1, 0, 2, 3).reshape(G, 3 * R, K)
    w_g = jnp.pad(w_g, ((0, 0), (0, GATE_ROWS - 3 * R), (0, 0))).reshape(G * GATE_ROWS, K)
    return jnp.concatenate([w_dt, w_g], axis=0)


def _norm_small_kernel(x_ref, nw_ref, whi_ref, wlo_ref, dtb_ref, dtbt_ref, alog_ref, alogt_ref,
                       h_ref, smallt_ref, acum_ref, dtt_ref, acumt_ref):
    L, H = SSM_CHUNK, SSM_HEADS
    nt = (((1,), (1,)), ((), ()))
    h = _rms(x_ref[...], nw_ref[...])
    h_hi = h.astype(BF16)
    h_lo = (h - h_hi.astype(F32)).astype(BF16)
    h_ref[...] = h_hi
    small = (lax.dot_general(h_hi, whi_ref[...], nt, preferred_element_type=F32)
             + lax.dot_general(h_hi, wlo_ref[...], nt, preferred_element_type=F32)
             + lax.dot_general(h_lo, whi_ref[...], nt, preferred_element_type=F32))
    small_t = small.T
    smallt_ref[0] = small_t

    ri = lax.broadcasted_iota(jnp.int32, (L, L), 0)
    ci = lax.broadcasted_iota(jnp.int32, (L, L), 1)
    lower = jnp.where(ci <= ri, 1.0, 0.0)
    upper = jnp.where(ri <= ci, 1.0, 0.0)
    hp = lax.Precision.HIGHEST
    a = -jnp.exp(alog_ref[...])
    at = -jnp.exp(alogt_ref[...])
    per = H // SSM_GROUPS
    for j in range(small.shape[0] // L):
        rows = slice(j * L, (j + 1) * L)
        dt = jax.nn.softplus(small[rows, 0:H] + dtb_ref[...])
        dtt = jax.nn.softplus(small_t[0:H, rows] + dtbt_ref[...])
        acum = jnp.dot(lower, dt * a, preferred_element_type=F32, precision=hp)
        for g in range(SSM_GROUPS):
            acum_ref[0, g, rows, :] = acum[:, g * per:(g + 1) * per]
        dtt_ref[0, :, rows] = dtt
        acumt_ref[0, :, rows] = jnp.dot(dtt * at, upper, preferred_element_type=F32, precision=hp)


def _norm_small(x2d, norm_w, w_small, dt_bias, a_log, B, S, tm=512):
    T = x2d.shape[0]
    H, G = SSM_HEADS, SSM_GROUPS
    per = S // tm
    w_hi = w_small.astype(BF16)
    w_lo = (w_small - w_hi.astype(F32)).astype(BF16)
    const = lambda shape: pl.BlockSpec(shape, lambda i: (0,) * len(shape))
    head_major = pl.BlockSpec((1, H, tm), lambda i: (i // per, 0, i % per))
    return pl.pallas_call(
        _norm_small_kernel,
        grid=(T // tm,),
        in_specs=[pl.BlockSpec((tm, D_MODEL), lambda i: (i, 0)),
                  const((1, D_MODEL)), const((N_SMALL, D_MODEL)), const((N_SMALL, D_MODEL)),
                  const((1, H)), const((H, 1)), const((1, H)), const((H, 1))],
        out_specs=[pl.BlockSpec((tm, D_MODEL), lambda i: (i, 0)),
                   pl.BlockSpec((1, N_SMALL, tm), lambda i: (i // per, 0, i % per)),
                   pl.BlockSpec((1, G, tm, H // G), lambda i: (i // per, 0, i % per, 0)),
                   head_major, head_major],
        out_shape=[jax.ShapeDtypeStruct((T, D_MODEL), BF16),
                   jax.ShapeDtypeStruct((B, N_SMALL, S), F32),
                   jax.ShapeDtypeStruct((B, G, S, H // G), F32),
                   jax.ShapeDtypeStruct((B, H, S), F32),
                   jax.ShapeDtypeStruct((B, H, S), F32)],
        compiler_params=_cparams(("parallel",)),
        name="norm_small",
    )(x2d, norm_w, w_hi, w_lo, dt_bias.reshape(1, H), dt_bias.reshape(H, 1), a_log.reshape(1, H), a_log.reshape(H, 1))


def _proj_act_kernel(h_ref, wt_ref, o_ref, wb, *, act):
    o_ref[...] = act(_wide_dot(h_ref, wt_ref, wb)).astype(o_ref.dtype)


def _proj_conv_kernel(h_ref, wt_ref, cw_ref, cb_ref, o_ref, wb, buf, *, tiles_per_seq):
    m = pl.program_id(1)

    @pl.when(m % tiles_per_seq == 0)
    def _():
        buf[...] = jnp.zeros_like(buf)

    y = _causal_conv(_wide_dot(h_ref, wt_ref, wb), buf, cw_ref, cb_ref)
    o_ref[...] = _silu(y).astype(o_ref.dtype)


def _proj_rope_kernel(h_ref, wt_ref, ca_ref, cb_ref, cc_ref, o_ref, wb, *, scale):
    n = pl.program_id(0)
    d = ATTN_HEAD_DIM
    half = ROPE_DIM // 2
    is_q = n < ATTN_WIDTH // WTILE
    acc = _wide_dot(h_ref, wt_ref, wb)
    q_scale = jnp.where(is_q, scale, 1.0).astype(F32)
    ca, cb, cc = ca_ref[...] * q_scale, cb_ref[...] * q_scale, cc_ref[...] * q_scale

    def rope(hd):
        xh = acc[:, hd * d:(hd + 1) * d]
        up = pltpu.roll(xh, d - half, 1)
        dn = pltpu.roll(xh, half, 1)
        o_ref[:, hd * d:(hd + 1) * d] = (xh * ca + up * cb + dn * cc).astype(o_ref.dtype)

    kv_heads = KV_WIDTH // d
    for hd in range(kv_heads):
        rope(hd)

    @pl.when(is_q)
    def _():
        for hd in range(kv_heads, WTILE // d):
            rope(hd)

    @pl.when(jnp.logical_not(is_q))
    def _():
        o_ref[:, KV_WIDTH:] = acc[:, KV_WIDTH:].astype(o_ref.dtype)


def _rope_tables(S):
    half = ROPE_DIM // 2
    pos = jnp.arange(S)
    inv = ROPE_THETA ** (-(jnp.arange(half, dtype=F32) * 2.0 / ROPE_DIM))
    ang = pos.astype(F32)[:, None] * inv[None, :]
    cos, sin = jnp.cos(ang), jnp.sin(ang)
    rest = ATTN_HEAD_DIM - ROPE_DIM
    ca = jnp.concatenate([cos, cos, jnp.ones((S, rest), F32)], axis=1)
    cb = jnp.concatenate([-sin, jnp.zeros((S, ATTN_HEAD_DIM - half), F32)], axis=1)
    cc = jnp.concatenate([jnp.zeros((S, half), F32), sin, jnp.zeros((S, rest), F32)], axis=1)
    return ca, cb, cc


def _proj(kind, h, wt, row0, width, S, extra=(), tm=1024):
    T, K = h.shape
    tn = WTILE
    per = S // tm
    in_specs = [pl.BlockSpec((tm, K), lambda n, m: (m, 0)),
                pl.BlockSpec((pl.Element(tn), pl.Element(K)),
                             lambda n, m: (pl.multiple_of(row0 + n * tn, TAIL), 0))]
    scratch = [pltpu.VMEM((tn, K), BF16)]
    if kind == "silu":
        body = functools.partial(_proj_act_kernel, act=_silu)
    elif kind == "sigmoid":
        body = functools.partial(_proj_act_kernel, act=jax.nn.sigmoid)
    elif kind == "conv":
        body = functools.partial(_proj_conv_kernel, tiles_per_seq=per)
        taps = extra[0].shape[0]
        in_specs += [pl.BlockSpec((taps, tn), lambda n, m: (0, n)),
                     pl.BlockSpec((1, tn), lambda n, m: (0, n))]
        scratch += [pltpu.VMEM((TAIL, tn), F32)]
    else:
        body = functools.partial(_proj_rope_kernel, scale=math.log2(math.e) / math.sqrt(ATTN_HEAD_DIM))
        tab = pl.BlockSpec((tm, ATTN_HEAD_DIM), lambda n, m: (m % per, 0))
        in_specs += [tab, tab, tab]
    return pl.pallas_call(
        body,
        grid=(width // tn, T // tm),
        in_specs=in_specs,
        out_specs=pl.BlockSpec((tm, tn), lambda n, m: (m, n)),
        out_shape=jax.ShapeDtypeStruct((T, width), BF16),
        scratch_shapes=scratch,
        compiler_params=_cparams(("parallel", "arbitrary")),
        name="proj_" + kind,
    )(h, wt, *extra)


def _ssd_kernel(x_ref, b_ref, c_ref, z_ref, acum_ref, acumt_ref, dtt_ref, dsk_ref, nw_ref, y_ref, h_scr):
    @pl.when(pl.program_id(2) == 0)
    def _():
        h_scr[...] = jnp.zeros_like(h_scr)

    def chunk(j, carry):
        _ssd_chunk(pl.ds(pl.multiple_of(j * SSM_CHUNK, SSM_CHUNK), SSM_CHUNK),
                   x_ref, b_ref, c_ref, z_ref, acum_ref, acumt_ref, dtt_ref, dsk_ref, nw_ref, y_ref, h_scr)
        return carry

    lax.fori_loop(0, x_ref.shape[0] // SSM_CHUNK, chunk, 0)


def _ssd_chunk(rows, x_ref, b_ref, c_ref, z_ref, acum_ref, acumt_ref, dtt_ref, dsk_ref, nw_ref, y_ref, h_scr):
    L = SSM_CHUNK
    P = SSM_HEAD_DIM
    xb = x_ref[rows, :]
    bmb = b_ref[rows, :]
    cmb = c_ref[rows, :]
    n_heads = SSM_HEADS // SSM_GROUPS
    acum = acum_ref[0, 0, rows, :]
    acumT = acumt_ref[0, :, rows]
    dtT = dtt_ref[0, :, rows]
    ri = lax.broadcasted_iota(jnp.int32, (L, L), 0)
    ci = lax.broadcasted_iota(jnp.int32, (L, L), 1)
    tril = ci <= ri

    cb = lax.dot_general(cmb, bmb, (((1,), (1,)), ((), ())), preferred_element_type=F32)
    yoff = lax.dot_general(cmb, h_scr[...].astype(BF16), (((1,), (1,)), ((), ())),
                           preferred_element_type=F32)

    lane = lax.broadcasted_iota(jnp.int32, (L, 2 * P), 1)
    lo = lane < P

    def pair(v0, v1):
        return jnp.where(lo, v0, v1)

    dsk = dsk_ref[...]
    ys = []
    for pr in range(n_heads // 2):
        sl = slice(pr * 2 * P, (pr + 1) * 2 * P)
        xp = xb[:, sl]
        yd, ea = [], []
        for r in (2 * pr, 2 * pr + 1):
            acol = jnp.broadcast_to(acum[:, r:r + 1], (L, L))
            dec = jnp.exp(jnp.where(tril, acol - acumT[r:r + 1, :], -jnp.inf))
            yd.append(jnp.dot((cb * dec * dtT[r:r + 1, :]).astype(BF16), xp, preferred_element_type=F32))
            ea.append(jnp.exp(acol))
        ys.append(pair(yd[0], yd[1]) + yoff[:, sl] * pair(ea[0], ea[1]) + dsk[:, sl] * xp.astype(F32))

    a_last = acumT[:, L - 1:L]
    wT = dtT * jnp.exp(a_last - acumT)
    cdT = jnp.exp(a_last)
    xT = xb.astype(F32).T
    xds = jnp.concatenate([xT[r * P:(r + 1) * P, :] * wT[r:r + 1, :] for r in range(n_heads)], axis=0)
    cd = jnp.concatenate([jnp.broadcast_to(cdT[r:r + 1, :], (P, h_scr.shape[1])) for r in range(n_heads)], axis=0)
    st = jnp.dot(xds.astype(BF16), bmb, preferred_element_type=F32)
    h_scr[...] = h_scr[...] * cd + st

    y = jnp.concatenate(ys, axis=1) * z_ref[rows, :].astype(F32)
    y_ref[rows, :] = _rms(y, nw_ref[...]).astype(y_ref.dtype)


def _ssd(xbc, zs, acum_g, dt_t, acum_t, d_skip, norm_w, B, S, rows=1024):
    T = B * S
    L, G, N = rows, SSM_GROUPS, SSM_STATE
    R = SSM_HEADS // G
    gw = D_INNER // G
    nc = S // L
    dsk = jnp.repeat(d_skip, SSM_HEAD_DIM).reshape(1, D_INNER)
    nw = norm_w.reshape(1, D_INNER)

    row = lambda b, g, c: b * nc + c
    in_specs = [
        pl.BlockSpec((L, gw), lambda b, g, c: (row(b, g, c), g)),
        pl.BlockSpec((L, N), lambda b, g, c: (row(b, g, c), D_INNER // N + g)),
        pl.BlockSpec((L, N), lambda b, g, c: (row(b, g, c), (D_INNER + SSM_BC) // N + g)),
        pl.BlockSpec((L, gw), lambda b, g, c: (row(b, g, c), g)),
        pl.BlockSpec((1, 1, L, R), lambda b, g, c: (b, g, c, 0)),
        pl.BlockSpec((1, R, L), lambda b, g, c: (b, g, c)),
        pl.BlockSpec((1, R, L), lambda b, g, c: (b, g, c)),
        pl.BlockSpec((1, gw), lambda b, g, c: (0, g)),
        pl.BlockSpec((1, gw), lambda b, g, c: (0, g)),
    ]
    return pl.pallas_call(
        _ssd_kernel,
        grid=(B, G, nc),
        in_specs=in_specs,
        out_specs=pl.BlockSpec((L, gw), lambda b, g, c: (row(b, g, c), g)),
        out_shape=jax.ShapeDtypeStruct((T, D_INNER), BF16),
        scratch_shapes=[pltpu.VMEM((gw, N), F32)],
        compiler_params=_cparams(("parallel", "parallel", "arbitrary")),
        name="ssd",
    )(xbc, xbc, xbc, zs, acum_g, acum_t, dt_t, dsk, nw)


def _compress_kernel(k_ref, v_ref, pek_ref, pev_ref, wk1_ref, wv1_ref, wk2_ref, wv2_ref, kc_ref, vct_ref, xf):
    d = ATTN_HEAD_DIM
    nsub = xf.shape[0] // CMP_STRIDE

    def mlp(src_ref, pe_ref, w1_ref, w2_ref):
        xf[...] = src_ref[...].astype(F32)
        u = jnp.zeros((nsub, d), F32)
        v = jnp.zeros((nsub, d), F32)
        for l in range(CMP_STRIDE):
            tok = xf[pl.ds(l, nsub, stride=CMP_STRIDE), :]
            l2 = CMP_STRIDE + l
            u = u + jnp.dot((tok + pe_ref[l:l + 1, :]).astype(BF16), w1_ref[l * d:(l + 1) * d, :],
                            preferred_element_type=F32)
            v = v + jnp.dot((tok + pe_ref[l2:l2 + 1, :]).astype(BF16), w1_ref[l2 * d:(l2 + 1) * d, :],
                            preferred_element_type=F32)
        pre = u + pltpu.roll(v, nsub - 1, 0)
        return jnp.dot(_silu(pre).astype(BF16), w2_ref[...], preferred_element_type=F32)

    kc_ref[0, 0] = mlp(k_ref, pek_ref, wk1_ref, wk2_ref).astype(kc_ref.dtype)
    vct_ref[0, 0] = mlp(v_ref, pev_ref, wv1_ref, wv2_ref).T.astype(vct_ref.dtype)


def _compress(qkv, pe_k, pe_v, wk1, wv1, wk2, wv2, B, S):
    G, d = ATTN_KV_GROUPS, ATTN_HEAD_DIM
    nsub = S // CMP_STRIDE
    full = lambda shape: pl.BlockSpec(shape, lambda b, g: (0, 0))
    out = pl.BlockSpec((1, 1, nsub, d), lambda b, g: (b, g, 0, 0))
    return pl.pallas_call(
        _compress_kernel,
        grid=(B, G),
        in_specs=[pl.BlockSpec((S, d), lambda b, g: (b, QKV_KC + g)),
                  pl.BlockSpec((S, d), lambda b, g: (b, QKV_VC + g)),
                  full((CMP_BLOCK, d)), full((CMP_BLOCK, d)),
                  full((CMP_BLOCK * d, d)), full((CMP_BLOCK * d, d)),
                  full((d, d)), full((d, d))],
        out_specs=[out, out],
        out_shape=[jax.ShapeDtypeStruct((B, G, nsub, d), BF16),
                   jax.ShapeDtypeStruct((B, G, d, nsub), BF16)],
        scratch_shapes=[pltpu.VMEM((S, d), F32)],
        compiler_params=_cparams(("parallel", "parallel")),
        name="compress",
    )(qkv, qkv, pe_k, pe_v, wk1.astype(BF16), wv1.astype(BF16), wk2.astype(BF16), wv2.astype(BF16))


def _staged(n, scores, probs, values):
    for step in range(n + 2):
        if step < n:
            scores(step)
        if 1 <= step <= n:
            probs(step - 1)
        if step >= 2:
            values(step - 2)


def _nsa_kernel(q_ref, kc_ref, vct_ref, ks_ref, vs_ref, kw_ref, vw_ref, gt_ref, o_ref,
                vst, vwt, qt, m_scr, acc_scr, out_scr, s_scr, p_scr, a_scr, *, tq, tks, n_cmp, n_blk):
    i = pl.program_id(2)
    d = ATTN_HEAD_DIM
    R = ATTN_REP
    heads = [slice(r * tq, (r + 1) * tq) for r in range(R)]

    def transposed(v):
        return v.astype(F32).T.astype(BF16)

    @pl.when(i == 0)
    def _():
        for c in range(vst.shape[0]):
            vst[c, 0:d, :] = transposed(vs_ref[c * tks:(c + 1) * tks, :])
            vst[c, d:, :] = jnp.ones((ONES_ROWS, tks), BF16)
        vwt[0:d, :] = transposed(vw_ref[...])
        vwt[d:, :] = jnp.ones((ONES_ROWS, vwt.shape[1]), BF16)

    for r in range(R):
        qt[:, heads[r]] = transposed(q_ref[:, r * d:(r + 1) * d])
    gate = jax.nn.sigmoid(gt_ref[0])

    nk = kc_ref.shape[2]
    kc = kc_ref[0, 0]
    vct = vct_ref[0, 0]
    n_idx = lax.broadcasted_iota(jnp.int32, (nk, tq), 0)
    t_cmp = i * tq + lax.broadcasted_iota(jnp.int32, (nk, tq), 1)
    cmask = (n_idx * CMP_STRIDE + CMP_BLOCK - 1 <= t_cmp) & (n_idx < n_cmp)
    psum = jnp.zeros((nk, tq), F32)
    for r in range(R):
        s = jnp.where(cmask, jnp.dot(kc, qt[:, heads[r]], preferred_element_type=F32), NEG_INF)
        e = jnp.exp2(s - jnp.max(s, axis=0, keepdims=True))
        p = jnp.where(cmask, e * (1.0 / jnp.sum(e, axis=0, keepdims=True)), 0.0)
        out_scr[:, heads[r]] = gate[r:r + 1, :] * jnp.dot(vct, p.astype(BF16), preferred_element_type=F32)
        psum = psum + p
    bi = lax.broadcasted_iota(jnp.int32, (n_blk, nk), 0)
    ni = lax.broadcasted_iota(jnp.int32, (n_blk, nk), 1)
    ovl = ((ni * CMP_STRIDE < bi * SEL_BLOCK + SEL_BLOCK) & (ni * CMP_STRIDE + CMP_BLOCK > bi * SEL_BLOCK)
           & (ni < n_cmp))
    imp = jnp.dot(jnp.where(ovl, 1.0, 0.0), psum, preferred_element_type=F32, precision=lax.Precision.HIGHEST)
    blk = lax.broadcasted_iota(jnp.int32, (n_blk, tq), 0)
    cur = (i * tq + lax.broadcasted_iota(jnp.int32, (n_blk, tq), 1)) // SEL_BLOCK
    forced = (blk == 0) | (blk == cur) | (blk == cur - 1)
    score = jnp.where(forced, FORCE_SCORE, jnp.where(blk <= cur, imp, NEG_INF))
    rank = jnp.zeros((n_blk, tq), F32)
    for k in range(n_blk):
        sk = score[k:k + 1, :]
        rank = rank + jnp.where((sk > score) | ((sk == score) & (k < blk)), 1.0, 0.0)
    sel = jnp.where(rank < N_SEL, 1.0, 0.0).astype(BF16)

    m_scr[...] = jnp.full(m_scr.shape, NEG_INF, F32)
    acc_scr[...] = jnp.zeros_like(acc_scr)
    n_chunks = (i * tq + tq - 1) // tks + 1

    def sel_inputs(c):
        k0 = pl.multiple_of(c * tks, tks)
        key = k0 + lax.broadcasted_iota(jnp.int32, (tks, tq), 0)
        t = i * tq + lax.broadcasted_iota(jnp.int32, (tks, tq), 1)
        in_blk = (lax.broadcasted_iota(jnp.int32, (tks, n_blk), 1)
                  == (k0 + lax.broadcasted_iota(jnp.int32, (tks, n_blk), 0)) // SEL_BLOCK)
        picked = jnp.dot(jnp.where(in_blk, 1.0, 0.0).astype(BF16), sel, preferred_element_type=F32)
        return ks_ref[pl.ds(k0, tks), :], jnp.where((picked > 0.5) & (key <= t), 0.0, NEG_INF)

    def sel_scores(r, k_chunk, bias):
        s = jnp.dot(k_chunk, qt[:, heads[r]], preferred_element_type=F32) + bias
        s_scr[r, 0:tks, :] = s
        m_prev = m_scr[r:r + 1, :]
        m_new = jnp.maximum(m_prev, jnp.max(s, axis=0, keepdims=True))
        m_scr[r:r + 1, :] = m_new
        return jnp.exp2(m_prev - m_new)

    def sel_probs(r):
        p_scr[r, 0:tks, :] = jnp.exp2(s_scr[r, 0:tks, :] - m_scr[r:r + 1, :]).astype(BF16)

    def sel_values(r, vt_chunk):
        acc_scr[:, heads[r]] = (a_scr[r:r + 1, :] * acc_scr[:, heads[r]]
                                + jnp.dot(vt_chunk, p_scr[r, 0:tks, :], preferred_element_type=F32))

    k_chunk, bias = sel_inputs(0)
    for r in range(R):
        a_scr[r:r + 1, :] = sel_scores(r, k_chunk, bias)

    def sel_body(c, carry):
        k_chunk, bias = sel_inputs(c)
        vt_prev = vst[c - 1]
        sel_probs(0)
        for r in range(R):
            alpha = sel_scores(r, k_chunk, bias)
            if r + 1 < R:
                sel_probs(r + 1)
            sel_values(r, vt_prev)
            a_scr[r:r + 1, :] = alpha
        return carry

    lax.fori_loop(1, n_chunks, sel_body, 0)
    span = WINDOW + tq
    k0 = pl.multiple_of(jnp.maximum(i * tq - WINDOW, 0), tq)
    dist = (i * tq + lax.broadcasted_iota(jnp.int32, (span, tq), 1)
            - (k0 + lax.broadcasted_iota(jnp.int32, (span, tq), 0)))
    bias = jnp.where((dist >= 0) & (dist < WINDOW), 0.0, NEG_INF)
    k_win = kw_ref[pl.ds(k0, span), :]
    vt_win = vwt[:, pl.ds(k0, span)]
    mx = [None] * R

    def scores(r):
        s = jnp.dot(k_win, qt[:, heads[r]], preferred_element_type=F32) + bias
        s_scr[r] = s
        mx[r] = jnp.max(s, axis=0, keepdims=True)

    def probs(r):
        p_scr[r] = jnp.exp2(s_scr[r] - mx[r]).astype(BF16)

    def values(r):
        o = jnp.dot(vt_win, p_scr[r], preferred_element_type=F32)
        out_scr[:, heads[r]] += gate[2 * R + r:2 * R + r + 1, :] * (o[0:d, :] * (1.0 / o[d:d + 1, :]))

    vt_last = vst[n_chunks - 1]
    sel_probs(0)
    for r in range(R):
        if r + 1 < R:
            sel_probs(r + 1)
        sel_values(r, vt_last)
        scores(r)
    for r in range(R):
        o = acc_scr[0:d, heads[r]] * (1.0 / acc_scr[d:d + 1, heads[r]])
        out_scr[:, heads[r]] += gate[R + r:R + r + 1, :] * o
    probs(0)
    for r in range(R):
        if r + 1 < R:
            probs(r + 1)
        values(r)

    for r in range(R):
        o_ref[:, r * d:(r + 1) * d] = out_scr[:, heads[r]].T.astype(o_ref.dtype)


def _nsa(qkv, kc, vct, small_t, B, S, tq=256, tks=512):
    assert WINDOW % tq == 0 and S >= WINDOW + tq
    T = B * S
    G, d, R = ATTN_KV_GROUPS, ATTN_HEAD_DIM, ATTN_REP
    nq = S // tq
    nk = kc.shape[2]
    n_cmp = S // CMP_STRIDE - CMP_BLOCK // CMP_STRIDE + 1
    n_blk = S // SEL_BLOCK
    seq = lambda c0: pl.BlockSpec((S, d), lambda b, g, i: (b, c0 + g))
    cmp_spec = pl.BlockSpec((1, 1, nk, d), lambda b, g, i: (b, g, 0, 0))
    gate0 = SSM_HEADS // GATE_ROWS
    return pl.pallas_call(
        functools.partial(_nsa_kernel, tq=tq, tks=tks, n_cmp=n_cmp, n_blk=n_blk),
        grid=(B, G, nq),
        in_specs=[pl.BlockSpec((tq, KV_WIDTH), lambda b, g, i: (b * nq + i, g)),
                  cmp_spec, cmp_spec,
                  seq(QKV_KS), seq(QKV_VS), seq(QKV_KW), seq(QKV_VW),
                  pl.BlockSpec((1, GATE_ROWS, tq), lambda b, g, i: (b, gate0 + g, i))],
        out_specs=pl.BlockSpec((tq, KV_WIDTH), lambda b, g, i: (b * nq + i, g)),
        out_shape=jax.ShapeDtypeStruct((T, ATTN_WIDTH), BF16),
        scratch_shapes=[pltpu.VMEM((S // tks, d + ONES_ROWS, tks), BF16),
                        pltpu.VMEM((d + ONES_ROWS, S), BF16),
                        pltpu.VMEM((d, R * tq), BF16),
                        pltpu.VMEM((8, tq), F32),
                        pltpu.VMEM((d + ONES_ROWS, R * tq), F32),
                        pltpu.VMEM((d, R * tq), F32),
                        pltpu.VMEM((R, WINDOW + tq, tq), F32),
                        pltpu.VMEM((R, WINDOW + tq, tq), BF16),
                        pltpu.VMEM((8, tq), F32)],
        compiler_params=_cparams(("parallel", "parallel", "arbitrary")),
        name="nsa",
    )(qkv, kc, vct, qkv, qkv, qkv, qkv, small_t)


def _merge_kernel(ys_ref, ws_ref, ya_ref, wa_ref, gs_ref, ga_ref, o_ref, wsb, wab):
    _cast_per_column_tile(((ws_ref, wsb), (wa_ref, wab)))
    s = jnp.dot(ys_ref[...], wsb[...], preferred_element_type=F32)
    a = jnp.dot(ya_ref[...], wab[...], preferred_element_type=F32)
    o_ref[...] = (gs_ref[...].astype(F32) * s + ga_ref[...].astype(F32) * a).astype(o_ref.dtype)


def _merge(y_ssm, w_s, y_attn, w_a, gm, tm=512, tn=512):
    T = y_ssm.shape[0]
    return pl.pallas_call(
        _merge_kernel,
        grid=(D_MODEL // tn, T // tm),
        in_specs=[pl.BlockSpec((tm, D_INNER), lambda n, m: (m, 0)),
                  pl.BlockSpec((D_INNER, tn), lambda n, m: (0, n)),
                  pl.BlockSpec((tm, ATTN_WIDTH), lambda n, m: (m, 0)),
                  pl.BlockSpec((ATTN_WIDTH, tn), lambda n, m: (0, n)),
                  pl.BlockSpec((tm, tn), lambda n, m: (m, n)),
                  pl.BlockSpec((tm, tn), lambda n, m: (m, D_MODEL // tn + n))],
        out_specs=pl.BlockSpec((tm, tn), lambda n, m: (m, n)),
        out_shape=jax.ShapeDtypeStruct((T, D_MODEL), BF16),
        scratch_shapes=[pltpu.VMEM((D_INNER, tn), BF16), pltpu.VMEM((ATTN_WIDTH, tn), BF16)],
        compiler_params=_cparams(("parallel", "arbitrary")),
        name="merge",
    )(y_ssm, w_s, y_attn, w_a, gm, gm)


def _cast_once(src, dst):
    @pl.when(pl.program_id(0) == 0)
    def _():
        dst[...] = src[...].astype(BF16)


def _mix_out_kernel(mg_ref, w_ref, x_ref, nw_ref, x1_ref, h_ref, wb):
    _cast_once(w_ref, wb)
    x1 = x_ref[...] + jnp.dot(mg_ref[...], wb[...], preferred_element_type=F32)
    x1_ref[...] = x1
    h_ref[...] = _rms(x1, nw_ref[...]).astype(h_ref.dtype)


def _mix_out(merged, w, x2d, norm_w, tm=512):
    T = x2d.shape[0]
    return pl.pallas_call(
        _mix_out_kernel,
        grid=(T // tm,),
        in_specs=[pl.BlockSpec((tm, D_MODEL), lambda m: (m, 0)),
                  _resident((D_MODEL, D_MODEL)),
                  pl.BlockSpec((tm, D_MODEL), lambda m: (m, 0)),
                  pl.BlockSpec((1, D_MODEL), lambda m: (0, 0))],
        out_specs=[pl.BlockSpec((tm, D_MODEL), lambda m: (m, 0)),
                   pl.BlockSpec((tm, D_MODEL), lambda m: (m, 0))],
        out_shape=[jax.ShapeDtypeStruct((T, D_MODEL), F32),
                   jax.ShapeDtypeStruct((T, D_MODEL), BF16)],
        scratch_shapes=[pltpu.VMEM((D_MODEL, D_MODEL), BF16)],
        compiler_params=_cparams(("arbitrary",)),
        name="mix_out",
    )(merged, w, x2d, norm_w)


def _ffn_up_kernel(h_ref, wg_ref, wu_ref, cw_ref, cb_ref, o_ref, gbuf, wgb, wub, *, tiles_per_seq):
    m = pl.program_id(1)
    _cast_per_column_tile(((wg_ref, wgb), (wu_ref, wub)))

    @pl.when(m % tiles_per_seq == 0)
    def _():
        gbuf[...] = jnp.zeros_like(gbuf)

    h = h_ref[...]
    g = jnp.dot(h, wgb[...], preferred_element_type=F32)
    u = jnp.dot(h, wub[...], preferred_element_type=F32)
    o_ref[...] = (_silu(_causal_conv(g, gbuf, cw_ref, cb_ref)) * u).astype(o_ref.dtype)


def _ffn_up(h, wg, wu, conv_w, conv_b, S, tm=1024, tn=512):
    T = h.shape[0]
    return pl.pallas_call(
        functools.partial(_ffn_up_kernel, tiles_per_seq=S // tm),
        grid=(D_FF // tn, T // tm),
        in_specs=[pl.BlockSpec((tm, D_MODEL), lambda n, m: (m, 0)),
                  pl.BlockSpec((D_MODEL, tn), lambda n, m: (0, n)),
                  pl.BlockSpec((D_MODEL, tn), lambda n, m: (0, n)),
                  pl.BlockSpec((FFN_CONV, tn), lambda n, m: (0, n)),
                  pl.BlockSpec((1, tn), lambda n, m: (0, n))],
        out_specs=pl.BlockSpec((tm, tn), lambda n, m: (m, n)),
        out_shape=jax.ShapeDtypeStruct((T, D_FF), BF16),
        scratch_shapes=[pltpu.VMEM((TAIL, tn), F32),
                        pltpu.VMEM((D_MODEL, tn), BF16), pltpu.VMEM((D_MODEL, tn), BF16)],
        compiler_params=_cparams(("parallel", "arbitrary")),
        name="ffn_up",
    )(h, wg, wu, conv_w, conv_b.reshape(1, D_FF))


def _ffn_down_kernel(a_ref, w_ref, x_ref, nw_ref, x2_ref, h_ref):
    x2 = x_ref[...] + jnp.dot(a_ref[...], w_ref[...], preferred_element_type=F32)
    x2_ref[...] = x2
    h_ref[...] = _rms(x2, nw_ref[...]).astype(h_ref.dtype)


def _ffn_down(act, w, x1, norm_w, tm=256):
    T = x1.shape[0]
    return pl.pallas_call(
        _ffn_down_kernel,
        grid=(T // tm,),
        in_specs=[pl.BlockSpec((tm, D_FF), lambda m: (m, 0)),
                  _resident((D_FF, D_MODEL)),
                  pl.BlockSpec((tm, D_MODEL), lambda m: (m, 0)),
                  _resident((1, D_MODEL))],
        out_specs=[pl.BlockSpec((tm, D_MODEL), lambda m: (m, 0)),
                   pl.BlockSpec((tm, D_MODEL), lambda m: (m, 0))],
        out_shape=[jax.ShapeDtypeStruct((T, D_MODEL), F32),
                   jax.ShapeDtypeStruct((T, D_MODEL), BF16)],
        compiler_params=_cparams(("parallel",)),
        name="ffn_down",
    )(act, w, x1, norm_w)


def _ple_kernel(h_ref, wg_ref, p_ref, wp_ref, x_ref, nw_ref, o_ref, wgb):
    _cast_once(wg_ref, wgb)
    gate = jax.nn.sigmoid(jnp.dot(h_ref[...], wgb[...], preferred_element_type=F32))
    emb = jnp.dot(p_ref[...].astype(BF16), wp_ref[...], preferred_element_type=F32)
    o_ref[...] = _rms(x_ref[...] + gate * emb, nw_ref[...])


def _ple(h, wg, p2d, wp, x2, norm_w, tm=512):
    T = x2.shape[0]
    return pl.pallas_call(
        _ple_kernel,
        grid=(T // tm,),
        in_specs=[pl.BlockSpec((tm, D_MODEL), lambda m: (m, 0)),
                  _resident((D_MODEL, D_MODEL)),
                  pl.BlockSpec((tm, PLE_DIM), lambda m: (m, 0)),
                  _resident((PLE_DIM, D_MODEL)),
                  pl.BlockSpec((tm, D_MODEL), lambda m: (m, 0)),
                  pl.BlockSpec((1, D_MODEL), lambda m: (0, 0))],
        out_specs=pl.BlockSpec((tm, D_MODEL), lambda m: (m, 0)),
        out_shape=jax.ShapeDtypeStruct((T, D_MODEL), F32),
        scratch_shapes=[pltpu.VMEM((D_MODEL, D_MODEL), BF16)],
        compiler_params=_cparams(("arbitrary",)),
        name="ple",
    )(h, wg, p2d, wp, x2, norm_w)


def _mixer_inputs(x2d, norm_w, w, ssm_conv_w, ssm_conv_b, dt_bias, a_log, B, S):
    wt = w.T
    h, small_t, acum_g, dt_t, acum_t = _norm_small(x2d, norm_w.reshape(1, D_MODEL), _small_weight(wt),
                                                   dt_bias, a_log, B, S)
    zs = _proj("silu", h, wt, 0, D_INNER, S)
    xbc = _proj("conv", h, wt, D_INNER, SSM_CONV_DIM, S,
                extra=(ssm_conv_w, ssm_conv_b.reshape(1, SSM_CONV_DIM)))
    qkv = _proj("rope", h, wt, NAT_Q, QKV_WIDTH, S, extra=_rope_tables(S))
    gm = _proj("sigmoid", h, wt, NAT_GM, 2 * D_MODEL, S)
    return small_t, (acum_g, dt_t, acum_t), zs, xbc, qkv, gm


def _layer(x2d, p2d, B, S, norm_mix_w, w_in, ssm_conv_w, ssm_conv_b, ssm_dt_bias, ssm_a_log, ssm_d,
           ssm_norm_w, cmp_pe_k, cmp_pe_v, cmp_wk1, cmp_wk2, cmp_wv1, cmp_wv2,
           w_ssm_branch, w_attn_branch, w_mix_out, norm_ffn_w, ffn_w_gate, ffn_w_up,
           ffn_conv_w, ffn_conv_b, ffn_w_down, ple_norm_w, ple_w_gate, ple_w_proj, out_norm_w):
    small_t, scalars, zs, xbc, qkv, gm = _mixer_inputs(x2d, norm_mix_w, w_in, ssm_conv_w, ssm_conv_b,
                                                       ssm_dt_bias, ssm_a_log, B, S)
    y_ssm = _ssd(xbc, zs, *scalars, ssm_d, ssm_norm_w, B, S)
    kc, vct = _compress(qkv, cmp_pe_k, cmp_pe_v, cmp_wk1, cmp_wv1, cmp_wk2, cmp_wv2, B, S)
    y_attn = _nsa(qkv, kc, vct, small_t, B, S)

    merged = _merge(y_ssm, w_ssm_branch, y_attn, w_attn_branch, gm)
    x1, h2 = _mix_out(merged, w_mix_out, x2d, norm_ffn_w.reshape(1, D_MODEL))

    act = _ffn_up(h2, ffn_w_gate, ffn_w_up, ffn_conv_w, ffn_conv_b, S)
    x2, h3 = _ffn_down(act, ffn_w_down.astype(BF16), x1, ple_norm_w.reshape(1, D_MODEL))
    return _ple(h3, ple_w_gate, p2d, ple_w_proj.astype(BF16), x2, out_norm_w.reshape(1, D_MODEL))


def kernel(x, p, norm_mix_w, w_in, ssm_conv_w, ssm_conv_b, ssm_dt_bias, ssm_a_log, ssm_d, ssm_norm_w, cmp_pe_k, cmp_pe_v, cmp_wk1, cmp_wk2, cmp_wv1, cmp_wv2, w_ssm_branch, w_attn_branch, w_mix_out, norm_ffn_w, ffn_w_gate, ffn_w_up, ffn_conv_w, ffn_conv_b, ffn_w_down, ple_norm_w, ple_w_gate, ple_w_proj, final_norm_w):
    B, S, D = x.shape
    depth = w_in.shape[0]
    assert depth == 1, "the final norm is fused into the (single) layer's last kernel"
    x2d = x.reshape(B * S, D)
    out = _layer(x2d, p[0].reshape(B * S, PLE_DIM), B, S, norm_mix_w[0], w_in[0], ssm_conv_w[0], ssm_conv_b[0],
                 ssm_dt_bias[0], ssm_a_log[0], ssm_d[0], ssm_norm_w[0], cmp_pe_k[0], cmp_pe_v[0],
                 cmp_wk1[0], cmp_wk2[0], cmp_wv1[0], cmp_wv2[0], w_ssm_branch[0], w_attn_branch[0],
                 w_mix_out[0], norm_ffn_w[0], ffn_w_gate[0], ffn_w_up[0], ffn_conv_w[0], ffn_conv_b[0],
                 ffn_w_down[0], ple_norm_w[0], ple_w_gate[0], ple_w_proj[0], final_norm_w)
    return out.reshape(B, S, D)
```

```python
import functools
import math

import jax
import jax.numpy as jnp
from jax import lax
from jax.experimental import pallas as pl
from jax.experimental.pallas import tpu as pltpu

F32 = jnp.float32
BF16 = jnp.bfloat16

D_MODEL = 2048
PLE_DIM = 256
EPS = 1e-6
D_INNER = 4096
SSM_HEAD_DIM = 64
SSM_HEADS = 64
SSM_GROUPS = 8
SSM_STATE = 128
SSM_CONV = 4
SSM_CHUNK = 128
SSM_BC = SSM_GROUPS * SSM_STATE
SSM_CONV_DIM = D_INNER + 2 * SSM_BC
ATTN_HEADS = 16
ATTN_HEAD_DIM = 128
ATTN_KV_GROUPS = 4
ATTN_REP = ATTN_HEADS // ATTN_KV_GROUPS
ATTN_WIDTH = ATTN_HEADS * ATTN_HEAD_DIM
KV_WIDTH = ATTN_KV_GROUPS * ATTN_HEAD_DIM
CMP_BLOCK = 32
CMP_STRIDE = 16
SEL_BLOCK = 64
N_SEL = 16
WINDOW = 512
ROPE_THETA = 500000.0
ROPE_DIM = ATTN_HEAD_DIM // 4
D_FF = 5632
FFN_CONV = 3
NEG_INF = -1e30
FORCE_SCORE = 1e4

LANES = 128
TAIL = 8
VMEM_LIMIT = 56 * 1024 * 1024

NAT_DT = D_INNER + SSM_CONV_DIM
NAT_Q = NAT_DT + SSM_HEADS
NAT_GN = NAT_Q + ATTN_WIDTH + 6 * KV_WIDTH
NAT_GM = NAT_GN + 3 * ATTN_HEADS
QKV_WIDTH = ATTN_WIDTH + 6 * KV_WIDTH
WTILE = 1024
N_SMALL = 128
GATE_ROWS = 16
ONES_ROWS = 16
QKV_KC, QKV_VC, QKV_KS, QKV_VS, QKV_KW, QKV_VW = (
    (ATTN_WIDTH + i * KV_WIDTH) // ATTN_HEAD_DIM for i in range(6))


def _cparams(sem):
    return pltpu.CompilerParams(dimension_semantics=sem, vmem_limit_bytes=VMEM_LIMIT)


def _resident(shape):
    return pl.BlockSpec(shape, lambda *_: (0,) * len(shape), pipeline_mode=pl.Buffered(1))


def _rms(xf, w):
    return xf * lax.rsqrt(jnp.mean(xf * xf, axis=-1, keepdims=True) + EPS) * w


def _silu(v):
    return v * jax.nn.sigmoid(v)


def _causal_conv(x, tail, w_ref, bias_ref):
    taps = w_ref.shape[0]
    rows, n = x.shape
    x3 = x.reshape(rows // TAIL, TAIL, n)
    t3 = tail[...].reshape(1, TAIL, n)
    sub = lax.broadcasted_iota(jnp.int32, (1, TAIL, n), 1)
    y = bias_ref[...].reshape(1, 1, n) + w_ref[taps - 1:taps, :].reshape(1, 1, n) * x3
    for s in range(1, taps):
        cur = pltpu.roll(x3, s, 1)
        above = jnp.concatenate([pltpu.roll(t3, s, 1), cur[:-1]], axis=0)
        y = y + w_ref[taps - 1 - s:taps - s, :].reshape(1, 1, n) * jnp.where(sub < s, above, cur)
    tail[...] = x[rows - TAIL:, :]
    return y.reshape(rows, n)


def _cast_per_column_tile(pairs):
    @pl.when(pl.program_id(1) == 0)
    def _():
        for src, dst in pairs:
            dst[...] = src[...].astype(BF16)


def _wide_dot(h_ref, wt_ref, wb):
    _cast_per_column_tile(((wt_ref, wb),))
    return lax.dot_general(h_ref[...], wb[...], (((1,), (1,)), ((), ())), preferred_element_type=F32)


def _small_weight(wt):
    G, R = ATTN_KV_GROUPS, ATTN_REP
    K = wt.shape[1]
    w_dt = wt[NAT_DT:NAT_DT + SSM_HEADS]
    w_g = wt[NAT_GN:NAT_GN + 3 * ATTN_HEADS].reshape(3, G, R, K).transpose(1, 0, 2, 3).reshape(G, 3 * R, K)
    w_g = jnp.pad(w_g, ((0, 0), (0, GATE_ROWS - 3 * R), (0, 0))).reshape(G * GATE_ROWS, K)
    return jnp.concatenate([w_dt, w_g], axis=0)


def _norm_small_kernel(x_ref, nw_ref, whi_ref, wlo_ref, dtb_ref, dtbt_ref, alog_ref, alogt_ref,
                       h_ref, smallt_ref, acum_ref, dtt_ref, acumt_ref):
    L, H = SSM_CHUNK, SSM_HEADS
    nt = (((1,), (1,)), ((), ()))
    h = _rms(x_ref[...], nw_ref[...])
    h_hi = h.astype(BF16)
    h_lo = (h - h_hi.astype(F32)).astype(BF16)
    h_ref[...] = h_hi
    small = (lax.dot_general(h_hi, whi_ref[...], nt, preferred_element_type=F32)
             + lax.dot_general(h_hi, wlo_ref[...], nt, preferred_element_type=F32)
             + lax.dot_general(h_lo, whi_ref[...], nt, preferred_element_type=F32))
    small_t = small.T
    smallt_ref[0] = small_t

    ri = lax.broadcasted_iota(jnp.int32, (L, L), 0)
    ci = lax.broadcasted_iota(jnp.int32, (L, L), 1)
    lower = jnp.where(ci <= ri, 1.0, 0.0)
    upper = jnp.where(ri <= ci, 1.0, 0.0)
    hp = lax.Precision.HIGHEST
    a = -jnp.exp(alog_ref[...])
    at = -jnp.exp(alogt_ref[...])
    per = H // SSM_GROUPS
    for j in range(small.shape[0] // L):
        rows = slice(j * L, (j + 1) * L)
        dt = jax.nn.softplus(small[rows, 0:H] + dtb_ref[...])
        dtt = jax.nn.softplus(small_t[0:H, rows] + dtbt_ref[...])
        acum = jnp.dot(lower, dt * a, preferred_element_type=F32, precision=hp)
        for g in range(SSM_GROUPS):
            acum_ref[0, g, rows, :] = acum[:, g * per:(g + 1) * per]
        dtt_ref[0, :, rows] = dtt
        acumt_ref[0, :, rows] = jnp.dot(dtt * at, upper, preferred_element_type=F32, precision=hp)


def _norm_small(x2d, norm_w, w_small, dt_bias, a_log, B, S, tm=512):
    T = x2d.shape[0]
    H, G = SSM_HEADS, SSM_GROUPS
    per = S // tm
    w_hi = w_small.astype(BF16)
    w_lo = (w_small - w_hi.astype(F32)).astype(BF16)
    const = lambda shape: pl.BlockSpec(shape, lambda i: (0,) * len(shape))
    head_major = pl.BlockSpec((1, H, tm), lambda i: (i // per, 0, i % per))
    return pl.pallas_call(
        _norm_small_kernel,
        grid=(T // tm,),
        in_specs=[pl.BlockSpec((tm, D_MODEL), lambda i: (i, 0)),
                  const((1, D_MODEL)), const((N_SMALL, D_MODEL)), const((N_SMALL, D_MODEL)),
                  const((1, H)), const((H, 1)), const((1, H)), const((H, 1))],
        out_specs=[pl.BlockSpec((tm, D_MODEL), lambda i: (i, 0)),
                   pl.BlockSpec((1, N_SMALL, tm), lambda i: (i // per, 0, i % per)),
                   pl.BlockSpec((1, G, tm, H // G), lambda i: (i // per, 0, i % per, 0)),
                   head_major, head_major],
        out_shape=[jax.ShapeDtypeStruct((T, D_MODEL), BF16),
                   jax.ShapeDtypeStruct((B, N_SMALL, S), F32),
                   jax.ShapeDtypeStruct((B, G, S, H // G), F32),
                   jax.ShapeDtypeStruct((B, H, S), F32),
                   jax.ShapeDtypeStruct((B, H, S), F32)],
        compiler_params=_cparams(("parallel",)),
        name="norm_small",
    )(x2d, norm_w, w_hi, w_lo, dt_bias.reshape(1, H), dt_bias.reshape(H, 1), a_log.reshape(1, H), a_log.reshape(H, 1))


def _proj_act_kernel(h_ref, wt_ref, o_ref, wb, *, act):
    o_ref[...] = act(_wide_dot(h_ref, wt_ref, wb)).astype(o_ref.dtype)


def _proj_conv_kernel(h_ref, wt_ref, cw_ref, cb_ref, o_ref, wb, buf, *, tiles_per_seq):
    m = pl.program_id(1)

    @pl.when(m % tiles_per_seq == 0)
    def _():
        buf[...] = jnp.zeros_like(buf)

    y = _causal_conv(_wide_dot(h_ref, wt_ref, wb), buf, cw_ref, cb_ref)
    o_ref[...] = _silu(y).astype(o_ref.dtype)


def _proj_rope_kernel(h_ref, wt_ref, ca_ref, cb_ref, cc_ref, o_ref, wb, *, scale):
    n = pl.program_id(0)
    d = ATTN_HEAD_DIM
    half = ROPE_DIM // 2
    is_q = n < ATTN_WIDTH // WTILE
    acc = _wide_dot(h_ref, wt_ref, wb)
    q_scale = jnp.where(is_q, scale, 1.0).astype(F32)
    ca, cb, cc = ca_ref[...] * q_scale, cb_ref[...] * q_scale, cc_ref[...] * q_scale

    def rope(hd):
        xh = acc[:, hd * d:(hd + 1) * d]
        up = pltpu.roll(xh, d - half, 1)
        dn = pltpu.roll(xh, half, 1)
        o_ref[:, hd * d:(hd + 1) * d] = (xh * ca + up * cb + dn * cc).astype(o_ref.dtype)

    kv_heads = KV_WIDTH // d
    for hd in range(kv_heads):
        rope(hd)

    @pl.when(is_q)
    def _():
        for hd in range(kv_heads, WTILE // d):
            rope(hd)

    @pl.when(jnp.logical_not(is_q))
    def _():
        o_ref[:, KV_WIDTH:] = acc[:, KV_WIDTH:].astype(o_ref.dtype)


def _rope_tables(S):
    half = ROPE_DIM // 2
    pos = jnp.arange(S)
    inv = ROPE_THETA ** (-(jnp.arange(half, dtype=F32) * 2.0 / ROPE_DIM))
    ang = pos.astype(F32)[:, None] * inv[None, :]
    cos, sin = jnp.cos(ang), jnp.sin(ang)
    rest = ATTN_HEAD_DIM - ROPE_DIM
    ca = jnp.concatenate([cos, cos, jnp.ones((S, rest), F32)], axis=1)
    cb = jnp.concatenate([-sin, jnp.zeros((S, ATTN_HEAD_DIM - half), F32)], axis=1)
    cc = jnp.concatenate([jnp.zeros((S, half), F32), sin, jnp.zeros((S, rest), F32)], axis=1)
    return ca, cb, cc


def _proj(kind, h, wt, row0, width, S, extra=(), tm=1024):
    T, K = h.shape
    tn = WTILE
    per = S // tm
    in_specs = [pl.BlockSpec((tm, K), lambda n, m: (m, 0)),
                pl.BlockSpec((pl.Element(tn), pl.Element(K)),
                             lambda n, m: (pl.multiple_of(row0 + n * tn, TAIL), 0))]
    scratch = [pltpu.VMEM((tn, K), BF16)]
    if kind == "silu":
        body = functools.partial(_proj_act_kernel, act=_silu)
    elif kind == "sigmoid":
        body = functools.partial(_proj_act_kernel, act=jax.nn.sigmoid)
    elif kind == "conv":
        body = functools.partial(_proj_conv_kernel, tiles_per_seq=per)
        taps = extra[0].shape[0]
        in_specs += [pl.BlockSpec((taps, tn), lambda n, m: (0, n)),
                     pl.BlockSpec((1, tn), lambda n, m: (0, n))]
        scratch += [pltpu.VMEM((TAIL, tn), F32)]
    else:
        body = functools.partial(_proj_rope_kernel, scale=math.log2(math.e) / math.sqrt(ATTN_HEAD_DIM))
        tab = pl.BlockSpec((tm, ATTN_HEAD_DIM), lambda n, m: (m % per, 0))
        in_specs += [tab, tab, tab]
    return pl.pallas_call(
        body,
        grid=(width // tn, T // tm),
        in_specs=in_specs,
        out_specs=pl.BlockSpec((tm, tn), lambda n, m: (m, n)),
        out_shape=jax.ShapeDtypeStruct((T, width), BF16),
        scratch_shapes=scratch,
        compiler_params=_cparams(("parallel", "arbitrary")),
        name="proj_" + kind,
    )(h, wt, *extra)


def _ssd_kernel(x_ref, b_ref, c_ref, z_ref, acum_ref, acumt_ref, dtt_ref, dsk_ref, nw_ref, y_ref, h_scr):
    @pl.when(pl.program_id(2) == 0)
    def _():
        h_scr[...] = jnp.zeros_like(h_scr)

    def chunk(j, carry):
        _ssd_chunk(pl.ds(pl.multiple_of(j * SSM_CHUNK, SSM_CHUNK), SSM_CHUNK),
                   x_ref, b_ref, c_ref, z_ref, acum_ref, acumt_ref, dtt_ref, dsk_ref, nw_ref, y_ref, h_scr)
        return carry

    lax.fori_loop(0, x_ref.shape[0] // SSM_CHUNK, chunk, 0)


def _ssd_chunk(rows, x_ref, b_ref, c_ref, z_ref, acum_ref, acumt_ref, dtt_ref, dsk_ref, nw_ref, y_ref, h_scr):
    L = SSM_CHUNK
    P = SSM_HEAD_DIM
    xb = x_ref[rows, :]
    bmb = b_ref[rows, :]
    cmb = c_ref[rows, :]
    n_heads = SSM_HEADS // SSM_GROUPS
    acum = acum_ref[0, 0, rows, :]
    acumT = acumt_ref[0, :, rows]
    dtT = dtt_ref[0, :, rows]
    ri = lax.broadcasted_iota(jnp.int32, (L, L), 0)
    ci = lax.broadcasted_iota(jnp.int32, (L, L), 1)
    tril = ci <= ri

    cb = lax.dot_general(cmb, bmb, (((1,), (1,)), ((), ())), preferred_element_type=F32)
    yoff = lax.dot_general(cmb, h_scr[...].astype(BF16), (((1,), (1,)), ((), ())),
                           preferred_element_type=F32)

    lane = lax.broadcasted_iota(jnp.int32, (L, 2 * P), 1)
    lo = lane < P

    def pair(v0, v1):
        return jnp.where(lo, v0, v1)

    dsk = dsk_ref[...]
    ys = []
    for pr in range(n_heads // 2):
        sl = slice(pr * 2 * P, (pr + 1) * 2 * P)
        xp = xb[:, sl]
        yd, ea = [], []
        for r in (2 * pr, 2 * pr + 1):
            acol = jnp.broadcast_to(acum[:, r:r + 1], (L, L))
            dec = jnp.exp(jnp.where(tril, acol - acumT[r:r + 1, :], -jnp.inf))
            yd.append(jnp.dot((cb * dec * dtT[r:r + 1, :]).astype(BF16), xp, preferred_element_type=F32))
            ea.append(jnp.exp(acol))
        ys.append(pair(yd[0], yd[1]) + yoff[:, sl] * pair(ea[0], ea[1]) + dsk[:, sl] * xp.astype(F32))

    a_last = acumT[:, L - 1:L]
    wT = dtT * jnp.exp(a_last - acumT)
    cdT = jnp.exp(a_last)
    xT = xb.astype(F32).T
    xds = jnp.concatenate([xT[r * P:(r + 1) * P, :] * wT[r:r + 1, :] for r in range(n_heads)], axis=0)
    cd = jnp.concatenate([jnp.broadcast_to(cdT[r:r + 1, :], (P, h_scr.shape[1])) for r in range(n_heads)], axis=0)
    st = jnp.dot(xds.astype(BF16), bmb, preferred_element_type=F32)
    h_scr[...] = h_scr[...] * cd + st

    y = jnp.concatenate(ys, axis=1) * z_ref[rows, :].astype(F32)
    y_ref[rows, :] = _rms(y, nw_ref[...]).astype(y_ref.dtype)


def _ssd(xbc, zs, acum_g, dt_t, acum_t, d_skip, norm_w, B, S, rows=2048):
    T = B * S
    L, G, N = rows, SSM_GROUPS, SSM_STATE
    R = SSM_HEADS // G
    gw = D_INNER // G
    nc = S // L
    dsk = jnp.repeat(d_skip, SSM_HEAD_DIM).reshape(1, D_INNER)
    nw = norm_w.reshape(1, D_INNER)

    row = lambda b, g, c: b * nc + c
    in_specs = [
        pl.BlockSpec((L, gw), lambda b, g, c: (row(b, g, c), g)),
        pl.BlockSpec((L, N), lambda b, g, c: (row(b, g, c), D_INNER // N + g)),
        pl.BlockSpec((L, N), lambda b, g, c: (row(b, g, c), (D_INNER + SSM_BC) // N + g)),
        pl.BlockSpec((L, gw), lambda b, g, c: (row(b, g, c), g)),
        pl.BlockSpec((1, 1, L, R), lambda b, g, c: (b, g, c, 0)),
        pl.BlockSpec((1, R, L), lambda b, g, c: (b, g, c)),
        pl.BlockSpec((1, R, L), lambda b, g, c: (b, g, c)),
        pl.BlockSpec((1, gw), lambda b, g, c: (0, g)),
        pl.BlockSpec((1, gw), lambda b, g, c: (0, g)),
    ]
    return pl.pallas_call(
        _ssd_kernel,
        grid=(B, G, nc),
        in_specs=in_specs,
        out_specs=pl.BlockSpec((L, gw), lambda b, g, c: (row(b, g, c), g)),
        out_shape=jax.ShapeDtypeStruct((T, D_INNER), BF16),
        scratch_shapes=[pltpu.VMEM((gw, N), F32)],
        compiler_params=_cparams(("parallel", "parallel", "arbitrary")),
        name="ssd",
    )(xbc, xbc, xbc, zs, acum_g, acum_t, dt_t, dsk, nw)


def _compress_kernel(k_ref, v_ref, pek_ref, pev_ref, wk1_ref, wv1_ref, wk2_ref, wv2_ref, kc_ref, vct_ref, xf):
    d = ATTN_HEAD_DIM
    nsub = xf.shape[0] // CMP_STRIDE

    def mlp(src_ref, pe_ref, w1_ref, w2_ref):
        xf[...] = src_ref[...].astype(F32)
        u = jnp.zeros((nsub, d), F32)
        v = jnp.zeros((nsub, d), F32)
        for l in range(CMP_STRIDE):
            tok = xf[pl.ds(l, nsub, stride=CMP_STRIDE), :]
            l2 = CMP_STRIDE + l
            u = u + jnp.dot((tok + pe_ref[l:l + 1, :]).astype(BF16), w1_ref[l * d:(l + 1) * d, :],
                            preferred_element_type=F32)
            v = v + jnp.dot((tok + pe_ref[l2:l2 + 1, :]).astype(BF16), w1_ref[l2 * d:(l2 + 1) * d, :],
                            preferred_element_type=F32)
        pre = u + pltpu.roll(v, nsub - 1, 0)
        return jnp.dot(_silu(pre).astype(BF16), w2_ref[...], preferred_element_type=F32)

    kc_ref[0, 0] = mlp(k_ref, pek_ref, wk1_ref, wk2_ref).astype(kc_ref.dtype)
    vct_ref[0, 0] = mlp(v_ref, pev_ref, wv1_ref, wv2_ref).T.astype(vct_ref.dtype)


def _compress(qkv, pe_k, pe_v, wk1, wv1, wk2, wv2, B, S):
    G, d = ATTN_KV_GROUPS, ATTN_HEAD_DIM
    nsub = S // CMP_STRIDE
    full = lambda shape: pl.BlockSpec(shape, lambda b, g: (0, 0))
    out = pl.BlockSpec((1, 1, nsub, d), lambda b, g: (b, g, 0, 0))
    return pl.pallas_call(
        _compress_kernel,
        grid=(B, G),
        in_specs=[pl.BlockSpec((S, d), lambda b, g: (b, QKV_KC + g)),
                  pl.BlockSpec((S, d), lambda b, g: (b, QKV_VC + g)),
                  full((CMP_BLOCK, d)), full((CMP_BLOCK, d)),
                  full((CMP_BLOCK * d, d)), full((CMP_BLOCK * d, d)),
                  full((d, d)), full((d, d))],
        out_specs=[out, out],
        out_shape=[jax.ShapeDtypeStruct((B, G, nsub, d), BF16),
                   jax.ShapeDtypeStruct((B, G, d, nsub), BF16)],
        scratch_shapes=[pltpu.VMEM((S, d), F32)],
        compiler_params=_cparams(("parallel", "parallel")),
        name="compress",
    )(qkv, qkv, pe_k, pe_v, wk1.astype(BF16), wv1.astype(BF16), wk2.astype(BF16), wv2.astype(BF16))


def _staged(n, scores, probs, values):
    for step in range(n + 2):
        if step < n:
            scores(step)
        if 1 <= step <= n:
            probs(step - 1)
        if step >= 2:
            values(step - 2)


def _nsa_kernel(q_ref, kc_ref, vct_ref, ks_ref, vs_ref, kw_ref, vw_ref, gt_ref, o_ref,
                vst, vwt, qt, m_scr, acc_scr, out_scr, s_scr, p_scr, a_scr, *, tq, tks, n_cmp, n_blk):
    i = pl.program_id(2)
    d = ATTN_HEAD_DIM
    R = ATTN_REP
    heads = [slice(r * tq, (r + 1) * tq) for r in range(R)]

    def transposed(v):
        return v.astype(F32).T.astype(BF16)

    @pl.when(i == 0)
    def _():
        for c in range(vst.shape[0]):
            vst[c, 0:d, :] = transposed(vs_ref[c * tks:(c + 1) * tks, :])
            vst[c, d:, :] = jnp.ones((ONES_ROWS, tks), BF16)
        vwt[0:d, :] = transposed(vw_ref[...])
        vwt[d:, :] = jnp.ones((ONES_ROWS, vwt.shape[1]), BF16)

    for r in range(R):
        qt[:, heads[r]] = transposed(q_ref[:, r * d:(r + 1) * d])
    gate = jax.nn.sigmoid(gt_ref[0])

    nk = kc_ref.shape[2]
    kc = kc_ref[0, 0]
    vct = vct_ref[0, 0]
    n_idx = lax.broadcasted_iota(jnp.int32, (nk, tq), 0)
    t_cmp = i * tq + lax.broadcasted_iota(jnp.int32, (nk, tq), 1)
    cmask = (n_idx * CMP_STRIDE + CMP_BLOCK - 1 <= t_cmp) & (n_idx < n_cmp)
    psum = jnp.zeros((nk, tq), F32)
    for r in range(R):
        s = jnp.where(cmask, jnp.dot(kc, qt[:, heads[r]], preferred_element_type=F32), NEG_INF)
        e = jnp.exp2(s - jnp.max(s, axis=0, keepdims=True))
        p = jnp.where(cmask, e * (1.0 / jnp.sum(e, axis=0, keepdims=True)), 0.0)
        out_scr[:, heads[r]] = gate[r:r + 1, :] * jnp.dot(vct, p.astype(BF16), preferred_element_type=F32)
        psum = psum + p
    bi = lax.broadcasted_iota(jnp.int32, (n_blk, nk), 0)
    ni = lax.broadcasted_iota(jnp.int32, (n_blk, nk), 1)
    ovl = ((ni * CMP_STRIDE < bi * SEL_BLOCK + SEL_BLOCK) & (ni * CMP_STRIDE + CMP_BLOCK > bi * SEL_BLOCK)
           & (ni < n_cmp))
    imp = jnp.dot(jnp.where(ovl, 1.0, 0.0), psum, preferred_element_type=F32, precision=lax.Precision.HIGHEST)
    blk = lax.broadcasted_iota(jnp.int32, (n_blk, tq), 0)
    cur = (i * tq + lax.broadcasted_iota(jnp.int32, (n_blk, tq), 1)) // SEL_BLOCK
    forced = (blk == 0) | (blk == cur) | (blk == cur - 1)
    score = jnp.where(forced, FORCE_SCORE, jnp.where(blk <= cur, imp, NEG_INF))
    rank = jnp.zeros((n_blk, tq), F32)
    for k in range(n_blk):
        sk = score[k:k + 1, :]
        rank = rank + jnp.where((sk > score) | ((sk == score) & (k < blk)), 1.0, 0.0)
    sel = jnp.where(rank < N_SEL, 1.0, 0.0).astype(BF16)

    m_scr[...] = jnp.full(m_scr.shape, NEG_INF, F32)
    acc_scr[...] = jnp.zeros_like(acc_scr)
    n_chunks = (i * tq + tq - 1) // tks + 1

    def sel_inputs(c):
        k0 = pl.multiple_of(c * tks, tks)
        key = k0 + lax.broadcasted_iota(jnp.int32, (tks, tq), 0)
        t = i * tq + lax.broadcasted_iota(jnp.int32, (tks, tq), 1)
        in_blk = (lax.broadcasted_iota(jnp.int32, (tks, n_blk), 1)
                  == (k0 + lax.broadcasted_iota(jnp.int32, (tks, n_blk), 0)) // SEL_BLOCK)
        picked = jnp.dot(jnp.where(in_blk, 1.0, 0.0).astype(BF16), sel, preferred_element_type=F32)
        return ks_ref[pl.ds(k0, tks), :], jnp.where((picked > 0.5) & (key <= t), 0.0, NEG_INF)

    def sel_scores(r, k_chunk, bias):
        s = jnp.dot(k_chunk, qt[:, heads[r]], preferred_element_type=F32) + bias
        s_scr[r, 0:tks, :] = s
        m_prev = m_scr[r:r + 1, :]
        m_new = jnp.maximum(m_prev, jnp.max(s, axis=0, keepdims=True))
        m_scr[r:r + 1, :] = m_new
        return jnp.exp2(m_prev - m_new)

    def sel_probs(r):
        p_scr[r, 0:tks, :] = jnp.exp2(s_scr[r, 0:tks, :] - m_scr[r:r + 1, :]).astype(BF16)

    def sel_values(r, vt_chunk):
        acc_scr[:, heads[r]] = (a_scr[r:r + 1, :] * acc_scr[:, heads[r]]
                                + jnp.dot(vt_chunk, p_scr[r, 0:tks, :], preferred_element_type=F32))

    k_chunk, bias = sel_inputs(0)
    for r in range(R):
        a_scr[r:r + 1, :] = sel_scores(r, k_chunk, bias)

    def sel_body(c, carry):
        k_chunk, bias = sel_inputs(c)
        vt_prev = vst[c - 1]
        sel_probs(0)
        for r in range(R):
            alpha = sel_scores(r, k_chunk, bias)
            if r + 1 < R:
                sel_probs(r + 1)
            sel_values(r, vt_prev)
            a_scr[r:r + 1, :] = alpha
        return carry

    lax.fori_loop(1, n_chunks, sel_body, 0)
    span = WINDOW + tq
    k0 = pl.multiple_of(jnp.maximum(i * tq - WINDOW, 0), tq)
    dist = (i * tq + lax.broadcasted_iota(jnp.int32, (span, tq), 1)
            - (k0 + lax.broadcasted_iota(jnp.int32, (span, tq), 0)))
    bias = jnp.where((dist >= 0) & (dist < WINDOW), 0.0, NEG_INF)
    k_win = kw_ref[pl.ds(k0, span), :]
    vt_win = vwt[:, pl.ds(k0, span)]
    mx = [None] * R

    def scores(r):
        s = jnp.dot(k_win, qt[:, heads[r]], preferred_element_type=F32) + bias
        s_scr[r] = s
        mx[r] = jnp.max(s, axis=0, keepdims=True)

    def probs(r):
        p_scr[r] = jnp.exp2(s_scr[r] - mx[r]).astype(BF16)

    def values(r):
        o = jnp.dot(vt_win, p_scr[r], preferred_element_type=F32)
        out_scr[:, heads[r]] += gate[2 * R + r:2 * R + r + 1, :] * (o[0:d, :] * (1.0 / o[d:d + 1, :]))

    vt_last = vst[n_chunks - 1]
    sel_probs(0)
    for r in range(R):
        if r + 1 < R:
            sel_probs(r + 1)
        sel_values(r, vt_last)
        scores(r)
    for r in range(R):
        o = acc_scr[0:d, heads[r]] * (1.0 / acc_scr[d:d + 1, heads[r]])
        out_scr[:, heads[r]] += gate[R + r:R + r + 1, :] * o
    probs(0)
    for r in range(R):
        if r + 1 < R:
            probs(r + 1)
        values(r)

    for r in range(R):
        o_ref[:, r * d:(r + 1) * d] = out_scr[:, heads[r]].T.astype(o_ref.dtype)


def _nsa(qkv, kc, vct, small_t, B, S, tq=256, tks=512):
    assert WINDOW % tq == 0 and S >= WINDOW + tq
    T = B * S
    G, d, R = ATTN_KV_GROUPS, ATTN_HEAD_DIM, ATTN_REP
    nq = S // tq
    nk = kc.shape[2]
    n_cmp = S // CMP_STRIDE - CMP_BLOCK // CMP_STRIDE + 1
    n_blk = S // SEL_BLOCK
    seq = lambda c0: pl.BlockSpec((S, d), lambda b, g, i: (b, c0 + g))
    cmp_spec = pl.BlockSpec((1, 1, nk, d), lambda b, g, i: (b, g, 0, 0))
    gate0 = SSM_HEADS // GATE_ROWS
    return pl.pallas_call(
        functools.partial(_nsa_kernel, tq=tq, tks=tks, n_cmp=n_cmp, n_blk=n_blk),
        grid=(B, G, nq),
        in_specs=[pl.BlockSpec((tq, KV_WIDTH), lambda b, g, i: (b * nq + i, g)),
                  cmp_spec, cmp_spec,
                  seq(QKV_KS), seq(QKV_VS), seq(QKV_KW), seq(QKV_VW),
                  pl.BlockSpec((1, GATE_ROWS, tq), lambda b, g, i: (b, gate0 + g, i))],
        out_specs=pl.BlockSpec((tq, KV_WIDTH), lambda b, g, i: (b * nq + i, g)),
        out_shape=jax.ShapeDtypeStruct((T, ATTN_WIDTH), BF16),
        scratch_shapes=[pltpu.VMEM((S // tks, d + ONES_ROWS, tks), BF16),
                        pltpu.VMEM((d + ONES_ROWS, S), BF16),
                        pltpu.VMEM((d, R * tq), BF16),
                        pltpu.VMEM((8, tq), F32),
                        pltpu.VMEM((d + ONES_ROWS, R * tq), F32),
                        pltpu.VMEM((d, R * tq), F32),
                        pltpu.VMEM((R, WINDOW + tq, tq), F32),
                        pltpu.VMEM((R, WINDOW + tq, tq), BF16),
                        pltpu.VMEM((8, tq), F32)],
        compiler_params=_cparams(("parallel", "parallel", "arbitrary")),
        name="nsa",
    )(qkv, kc, vct, qkv, qkv, qkv, qkv, small_t)


def _merge_kernel(ys_ref, ws_ref, ya_ref, wa_ref, gs_ref, ga_ref, o_ref, wsb, wab):
    _cast_per_column_tile(((ws_ref, wsb), (wa_ref, wab)))
    s = jnp.dot(ys_ref[...], wsb[...], preferred_element_type=F32)
    a = jnp.dot(ya_ref[...], wab[...], preferred_element_type=F32)
    o_ref[...] = (gs_ref[...].astype(F32) * s + ga_ref[...].astype(F32) * a).astype(o_ref.dtype)


def _merge(y_ssm, w_s, y_attn, w_a, gm, tm=512, tn=512):
    T = y_ssm.shape[0]
    return pl.pallas_call(
        _merge_kernel,
        grid=(D_MODEL // tn, T // tm),
        in_specs=[pl.BlockSpec((tm, D_INNER), lambda n, m: (m, 0)),
                  pl.BlockSpec((D_INNER, tn), lambda n, m: (0, n)),
                  pl.BlockSpec((tm, ATTN_WIDTH), lambda n, m: (m, 0)),
                  pl.BlockSpec((ATTN_WIDTH, tn), lambda n, m: (0, n)),
                  pl.BlockSpec((tm, tn), lambda n, m: (m, n)),
                  pl.BlockSpec((tm, tn), lambda n, m: (m, D_MODEL // tn + n))],
        out_specs=pl.BlockSpec((tm, tn), lambda n, m: (m, n)),
        out_shape=jax.ShapeDtypeStruct((T, D_MODEL), BF16),
        scratch_shapes=[pltpu.VMEM((D_INNER, tn), BF16), pltpu.VMEM((ATTN_WIDTH, tn), BF16)],
        compiler_params=_cparams(("parallel", "arbitrary")),
        name="merge",
    )(y_ssm, w_s, y_attn, w_a, gm, gm)


def _cast_once(src, dst):
    @pl.when(pl.program_id(0) == 0)
    def _():
        dst[...] = src[...].astype(BF16)


def _mix_out_kernel(mg_ref, w_ref, x_ref, nw_ref, x1_ref, h_ref, wb):
    _cast_once(w_ref, wb)
    x1 = x_ref[...] + jnp.dot(mg_ref[...], wb[...], preferred_element_type=F32)
    x1_ref[...] = x1
    h_ref[...] = _rms(x1, nw_ref[...]).astype(h_ref.dtype)


def _mix_out(merged, w, x2d, norm_w, tm=512):
    T = x2d.shape[0]
    return pl.pallas_call(
        _mix_out_kernel,
        grid=(T // tm,),
        in_specs=[pl.BlockSpec((tm, D_MODEL), lambda m: (m, 0)),
                  _resident((D_MODEL, D_MODEL)),
                  pl.BlockSpec((tm, D_MODEL), lambda m: (m, 0)),
                  pl.BlockSpec((1, D_MODEL), lambda m: (0, 0))],
        out_specs=[pl.BlockSpec((tm, D_MODEL), lambda m: (m, 0)),
                   pl.BlockSpec((tm, D_MODEL), lambda m: (m, 0))],
        out_shape=[jax.ShapeDtypeStruct((T, D_MODEL), F32),
                   jax.ShapeDtypeStruct((T, D_MODEL), BF16)],
        scratch_shapes=[pltpu.VMEM((D_MODEL, D_MODEL), BF16)],
        compiler_params=_cparams(("arbitrary",)),
        name="mix_out",
    )(merged, w, x2d, norm_w)


def _ffn_up_kernel(h_ref, wg_ref, wu_ref, cw_ref, cb_ref, o_ref, gbuf, wgb, wub, *, tiles_per_seq):
    m = pl.program_id(1)
    _cast_per_column_tile(((wg_ref, wgb), (wu_ref, wub)))

    @pl.when(m % tiles_per_seq == 0)
    def _():
        gbuf[...] = jnp.zeros_like(gbuf)

    h = h_ref[...]
    g = jnp.dot(h, wgb[...], preferred_element_type=F32)
    u = jnp.dot(h, wub[...], preferred_element_type=F32)
    o_ref[...] = (_silu(_causal_conv(g, gbuf, cw_ref, cb_ref)) * u).astype(o_ref.dtype)


def _ffn_up(h, wg, wu, conv_w, conv_b, S, tm=1024, tn=512):
    T = h.shape[0]
    return pl.pallas_call(
        functools.partial(_ffn_up_kernel, tiles_per_seq=S // tm),
        grid=(D_FF // tn, T // tm),
        in_specs=[pl.BlockSpec((tm, D_MODEL), lambda n, m: (m, 0)),
                  pl.BlockSpec((D_MODEL, tn), lambda n, m: (0, n)),
                  pl.BlockSpec((D_MODEL, tn), lambda n, m: (0, n)),
                  pl.BlockSpec((FFN_CONV, tn), lambda n, m: (0, n)),
                  pl.BlockSpec((1, tn), lambda n, m: (0, n))],
        out_specs=pl.BlockSpec((tm, tn), lambda n, m: (m, n)),
        out_shape=jax.ShapeDtypeStruct((T, D_FF), BF16),
        scratch_shapes=[pltpu.VMEM((TAIL, tn), F32),
                        pltpu.VMEM((D_MODEL, tn), BF16), pltpu.VMEM((D_MODEL, tn), BF16)],
        compiler_params=_cparams(("parallel", "arbitrary")),
        name="ffn_up",
    )(h, wg, wu, conv_w, conv_b.reshape(1, D_FF))


def _ffn_down_kernel(a_ref, w_ref, x_ref, nw_ref, x2_ref, h_ref):
    x2 = x_ref[...] + jnp.dot(a_ref[...], w_ref[...], preferred_element_type=F32)
    x2_ref[...] = x2
    h_ref[...] = _rms(x2, nw_ref[...]).astype(h_ref.dtype)


def _ffn_down(act, w, x1, norm_w, tm=256):
    T = x1.shape[0]
    return pl.pallas_call(
        _ffn_down_kernel,
        grid=(T // tm,),
        in_specs=[pl.BlockSpec((tm, D_FF), lambda m: (m, 0)),
                  _resident((D_FF, D_MODEL)),
                  pl.BlockSpec((tm, D_MODEL), lambda m: (m, 0)),
                  _resident((1, D_MODEL))],
        out_specs=[pl.BlockSpec((tm, D_MODEL), lambda m: (m, 0)),
                   pl.BlockSpec((tm, D_MODEL), lambda m: (m, 0))],
        out_shape=[jax.ShapeDtypeStruct((T, D_MODEL), F32),
                   jax.ShapeDtypeStruct((T, D_MODEL), BF16)],
        compiler_params=_cparams(("parallel",)),
        name="ffn_down",
    )(act, w, x1, norm_w)


def _ple_kernel(h_ref, wg_ref, p_ref, wp_ref, x_ref, nw_ref, o_ref, wgb):
    _cast_once(wg_ref, wgb)
    gate = jax.nn.sigmoid(jnp.dot(h_ref[...], wgb[...], preferred_element_type=F32))
    emb = jnp.dot(p_ref[...].astype(BF16), wp_ref[...], preferred_element_type=F32)
    o_ref[...] = _rms(x_ref[...] + gate * emb, nw_ref[...])


def _ple(h, wg, p2d, wp, x2, norm_w, tm=512):
    T = x2.shape[0]
    return pl.pallas_call(
        _ple_kernel,
        grid=(T // tm,),
        in_specs=[pl.BlockSpec((tm, D_MODEL), lambda m: (m, 0)),
                  _resident((D_MODEL, D_MODEL)),
                  pl.BlockSpec((tm, PLE_DIM), lambda m: (m, 0)),
                  _resident((PLE_DIM, D_MODEL)),
                  pl.BlockSpec((tm, D_MODEL), lambda m: (m, 0)),
                  pl.BlockSpec((1, D_MODEL), lambda m: (0, 0))],
        out_specs=pl.BlockSpec((tm, D_MODEL), lambda m: (m, 0)),
        out_shape=jax.ShapeDtypeStruct((T, D_MODEL), F32),
        scratch_shapes=[pltpu.VMEM((D_MODEL, D_MODEL), BF16)],
        compiler_params=_cparams(("arbitrary",)),
        name="ple",
    )(h, wg, p2d, wp, x2, norm_w)


def _mixer_inputs(x2d, norm_w, w, ssm_conv_w, ssm_conv_b, dt_bias, a_log, B, S):
    wt = w.T
    h, small_t, acum_g, dt_t, acum_t = _norm_small(x2d, norm_w.reshape(1, D_MODEL), _small_weight(wt),
                                                   dt_bias, a_log, B, S)
    zs = _proj("silu", h, wt, 0, D_INNER, S)
    xbc = _proj("conv", h, wt, D_INNER, SSM_CONV_DIM, S,
                extra=(ssm_conv_w, ssm_conv_b.reshape(1, SSM_CONV_DIM)))
    qkv = _proj("rope", h, wt, NAT_Q, QKV_WIDTH, S, extra=_rope_tables(S))
    gm = _proj("sigmoid", h, wt, NAT_GM, 2 * D_MODEL, S)
    return small_t, (acum_g, dt_t, acum_t), zs, xbc, qkv, gm


def _layer(x2d, p2d, B, S, norm_mix_w, w_in, ssm_conv_w, ssm_conv_b, ssm_dt_bias, ssm_a_log, ssm_d,
           ssm_norm_w, cmp_pe_k, cmp_pe_v, cmp_wk1, cmp_wk2, cmp_wv1, cmp_wv2,
           w_ssm_branch, w_attn_branch, w_mix_out, norm_ffn_w, ffn_w_gate, ffn_w_up,
           ffn_conv_w, ffn_conv_b, ffn_w_down, ple_norm_w, ple_w_gate, ple_w_proj, out_norm_w):
    small_t, scalars, zs, xbc, qkv, gm = _mixer_inputs(x2d, norm_mix_w, w_in, ssm_conv_w, ssm_conv_b,
                                                       ssm_dt_bias, ssm_a_log, B, S)
    y_ssm = _ssd(xbc, zs, *scalars, ssm_d, ssm_norm_w, B, S)
    kc, vct = _compress(qkv, cmp_pe_k, cmp_pe_v, cmp_wk1, cmp_wv1, cmp_wk2, cmp_wv2, B, S)
    y_attn = _nsa(qkv, kc, vct, small_t, B, S)

    merged = _merge(y_ssm, w_ssm_branch, y_attn, w_attn_branch, gm)
    x1, h2 = _mix_out(merged, w_mix_out, x2d, norm_ffn_w.reshape(1, D_MODEL))

    act = _ffn_up(h2, ffn_w_gate, ffn_w_up, ffn_conv_w, ffn_conv_b, S)
    x2, h3 = _ffn_down(act, ffn_w_down.astype(BF16), x1, ple_norm_w.reshape(1, D_MODEL))
    return _ple(h3, ple_w_gate, p2d, ple_w_proj.astype(BF16), x2, out_norm_w.reshape(1, D_MODEL))


def kernel(x, p, norm_mix_w, w_in, ssm_conv_w, ssm_conv_b, ssm_dt_bias, ssm_a_log, ssm_d, ssm_norm_w, cmp_pe_k, cmp_pe_v, cmp_wk1, cmp_wk2, cmp_wv1, cmp_wv2, w_ssm_branch, w_attn_branch, w_mix_out, norm_ffn_w, ffn_w_gate, ffn_w_up, ffn_conv_w, ffn_conv_b, ffn_w_down, ple_norm_w, ple_w_gate, ple_w_proj, final_norm_w):
    B, S, D = x.shape
    depth = w_in.shape[0]
    assert depth == 1, "the final norm is fused into the (single) layer's last kernel"
    x2d = x.reshape(B * S, D)
    out = _layer(x2d, p[0].reshape(B * S, PLE_DIM), B, S, norm_mix_w[0], w_in[0], ssm_conv_w[0], ssm_conv_b[0],
                 ssm_dt_bias[0], ssm_a_log[0], ssm_d[0], ssm_norm_w[0], cmp_pe_k[0], cmp_pe_v[0],
                 cmp_wk1[0], cmp_wk2[0], cmp_wv1[0], cmp_wv2[0], w_ssm_branch[0], w_attn_branch[0],
                 w_mix_out[0], norm_ffn_w[0], ffn_w_gate[0], ffn_w_up[0], ffn_conv_w[0], ffn_conv_b[0],
                 ffn_w_down[0], ple_norm_w[0], ple_w_gate[0], ple_w_proj[0], final_norm_w)
    return out.reshape(B, S, D)
```
